```python
import math
import jax, jax.numpy as jnp
from jax import lax
import numpy as np

D_MODEL = 1024
BATCH = 2
SEQ = 8192
DEPTH = 4
DEC_BATCH = 128
DEC_SEQ = 8
PAST_LEN = 2048
PAGE_SIZE = 128

GDN_HEADS = 4
GDN_DK = 128
GDN_DV = 128
GDN_QKV = GDN_HEADS * (2 * GDN_DK + GDN_DV)
CONV_W = 4
GDN_CHUNK = 64
NSA_HEADS = 8
NSA_KV_HEADS = 2
NSA_HD = 64
NSA_REP = NSA_HEADS // NSA_KV_HEADS
CMP_LEN = 32
CMP_STRIDE = 16
SEL_BLOCK = 64
N_SEL = 16
WINDOW = 512
Q_BLOCK = 128
SAMPLE_Q_BLOCK = 1
NSA_KV_W = 2 * NSA_KV_HEADS * NSA_HD
MIX_W = GDN_HEADS * GDN_DV + NSA_HEADS * NSA_HD
D_FF = (8 * D_MODEL + 3 * 256 - 1) // (3 * 256) * 256
IN_SIZES = (GDN_QKV, GDN_HEADS, GDN_HEADS, GDN_HEADS * GDN_DV,
            NSA_HEADS * NSA_HD, NSA_KV_W, NSA_KV_W, NSA_KV_W, 3 * NSA_HEADS)
IN_SPLITS = tuple(int(s) for s in np.cumsum(IN_SIZES)[:-1])
N_IN = sum(IN_SIZES)
NEG_INF = -1e30
FORCE_BONUS = 1e4
EPS = 1e-6

kernel_name = "hybrid_gdn_nsa_decode_step"


def rmsnorm(x, g):
    xf = x.astype(jnp.float32)
    y = xf * lax.rsqrt(jnp.mean(xf * xf, axis=-1, keepdims=True) + EPS)
    return (y * g.astype(jnp.float32)).astype(x.dtype)


def l2norm(x):
    xf = x.astype(jnp.float32)
    return (xf * lax.rsqrt(jnp.sum(xf * xf, axis=-1, keepdims=True) + EPS)).astype(x.dtype)


def alibi_slopes(n):
    return jnp.asarray([2.0 ** (-8.0 * (h + 1) / n) for h in range(n)], jnp.float32)


def causal_conv(u, buf, w):
    up = jnp.concatenate([buf.astype(u.dtype), u], axis=1)
    T = u.shape[1]
    y = sum(up[:, j:j + T] * w[j] for j in range(CONV_W))
    return jax.nn.silu(y), up[:, -(CONV_W - 1):]


def gated_delta_rule(q, k, v, beta, g, s0):
    f32 = jnp.float32
    B, T, H, DK = q.shape
    DV = v.shape[-1]
    C = min(GDN_CHUNK, T)
    n = -(-T // C)
    pad = n * C - T

    def prep(a):
        a = jnp.pad(a.astype(f32), [(0, 0), (0, pad)] + [(0, 0)] * (a.ndim - 2))
        return jnp.swapaxes(a.reshape((B, n, C) + a.shape[2:]), 2, 3)

    qc = prep(q) * DK ** -0.5
    kc, vc, bc, gc = prep(k), prep(v), prep(beta), prep(g)
    dcy = jnp.cumsum(gc, axis=-1)
    lower = jnp.tril(jnp.ones((C, C), bool))
    strict = jnp.tril(jnp.ones((C, C), bool), -1)
    dmat = jnp.where(lower, jnp.exp(jnp.where(lower, dcy[..., :, None] - dcy[..., None, :], 0.0)), 0.0)
    kb = kc * bc[..., None]
    a_mat = jnp.where(strict, jnp.einsum('bnhid,bnhjd->bnhij', kb, kc) * dmat, 0.0)
    rhs = jnp.concatenate([vc * bc[..., None], kb * jnp.exp(dcy)[..., None]], axis=-1)
    sol = lax.linalg.triangular_solve(jnp.eye(C, dtype=f32) + a_mat, rhs, left_side=True,
                                      lower=True, unit_diagonal=True)
    u, w = sol[..., :DV], sol[..., DV:]
    attn = jnp.where(lower, jnp.einsum('bnhid,bnhjd->bnhij', qc, kc) * dmat, 0.0)
    qd = qc * jnp.exp(dcy)[..., None]
    dlast = dcy[..., -1:]
    kd = kc * jnp.exp(dlast - dcy)[..., None]
    glast = jnp.exp(dlast[..., 0])

    def step(s, xs):
        u_i, w_i, qd_i, kd_i, at_i, gl_i = xs
        v_new = u_i - jnp.einsum('bhck,bhkv->bhcv', w_i, s)
        o = jnp.einsum('bhck,bhkv->bhcv', qd_i, s) + jnp.einsum('bhij,bhjv->bhiv', at_i, v_new)
        s = s * gl_i[..., None, None] + jnp.einsum('bhck,bhcv->bhkv', kd_i, v_new)
        return s, o

    xs = tuple(jnp.moveaxis(a, 1, 0) for a in (u, w, qd, kd, attn, glast))
    s_fin, o = lax.scan(step, s0.astype(f32), xs)
    o = jnp.swapaxes(jnp.moveaxis(o, 0, 1), 2, 3).reshape(B, n * C, H, DV)[:, :T]
    return o.astype(q.dtype), s_fin.astype(s0.dtype)


def gdn_mixer(qkv_raw, b_logit, a_logit, gate, conv_buf, s0, conv_w, a_log, dt_bias, norm_g):
    B, T, _ = qkv_raw.shape
    qkv, new_buf = causal_conv(qkv_raw, conv_buf, conv_w)
    q, k, v = jnp.split(qkv, [GDN_HEADS * GDN_DK, 2 * GDN_HEADS * GDN_DK], axis=-1)
    q = l2norm(q.reshape(B, T, GDN_HEADS, GDN_DK))
    k = l2norm(k.reshape(B, T, GDN_HEADS, GDN_DK))
    v = v.reshape(B, T, GDN_HEADS, GDN_DV)
    beta = jax.nn.sigmoid(b_logit.astype(jnp.float32))
    g = -jnp.exp(a_log.astype(jnp.float32)) * jax.nn.softplus(a_logit.astype(jnp.float32) + dt_bias.astype(jnp.float32))
    o, s_new = gated_delta_rule(q, k, v, beta, g, s0)
    o = rmsnorm(o, norm_g) * jax.nn.silu(gate.reshape(B, T, GDN_HEADS, GDN_DV))
    return o.reshape(B, T, GDN_HEADS * GDN_DV), s_new, new_buf


def compress_blocks(kv, cmp_w, cmp_pe):
    B, T = kv.shape[:2]
    r = CMP_LEN // CMP_STRIDE
    n_sub = -(-T // CMP_STRIDE)
    kvp = jnp.pad(kv, ((0, 0), (0, n_sub * CMP_STRIDE - T), (0, 0), (0, 0), (0, 0)))
    sub = kvp.reshape((B, n_sub, CMP_STRIDE) + kv.shape[2:])
    nc = n_sub - r + 1
    blk = jnp.concatenate([sub[:, j:j + nc] for j in range(r)], axis=2)
    ckv = jnp.einsum('bnlcgd,lcde->bncge', blk + cmp_pe[:, :, None, :].astype(blk.dtype), cmp_w)
    c_start = jnp.arange(nc, dtype=jnp.int32) * CMP_STRIDE
    c_end = c_start + (CMP_LEN - 1)
    c_ctr = c_start.astype(jnp.float32) + 0.5 * (CMP_LEN - 1)
    return ckv, c_end, c_ctr


def sel_blocks(kv):
    B, T = kv.shape[:2]
    ns = -(-T // SEL_BLOCK)
    kvp = jnp.pad(kv, ((0, 0), (0, ns * SEL_BLOCK - T), (0, 0), (0, 0), (0, 0)))
    return kvp.reshape((B, ns, SEL_BLOCK) + kv.shape[2:])


def nsa_attend(q, qpos, ckv, c_end, c_ctr, skv, wkv, wpos, wlen, qb):
    f32 = jnp.float32
    B, T, H, D = q.shape
    G, R = NSA_KV_HEADS, NSA_REP
    nb = T // qb
    nc, ns = ckv.shape[1], skv.shape[1]
    spb = SEL_BLOCK // CMP_STRIDE
    n_sel = min(N_SEL, ns)
    scale = D ** -0.5
    slopes = alibi_slopes(H).reshape(G, R)
    kc, vc = ckv[:, :, 0], ckv[:, :, 1]
    sk = jnp.transpose(skv[:, :, :, 0], (0, 3, 1, 2, 4))
    sv = jnp.transpose(skv[:, :, :, 1], (0, 3, 1, 2, 4))
    b_ix = jnp.arange(B)[:, None, None, None]
    g_ix = jnp.arange(G)[None, :, None, None]
    blk_ix = jnp.arange(ns, dtype=jnp.int32)

    def block(args):
        qi, pi, st = args
        qg = qi.reshape(B, qb, G, R, D)
        pf = pi.astype(f32)
        s_c = jnp.einsum('bqgrd,bngd->bgrqn', qg, kc, preferred_element_type=f32) * scale
        s_c = s_c - slopes[:, :, None, None] * (pf[:, None] - c_ctr[None, :])
        ok_c = c_end[None, :] <= pi[:, None]
        p_c = jax.nn.softmax(jnp.where(ok_c, s_c, NEG_INF), axis=-1) * ok_c
        o_c = jnp.einsum('bgrqn,bngd->bqgrd', p_c.astype(vc.dtype), vc)
        imp = jnp.pad(p_c.sum(axis=2), ((0, 0), (0, 0), (0, 0), (0, ns * spb - nc)))
        imp = imp.reshape(B, G, qb, ns, spb).sum(-1)
        cur = (pi // SEL_BLOCK)[:, None]
        forced = (blk_ix == 0) | (blk_ix == cur) | (blk_ix == cur - 1)
        score = jnp.where(blk_ix <= cur, imp + jnp.where(forced, FORCE_BONUS, 0.0), NEG_INF)
        _, idx = lax.top_k(score, n_sel)
        ks_ = sk[b_ix, g_ix, idx]
        vs_ = sv[b_ix, g_ix, idx]
        dist_s = pi[:, None, None] - (idx[..., None] * SEL_BLOCK + jnp.arange(SEL_BLOCK, dtype=jnp.int32))
        s_s = jnp.einsum('bqgrd,bgqnsd->bgrqns', qg, ks_, preferred_element_type=f32) * scale
        s_s = s_s - slopes[:, :, None, None, None] * dist_s[:, :, None].astype(f32)
        s_s = jnp.where(dist_s[:, :, None] >= 0, s_s, NEG_INF).reshape(B, G, R, qb, n_sel * SEL_BLOCK)
        p_s = jax.nn.softmax(s_s, axis=-1).reshape(B, G, R, qb, n_sel, SEL_BLOCK)
        o_s = jnp.einsum('bgrqns,bgqnsd->bqgrd', p_s.astype(vs_.dtype), vs_)
        kw = lax.dynamic_slice_in_dim(wkv, st, qb + wlen, axis=1)
        pw = lax.dynamic_slice_in_dim(wpos, st, qb + wlen)
        dist_w = pi[:, None] - pw[None, :]
        ok_w = (dist_w >= 0) & (dist_w < WINDOW) & (pw >= 0)[None, :]
        s_w = jnp.einsum('bqgrd,bkgd->bgrqk', qg, kw[:, :, 0], preferred_element_type=f32) * scale
        s_w = s_w - slopes[:, :, None, None] * dist_w.astype(f32)
        p_w = jax.nn.softmax(jnp.where(ok_w, s_w, NEG_INF), axis=-1)
        o_w = jnp.einsum('bgrqk,bkgd->bqgrd', p_w.astype(kw.dtype), kw[:, :, 1])
        return jnp.stack([o_c, o_s, o_w], axis=-2)

    out = lax.map(block, (jnp.swapaxes(q.reshape(B, nb, qb, H, D), 0, 1),
                          qpos.reshape(nb, qb),
                          jnp.arange(nb, dtype=jnp.int32) * qb))
    return jnp.swapaxes(out, 0, 1).reshape(B, T, H, 3, D)


def trunk_layer(x, past_len, cmp_past, slc_past, win_buf, win_start, keep, conv_buf, s0, qb,
                g_mix, w_in, conv_w, a_log, dt_bias, gdn_g, cmp_w, cmp_pe, nsa_g,
                w_out, g_ffn, w_ffn_in, w_ffn_out):
    B, T, _ = x.shape
    h = rmsnorm(x, g_mix)
    proj = h @ w_in
    qkv_raw, b_logit, a_logit, gate, nq, ncmp, nslc, nwin, ngate = jnp.split(proj, IN_SPLITS, axis=-1)
    o_gdn, s_new, conv_new = gdn_mixer(qkv_raw, b_logit, a_logit, gate, conv_buf, s0,
                                       conv_w, a_log, dt_bias, gdn_g)
    kvr = lambda a: a.reshape(B, T, 2, NSA_KV_HEADS, NSA_HD)
    ncmp, nslc, nwin = kvr(ncmp), kvr(nslc), kvr(nwin)
    cmp_all = jnp.concatenate([cmp_past.astype(x.dtype), ncmp], axis=1)
    slc_all = jnp.concatenate([slc_past.astype(x.dtype), nslc], axis=1)
    wkv = jnp.concatenate([win_buf.astype(x.dtype), nwin], axis=1)
    wpos = win_start + jnp.arange(wkv.shape[1], dtype=jnp.int32)
    qpos = past_len + jnp.arange(T, dtype=jnp.int32)
    ckv, c_end, c_ctr = compress_blocks(cmp_all, cmp_w, cmp_pe)
    o_br = nsa_attend(nq.reshape(B, T, NSA_HEADS, NSA_HD), qpos, ckv, c_end, c_ctr,
                      sel_blocks(slc_all), wkv, wpos, win_buf.shape[1], qb)
    gates = jax.nn.sigmoid(ngate.astype(jnp.float32)).reshape(B, T, NSA_HEADS, 3)
    o_nsa = jnp.einsum('bthc,bthcd->bthd', gates.astype(o_br.dtype), o_br)
    o_nsa = rmsnorm(o_nsa, nsa_g).reshape(B, T, NSA_HEADS * NSA_HD)
    x = x + jnp.concatenate([o_gdn, o_nsa], axis=-1) @ w_out
    h = rmsnorm(x, g_ffn)
    gt, up = jnp.split(h @ w_ffn_in, 2, axis=-1)
    x = x + (jax.nn.silu(gt) * up) @ w_ffn_out
    return x, ncmp, nslc, wkv[:, -keep:], s_new, conv_new


def gather_pages(pool, page_table):
    g = pool[page_table]
    return g.reshape((page_table.shape[0], page_table.shape[1] * pool.shape[1]) + pool.shape[2:])


def setup_inputs(seed: int = 0) -> dict:
    key = jax.random.key(seed)
    ks = jax.random.split(key, 24)
    f32 = jnp.float32
    n_pages = PAST_LEN // PAGE_SIZE
    n_pool = (5 * DEC_BATCH * n_pages + 3) // 4
    w_buf = min(WINDOW, PAST_LEN)
    kv_row = (2, NSA_KV_HEADS, NSA_HD)

    def nrm(k, shape, s):
        return s * jax.random.normal(k, shape, f32)

    x_prompt = nrm(ks[0], (BATCH, SEQ, D_MODEL), 1.0)
    x_sample = nrm(ks[1], (DEC_BATCH, DEC_SEQ, D_MODEL), 1.0)
    cache_cmp_kv = nrm(ks[2], (DEPTH, n_pool, PAGE_SIZE) + kv_row, 1.0)
    cache_slc_kv = nrm(ks[3], (DEPTH, n_pool, PAGE_SIZE) + kv_row, 1.0)
    page_table = jax.random.permutation(ks[4], n_pool)[:DEC_BATCH * n_pages].reshape(DEC_BATCH, n_pages).astype(jnp.int32)
    cache_win_kv = nrm(ks[5], (DEPTH, DEC_BATCH, w_buf) + kv_row, 1.0)
    state_gdn = nrm(ks[6], (DEPTH, DEC_BATCH, GDN_HEADS, GDN_DK, GDN_DV), 0.05)
    state_conv = nrm(ks[7], (DEPTH, DEC_BATCH, CONV_W - 1, GDN_QKV), 1.0)
    norm_mix = 1.0 + nrm(ks[8], (DEPTH, D_MODEL), 0.02)
    w_in = nrm(ks[9], (DEPTH, D_MODEL, N_IN), D_MODEL ** -0.5)
    conv_w = nrm(ks[10], (DEPTH, CONV_W, GDN_QKV), CONV_W ** -0.5)
    gdn_a_log = jnp.log(jax.random.uniform(ks[11], (DEPTH, GDN_HEADS), f32, minval=1.0, maxval=16.0))
    dt = jnp.exp(jax.random.uniform(ks[12], (DEPTH, GDN_HEADS), f32,
                                    minval=math.log(1e-3), maxval=math.log(1e-1)))
    gdn_dt_bias = dt + jnp.log(-jnp.expm1(-dt))
    gdn_norm = 1.0 + nrm(ks[13], (DEPTH, GDN_DV), 0.02)
    nsa_cmp_w = nrm(ks[14], (DEPTH, CMP_LEN, 2, NSA_HD, NSA_HD), (CMP_LEN * NSA_HD) ** -0.5)
    nsa_cmp_pe = nrm(ks[15], (DEPTH, CMP_LEN, 2, NSA_HD), 0.02)
    nsa_norm = 1.0 + nrm(ks[16], (DEPTH, NSA_HD), 0.02)
    w_out = nrm(ks[17], (DEPTH, MIX_W, D_MODEL), MIX_W ** -0.5)
    norm_ffn = 1.0 + nrm(ks[18], (DEPTH, D_MODEL), 0.02)
    w_ffn_in = nrm(ks[19], (DEPTH, D_MODEL, 2 * D_FF), D_MODEL ** -0.5)
    w_ffn_out = nrm(ks[20], (DEPTH, D_FF, D_MODEL), D_FF ** -0.5)
    norm_final = 1.0 + nrm(ks[21], (D_MODEL,), 0.02)
    return {"x_prompt": x_prompt, "x_sample": x_sample,
            "cache_cmp_kv": cache_cmp_kv, "cache_slc_kv": cache_slc_kv, "page_table": page_table,
            "cache_win_kv": cache_win_kv, "state_gdn": state_gdn, "state_conv": state_conv,
            "norm_mix": norm_mix, "w_in": w_in, "conv_w": conv_w, "gdn_a_log": gdn_a_log,
            "gdn_dt_bias": gdn_dt_bias, "gdn_norm": gdn_norm, "nsa_cmp_w": nsa_cmp_w,
            "nsa_cmp_pe": nsa_cmp_pe, "nsa_norm": nsa_norm, "w_out": w_out, "norm_ffn": norm_ffn,
            "w_ffn_in": w_ffn_in, "w_ffn_out": w_ffn_out, "norm_final": norm_final}


def reference(x_prompt, x_sample, cache_cmp_kv, cache_slc_kv, page_table, cache_win_kv, state_gdn,
              state_conv, norm_mix, w_in, conv_w, gdn_a_log, gdn_dt_bias, gdn_norm, nsa_cmp_w,
              nsa_cmp_pe, nsa_norm, w_out, norm_ffn, w_ffn_in, w_ffn_out, norm_final):
    bp, tp = x_prompt.shape[0], x_prompt.shape[1]
    db = x_sample.shape[0]
    w_buf = cache_win_kv.shape[2]
    kv_row = (2, NSA_KV_HEADS, NSA_HD)
    yp, ys = x_prompt, x_sample
    cmp_p, cmp_s, slc_p, slc_s, win_p, win_s, gdn_p, gdn_s, conv_p, conv_s = ([] for _ in range(10))
    for l in range(DEPTH):
        wts = (norm_mix[l], w_in[l], conv_w[l], gdn_a_log[l], gdn_dt_bias[l], gdn_norm[l],
               nsa_cmp_w[l], nsa_cmp_pe[l], nsa_norm[l], w_out[l], norm_ffn[l], w_ffn_in[l], w_ffn_out[l])
        empty = jnp.zeros((bp, 0) + kv_row, x_prompt.dtype)
        yp, c1, s1, w1, g1, v1 = trunk_layer(
            yp, 0, empty, empty, jnp.zeros((bp, WINDOW) + kv_row, x_prompt.dtype), -WINDOW,
            min(WINDOW, tp), jnp.zeros((bp, CONV_W - 1, GDN_QKV), x_prompt.dtype),
            jnp.zeros((bp, GDN_HEADS, GDN_DK, GDN_DV), x_prompt.dtype), Q_BLOCK, *wts)
        cmp_past = gather_pages(cache_cmp_kv[l], page_table)
        slc_past = gather_pages(cache_slc_kv[l], page_table)
        past_len = cmp_past.shape[1]
        ys, c2, s2, w2, g2, v2 = trunk_layer(
            ys, past_len, cmp_past, slc_past, cache_win_kv[l], past_len - w_buf, w_buf,
            state_conv[l], state_gdn[l], SAMPLE_Q_BLOCK, *wts)
        cmp_p.append(c1); cmp_s.append(c2); slc_p.append(s1); slc_s.append(s2)
        win_p.append(w1); win_s.append(w2); gdn_p.append(g1); gdn_s.append(g2)
        conv_p.append(v1); conv_s.append(v2)
    y_prompt = rmsnorm(yp, norm_final)
    y_sample = rmsnorm(ys, norm_final)
    return (y_prompt, y_sample,
            jnp.stack(cmp_p), jnp.stack(cmp_s), jnp.stack(slc_p), jnp.stack(slc_s),
            jnp.stack(win_p), jnp.stack(win_s), jnp.stack(gdn_p), jnp.stack(gdn_s),
            jnp.stack(conv_p), jnp.stack(conv_s))
```

```python
import functools

import numpy as np
import jax
import jax.numpy as jnp
from jax import lax
from jax.experimental import pallas as pl
from jax.experimental.pallas import tpu as pltpu

F32 = jnp.float32
BF16 = jnp.bfloat16

D_MODEL = 1024
DEPTH = 4
PAGE_SIZE = 128
GDN_HEADS = 4
GDN_DK = 128
GDN_DV = 128
GDN_QKV = GDN_HEADS * (2 * GDN_DK + GDN_DV)
CONV_W = 4
NSA_HEADS = 8
NSA_KV_HEADS = 2
NSA_HD = 64
NSA_REP = NSA_HEADS // NSA_KV_HEADS
CMP_LEN = 32
CMP_STRIDE = 16
SEL_BLOCK = 64
N_SEL = 16
WINDOW = 512
NSA_KV_W = 2 * NSA_KV_HEADS * NSA_HD
D_FF = (8 * D_MODEL + 3 * 256 - 1) // (3 * 256) * 256
NEG_INF = -1e30
FORCE_BONUS = 1e4
EPS = 1e-6

LANE = 128
GDN_CHUNK = 128
INV_BLOCK = 16
Q_BLOCK = 128
SLC_TILE = 512
WIN_KEYS = WINDOW + Q_BLOCK
SUB_W = CMP_STRIDE * NSA_KV_W
VMEM_LIMIT = 56 * 1024 * 1024

C_QKV, C_GATE, C_NQ, C_KV, C_SMALL = 0, 1536, 2048, 2560, 3328
N_PROJ = 3456
SM_B, SM_A, SM_G = 0, 4, 8


def _params(sem):
    return pltpu.CompilerParams(dimension_semantics=sem, vmem_limit_bytes=VMEM_LIMIT)


def _dot(a, b):
    return jnp.dot(a.astype(BF16), b.astype(BF16), preferred_element_type=F32)


def _dot_nt(a, b):
    return lax.dot_general(a.astype(BF16), b.astype(BF16), (((1,), (1,)), ((), ())),
                           preferred_element_type=F32)


def _split2(x):
    hi = x.astype(BF16)
    lo = (x - hi.astype(F32)).astype(BF16)
    return hi, lo


def _split3(x):
    hi = x.astype(BF16)
    r = x - hi.astype(F32)
    mid = r.astype(BF16)
    lo = (r - mid.astype(F32)).astype(BF16)
    return hi, mid, lo


def _dotx(a, b):
    ah, al = _split2(a)
    bh, bl = _split2(b)
    d = functools.partial(jnp.dot, preferred_element_type=F32)
    return d(ah, bh) + (d(ah, bl) + d(al, bh))


def _dot01_l(m01, x):
    m = m01.astype(BF16)
    hi, mid, lo = _split3(x)
    d = functools.partial(jnp.dot, preferred_element_type=F32)
    return d(m, hi) + (d(m, mid) + d(m, lo))


def _dot01_r(x, m01):
    m = m01.astype(BF16)
    hi, mid, lo = _split3(x)
    d = functools.partial(jnp.dot, preferred_element_type=F32)
    return d(hi, m) + (d(mid, m) + d(lo, m))


def _iota(shape, dim):
    return lax.broadcasted_iota(jnp.int32, shape, dim)


def _idiv(x, d):
    sh = int(d).bit_length() - 1
    assert (1 << sh) == d
    return lax.shift_right_logical(x, jnp.int32(sh))


def _rms_rows(x, g):
    return x * lax.rsqrt(jnp.mean(x * x, axis=-1, keepdims=True) + EPS) * g


def _proj_kernel(x_ref, g_ref, w_ref, qkv_ref, gate_ref, nq_ref, cmp_ref, slc_ref, win_ref, kvb_ref, sm_ref):
    h = _rms_rows(x_ref[...], g_ref[...]).astype(BF16)
    d = functools.partial(jnp.dot, preferred_element_type=F32)
    qkv_ref[...] = d(h, w_ref[:, C_QKV:C_GATE])
    gate_ref[...] = d(h, w_ref[:, C_GATE:C_NQ])
    nq_ref[...] = d(h, w_ref[:, C_NQ:C_KV])
    kv = d(h, w_ref[:, C_KV:C_SMALL])
    cmp_ref[...] = kv[:, 0:NSA_KV_W]
    slc_ref[...] = kv[:, NSA_KV_W:2 * NSA_KV_W]
    win_ref[...] = kv[:, 2 * NSA_KV_W:3 * NSA_KV_W]
    kvb_ref[...] = kv[:, NSA_KV_W:3 * NSA_KV_W].astype(BF16)
    sm_ref[...] = d(h, w_ref[:, C_SMALL:N_PROJ])


def _proj(x, g, w, tm):
    m = x.shape[0]
    tm = min(tm, m)
    widths = (GDN_QKV, 512, 512, NSA_KV_W, NSA_KV_W, NSA_KV_W, 2 * NSA_KV_W, LANE)
    dts = (F32, F32, F32, F32, F32, F32, BF16, F32)
    return pl.pallas_call(
        _proj_kernel,
        grid=(m // tm,),
        in_specs=[pl.BlockSpec((tm, D_MODEL), lambda i: (i, 0)),
                  pl.BlockSpec((1, D_MODEL), lambda i: (0, 0)),
                  pl.BlockSpec((D_MODEL, N_PROJ), lambda i: (0, 0))],
        out_specs=[pl.BlockSpec((tm, wd), lambda i: (i, 0)) for wd in widths],
        out_shape=[jax.ShapeDtypeStruct((m, wd), dt) for wd, dt in zip(widths, dts)],
        compiler_params=_params(("parallel",)),
    )(x, g, w)


def _out_kernel(x_ref, a1_ref, a2_ref, w_ref, o_ref):
    half = w_ref.shape[0] // 2
    o_ref[...] = x_ref[...] + (_dot(a1_ref[...], w_ref[:half, :]) + _dot(a2_ref[...], w_ref[half:, :]))


def _out_proj(x, a1, a2, w, tm):
    m = x.shape[0]
    tm = min(tm, m)
    return pl.pallas_call(
        _out_kernel,
        grid=(m // tm,),
        in_specs=[pl.BlockSpec((tm, D_MODEL), lambda i: (i, 0)),
                  pl.BlockSpec((tm, 512), lambda i: (i, 0)),
                  pl.BlockSpec((tm, 512), lambda i: (i, 0)),
                  pl.BlockSpec((D_MODEL, D_MODEL), lambda i: (0, 0))],
        out_specs=pl.BlockSpec((tm, D_MODEL), lambda i: (i, 0)),
        out_shape=jax.ShapeDtypeStruct((m, D_MODEL), F32),
        compiler_params=_params(("parallel",)),
    )(x, a1, a2, w)


def _ffn_kernel(x_ref, g_ref, wg_ref, wu_ref, wo_ref, o_ref, h_scr, acc_scr):
    f = pl.program_id(1)

    @pl.when(f == 0)
    def _():
        h_scr[...] = _rms_rows(x_ref[...], g_ref[...]).astype(BF16)
        acc_scr[...] = jnp.zeros_like(acc_scr)

    h = h_scr[...]
    gt = jnp.dot(h, wg_ref[...], preferred_element_type=F32)
    up = jnp.dot(h, wu_ref[...], preferred_element_type=F32)
    act = (gt * jax.nn.sigmoid(gt)) * up
    acc_scr[...] += jnp.dot(act.astype(BF16), wo_ref[...], preferred_element_type=F32)

    @pl.when(f == pl.num_programs(1) - 1)
    def _():
        o_ref[...] = x_ref[...] + acc_scr[...]


def _ffn(x, g, w_in, w_out, tm):
    m = x.shape[0]
    tm = min(tm, m)
    nf = 2
    tf = D_FF // nf
    return pl.pallas_call(
        _ffn_kernel,
        grid=(m // tm, nf),
        in_specs=[pl.BlockSpec((tm, D_MODEL), lambda i, f: (i, 0)),
                  pl.BlockSpec((1, D_MODEL), lambda i, f: (0, 0)),
                  pl.BlockSpec((D_MODEL, tf), lambda i, f: (0, f)),
                  pl.BlockSpec((D_MODEL, tf), lambda i, f: (0, nf + f)),
                  pl.BlockSpec((tf, D_MODEL), lambda i, f: (f, 0))],
        out_specs=pl.BlockSpec((tm, D_MODEL), lambda i, f: (i, 0)),
        out_shape=jax.ShapeDtypeStruct((m, D_MODEL), F32),
        scratch_shapes=[pltpu.VMEM((tm, D_MODEL), BF16), pltpu.VMEM((tm, D_MODEL), F32)],
        compiler_params=_params(("parallel", "arbitrary")),
    )(x, g, w_in, w_in, w_out)


def _final_norm_kernel(x_ref, g_ref, o_ref):
    o_ref[...] = _rms_rows(x_ref[...], g_ref[...])


def _final_norm(x, g, tm):
    m = x.shape[0]
    tm = min(tm, m)
    return pl.pallas_call(
        _final_norm_kernel,
        grid=(m // tm,),
        in_specs=[pl.BlockSpec((tm, D_MODEL), lambda i: (i, 0)),
                  pl.BlockSpec((1, D_MODEL), lambda i: (0, 0))],
        out_specs=pl.BlockSpec((tm, D_MODEL), lambda i: (i, 0)),
        out_shape=jax.ShapeDtypeStruct((m, D_MODEL), F32),
        compiler_params=_params(("parallel",)),
    )(x, g)


def _prep_kernel(raw_ref, hist_ref, cw_ref, o_ref, ext_scr, *, tc):
    @pl.when(pl.program_id(1) == 0)
    def _():
        ext_scr[0:8, :] = hist_ref[0]

    ext_scr[8:8 + tc, :] = raw_ref[0]
    y = jnp.zeros((tc, GDN_QKV), F32)
    for j in range(CONV_W):
        y = y + ext_scr[pl.ds(8 - (CONV_W - 1) + j, tc), :] * cw_ref[j:j + 1, :]
    tail = ext_scr[tc:tc + 8, :]
    ext_scr[0:8, :] = tail
    y = y * jax.nn.sigmoid(y)
    for s in range(GDN_QKV // LANE):
        blk = y[:, s * LANE:(s + 1) * LANE]
        if s < 2 * GDN_HEADS:
            blk = blk * lax.rsqrt(jnp.sum(blk * blk, axis=-1, keepdims=True) + EPS)
            if s < GDN_HEADS:
                blk = blk * (GDN_DK ** -0.5)
        o_ref[0, :, s * LANE:(s + 1) * LANE] = blk


def _gdn_prep(raw, hist, cw, tc):
    b, t, _ = raw.shape
    return pl.pallas_call(
        functools.partial(_prep_kernel, tc=tc),
        grid=(b, t // tc),
        in_specs=[pl.BlockSpec((1, tc, GDN_QKV), lambda i, j: (i, j, 0)),
                  pl.BlockSpec((1, 8, GDN_QKV), lambda i, j: (i, 0, 0)),
                  pl.BlockSpec((8, GDN_QKV), lambda i, j: (0, 0))],
        out_specs=pl.BlockSpec((1, tc, GDN_QKV), lambda i, j: (i, j, 0)),
        out_shape=jax.ShapeDtypeStruct((b, t, GDN_QKV), F32),
        scratch_shapes=[pltpu.VMEM((tc + 8, GDN_QKV), F32)],
        compiler_params=_params(("parallel", "arbitrary")),
    )(raw, hist, cw)


def _tri_inv(a, ri, ci):
    c = a.shape[0]
    eye = (ri == ci).astype(F32)
    bd = _idiv(ri, INV_BLOCK) == _idiv(ci, INV_BLOCK)
    ad = jnp.where(bd, a, 0.0)
    ao = a - ad
    pw = ad
    td = eye - ad
    k = 2
    while k < INV_BLOCK:
        pw = _dotx(pw, pw)
        td = _dotx(td, eye + pw)
        k *= 2
    n = _dotx(td, ao)
    tn = eye - n
    pw = n
    k = 2
    while k < c // INV_BLOCK:
        pw = _dotx(pw, pw)
        tn = _dotx(tn, eye + pw)
        k *= 2
    return _dotx(tn, td)


def _softplus(x):
    return jnp.maximum(x, 0.0) + jnp.log1p(jnp.exp(-jnp.abs(x)))


def _gdn_kernel(qkv_ref, gate_ref, sm_ref, smt_ref, alog_ref, dtb_ref, ng_ref, s0_ref,
                o_ref, sout_ref, *, ls):
    c = GDN_CHUNK
    nseq = c // ls

    @pl.when(pl.program_id(1) == 0)
    def _():
        sout_ref[...] = s0_ref[...]

    ri = _iota((c, c), 0)
    ci = _iota((c, c), 1)
    same = _idiv(ri, ls) == _idiv(ci, ls)
    lower = same & (ci <= ri)
    strict = same & (ci < ri)
    lo_m = lower.astype(F32)
    up_m = (same & (ri <= ci)).astype(F32)
    same_m = same.astype(F32)
    sm = sm_ref[...]
    smt = smt_ref[0]
    for h in range(GDN_HEADS):
        q = qkv_ref[:, h * LANE:(h + 1) * LANE]
        k = qkv_ref[:, (GDN_HEADS + h) * LANE:(GDN_HEADS + h + 1) * LANE]
        v = qkv_ref[:, (2 * GDN_HEADS + h) * LANE:(2 * GDN_HEADS + h + 1) * LANE]
        neg_a = -jnp.exp(alog_ref[h:h + 1, :])
        dtb = dtb_ref[h:h + 1, :]
        beta = jax.nn.sigmoid(jnp.broadcast_to(sm[:, SM_B + h:SM_B + h + 1], (c, LANE)))
        g_col = neg_a * _softplus(jnp.broadcast_to(sm[:, SM_A + h:SM_A + h + 1], (c, LANE)) + dtb)
        g_row = neg_a * _softplus(jnp.broadcast_to(smt[SM_A + h:SM_A + h + 1, :], (c, c)) + dtb)
        dcy = _dot01_l(lo_m, g_col)
        dcy_row = _dot01_r(g_row, up_m)
        dtot = _dot01_l(same_m, g_col)
        dm = jnp.where(lower, jnp.exp(jnp.where(lower, dcy - dcy_row, 0.0)), 0.0)
        kb = k * beta
        a_mat = jnp.where(strict, _dot_nt(kb, k) * dm, 0.0)
        edc = jnp.exp(dcy)
        t_inv = _tri_inv(a_mat, ri, ci)
        u = _dotx(t_inv, v * beta)
        w = _dotx(t_inv, kb * edc)
        attn = jnp.where(lower, _dot_nt(q, k) * dm, 0.0)
        qd = q * edc
        kd = k * jnp.exp(dtot - dcy)
        gl = jnp.exp(dtot)
        if nseq == 1:
            s_old = sout_ref[0, h]
            v_new = u - _dot(w, s_old)
            o = _dot(qd, s_old) + _dot(attn, v_new)
            sout_ref[0, h] = s_old * gl[0:1, :] + _dot(kd.T, v_new)
        else:
            vn, oq = [], []
            for s in range(nseq):
                s_old = sout_ref[s, h]
                rows = slice(s * ls, (s + 1) * ls)
                vn.append(u[rows] - _dot(w[rows], s_old))
                oq.append(_dot(qd[rows], s_old))
            v_new = jnp.concatenate(vn, axis=0)
            o = jnp.concatenate(oq, axis=0) + _dot(attn, v_new)
            kdt = kd.T
            for s in range(nseq):
                kdt_s = jnp.where(_idiv(ci, ls) == s, kdt, 0.0)
                sout_ref[s, h] = sout_ref[s, h] * gl[s * ls:s * ls + 1, :] + _dot(kdt_s, v_new)
        o = o * lax.rsqrt(jnp.mean(o * o, axis=-1, keepdims=True) + EPS) * ng_ref[...]
        gt = gate_ref[:, h * LANE:(h + 1) * LANE]
        o_ref[:, h * LANE:(h + 1) * LANE] = o * (gt * jax.nn.sigmoid(gt))


def _gdn(qkv, gate, sm, smt, alog_b, dtb_b, ng, s0, s0_off, nb, nt, ls):
    m = qkv.shape[0]
    c = GDN_CHUNK
    nseq = c // ls
    row = lambda i, j: (i * nt + j, 0)
    n_state = nb * nseq
    return pl.pallas_call(
        functools.partial(_gdn_kernel, ls=ls),
        grid=(nb, nt),
        in_specs=[pl.BlockSpec((c, GDN_QKV), row),
                  pl.BlockSpec((c, 512), row),
                  pl.BlockSpec((c, LANE), row),
                  pl.BlockSpec((1, 8, c), lambda i, j: (i * nt + j, 0, 0)),
                  pl.BlockSpec((8, LANE), lambda i, j: (0, 0)),
                  pl.BlockSpec((8, LANE), lambda i, j: (0, 0)),
                  pl.BlockSpec((1, LANE), lambda i, j: (0, 0)),
                  pl.BlockSpec((nseq, GDN_HEADS, GDN_DK, GDN_DV), lambda i, j: (s0_off + i, 0, 0, 0))],
        out_specs=[pl.BlockSpec((c, 512), row),
                   pl.BlockSpec((nseq, GDN_HEADS, GDN_DK, GDN_DV), lambda i, j: (i, 0, 0, 0))],
        out_shape=[jax.ShapeDtypeStruct((m, 512), F32),
                   jax.ShapeDtypeStruct((n_state, GDN_HEADS, GDN_DK, GDN_DV), F32)],
        compiler_params=_params(("parallel", "arbitrary")),
    )(qkv, gate, sm, smt, alog_b, dtb_b, ng, s0)


def _cmp_kernel(x_ref, pe_ref, wlo_ref, whi_ref, y_ref):
    x = x_ref[0]
    y_ref[0, :, 0:NSA_KV_W] = _dot(x + pe_ref[0, 0:1, :], wlo_ref[0])
    y_ref[0, :, NSA_KV_W:2 * NSA_KV_W] = _dot(x + pe_ref[0, 1:2, :], whi_ref[0])


def _compress(x, pe, wlo, whi, tr):
    nl, r, _ = x.shape
    tr = min(tr, r)
    return pl.pallas_call(
        _cmp_kernel,
        grid=(nl, r // tr),
        in_specs=[pl.BlockSpec((1, tr, SUB_W), lambda l, i: (l, i, 0)),
                  pl.BlockSpec((1, 2, SUB_W), lambda l, i: (l, 0, 0)),
                  pl.BlockSpec((1, SUB_W, NSA_KV_W), lambda l, i: (l, 0, 0)),
                  pl.BlockSpec((1, SUB_W, NSA_KV_W), lambda l, i: (l, 0, 0))],
        out_specs=pl.BlockSpec((1, tr, 2 * NSA_KV_W), lambda l, i: (l, i, 0)),
        out_shape=jax.ShapeDtypeStruct((nl, r, 2 * NSA_KV_W), F32),
        compiler_params=_params(("parallel", "parallel")),
    )(x, pe, wlo, whi)


def _slope(h):
    return 2.0 ** (-8.0 * (h + 1) / NSA_HEADS)


def _q_heads(q):
    return [(q[:, h * NSA_HD:(h + 1) * NSA_HD] * (NSA_HD ** -0.5)).astype(BF16) for h in range(NSA_HEADS)]


def _cmp_branch(qh, pq, ckv, ncp):
    qb = pq.shape[0]
    n_ix = _iota((1, ncp), 1)
    c_end = n_ix * CMP_STRIDE + (CMP_LEN - 1)
    c_ctr = (n_ix * CMP_STRIDE).astype(F32) + 0.5 * (CMP_LEN - 1)
    ok = c_end <= pq
    dist = pq.astype(F32) - c_ctr
    pool = (_idiv(_iota((ncp, LANE), 0), SEL_BLOCK // CMP_STRIDE) == _iota((ncp, LANE), 1)).astype(F32)
    ckv_b = ckv.astype(BF16)
    outs, imps = [], []
    for g in range(NSA_KV_HEADS):
        kc = ckv_b[:, g * NSA_HD:(g + 1) * NSA_HD]
        vc = ckv_b[:, (NSA_KV_HEADS + g) * NSA_HD:(NSA_KV_HEADS + g + 1) * NSA_HD]
        psum = jnp.zeros((qb, ncp), F32)
        for r in range(NSA_REP):
            h = g * NSA_REP + r
            s = _dot_nt(qh[h], kc) - _slope(h) * dist
            s = jnp.where(ok, s, NEG_INF)
            m = jnp.max(s, axis=-1, keepdims=True)
            e = jnp.where(ok, jnp.exp(s - m), 0.0)
            den = jnp.sum(e, axis=-1, keepdims=True)
            p = e / jnp.where(den > 0.0, den, 1.0)
            outs.append(_dot(p, vc))
            psum = psum + p
        imps.append(_dot01_r(psum, pool))
    return outs, imps


def _select_blocks(imp, pq, axis_t):
    qb = imp.shape[0]
    blk = _iota((qb, LANE), 1)
    cur = _idiv(pq, SEL_BLOCK)
    forced = (blk == 0) | (blk == cur) | (blk == cur - 1)
    score = jnp.where(blk <= cur, imp + jnp.where(forced, FORCE_BONUS, 0.0), NEG_INF)
    if axis_t:
        work = score.T
        idx = _iota(work.shape, 0).astype(F32)
        ax = 0
    else:
        work = score
        idx = blk.astype(F32)
        ax = 1
    sel = jnp.zeros(work.shape, F32)
    for _ in range(N_SEL):
        m = jnp.max(work, axis=ax, keepdims=True)
        first = jnp.min(jnp.where(work == m, idx, 2.0 * LANE), axis=ax, keepdims=True)
        hit = idx == first
        sel = jnp.where(hit, 1.0, sel)
        work = jnp.where(hit, -jnp.inf, work)
    return sel.T if axis_t else sel


def _slc_tile(qh, sel, pq, kv_tile, k0, tk, causal, m_scr, l_scr, acc_scr):
    qb = pq.shape[0]
    key = k0 + _iota((1, tk), 1)
    dist_i = pq - key
    dist = dist_i.astype(F32)
    e_tile = (_iota((LANE, tk), 0) == _idiv(k0 + _iota((LANE, tk), 1), SEL_BLOCK)).astype(BF16)
    for g in range(NSA_KV_HEADS):
        on = jnp.dot(sel[g].astype(BF16), e_tile, preferred_element_type=F32) > 0.5
        if causal:
            on = on & (dist_i >= 0)
        bias = jnp.where(on, 0.0, NEG_INF)
        kt = kv_tile[:, g * NSA_HD:(g + 1) * NSA_HD]
        vt = kv_tile[:, (NSA_KV_HEADS + g) * NSA_HD:(NSA_KV_HEADS + g + 1) * NSA_HD]
        for r in range(NSA_REP):
            h = g * NSA_REP + r
            s = _dot_nt(qh[h], kt) + (bias - _slope(h) * dist)
            m_prev = m_scr[h][:, 0:1]
            m_new = jnp.maximum(m_prev, jnp.max(s, axis=-1, keepdims=True))
            alpha = jnp.exp(m_prev - m_new)
            p = jnp.exp(s - m_new)
            l_new = alpha * l_scr[h][:, 0:1] + jnp.sum(p, axis=-1, keepdims=True)
            acc_scr[h] = alpha * acc_scr[h] + _dot(p, vt)
            m_scr[h] = jnp.broadcast_to(m_new, (qb, LANE))
            l_scr[h] = jnp.broadcast_to(l_new, (qb, LANE))


def _win_branch(qh, pq, kv, w0):
    wk = kv.shape[0]
    dist_i = pq - (w0 + _iota((1, wk), 1))
    dist = dist_i.astype(F32)
    ok = (dist_i >= 0) & (dist_i < WINDOW)
    bias = jnp.where(ok, 0.0, NEG_INF)
    outs = []
    for g in range(NSA_KV_HEADS):
        kt = kv[:, g * NSA_HD:(g + 1) * NSA_HD]
        vt = kv[:, (NSA_KV_HEADS + g) * NSA_HD:(NSA_KV_HEADS + g + 1) * NSA_HD]
        for r in range(NSA_REP):
            h = g * NSA_REP + r
            s = _dot_nt(qh[h], kt) + (bias - _slope(h) * dist)
            m = jnp.max(s, axis=-1, keepdims=True)
            p = jnp.exp(s - m)
            outs.append(_dot(p, vt) / jnp.sum(p, axis=-1, keepdims=True))
    return outs


def _nsa_finish(o_c, o_w, sm, ng_ref, o_ref, l_scr, acc_scr):
    gates = jax.nn.sigmoid(sm)
    for h in range(NSA_HEADS):
        o_s = acc_scr[h] / l_scr[h][:, 0:1]
        c0 = SM_G + 3 * h
        o = gates[:, c0:c0 + 1] * o_c[h] + gates[:, c0 + 1:c0 + 2] * o_s + gates[:, c0 + 2:c0 + 3] * o_w[h]
        o = o * lax.rsqrt(jnp.mean(o * o, axis=-1, keepdims=True) + EPS) * ng_ref[...]
        o_ref[:, h * NSA_HD:(h + 1) * NSA_HD] = o


def _slc_init(qb, m_scr, l_scr, acc_scr):
    m_scr[...] = jnp.full(m_scr.shape, NEG_INF, F32)
    l_scr[...] = jnp.zeros(l_scr.shape, F32)
    acc_scr[...] = jnp.zeros(acc_scr.shape, F32)


def _nsa_prompt_kernel(q_ref, sm_ref, y_ref, slc_ref, win_ref, ng_ref, o_ref,
                       ckv_scr, m_scr, l_scr, acc_scr, *, t_len):
    i = pl.program_id(1)
    nsub = t_len // CMP_STRIDE

    @pl.when(i == 0)
    def _():
        y = y_ref[0]
        ckv_scr[...] = y[:, 0:NSA_KV_W] + pltpu.roll(y[:, NSA_KV_W:2 * NSA_KV_W], nsub - 1, 0)

    pos0 = i * Q_BLOCK
    pq = pos0 + _iota((Q_BLOCK, 1), 0)
    qh = _q_heads(q_ref[...])
    o_c, imps = _cmp_branch(qh, pq, ckv_scr[...], nsub)
    sel = [_select_blocks(imp, pq, True) for imp in imps]

    _slc_init(Q_BLOCK, m_scr, l_scr, acc_scr)
    n_tiles = (pos0 + Q_BLOCK + SLC_TILE - 1) // SLC_TILE

    def body(kt, carry):
        k0 = pl.multiple_of(kt * SLC_TILE, SLC_TILE)
        _slc_tile(qh, sel, pq, slc_ref[0, pl.ds(k0, SLC_TILE), :], k0, SLC_TILE, False, m_scr, l_scr, acc_scr)
        return carry

    lax.fori_loop(0, n_tiles - 1, body, 0)
    k_last = pl.multiple_of((n_tiles - 1) * SLC_TILE, SLC_TILE)
    _slc_tile(qh, sel, pq, slc_ref[0, pl.ds(k_last, SLC_TILE), :], k_last, SLC_TILE, True, m_scr, l_scr, acc_scr)

    w0 = pl.multiple_of(jnp.maximum(pos0 - WINDOW, 0), Q_BLOCK)
    o_w = _win_branch(qh, pq, win_ref[0, pl.ds(w0, WIN_KEYS), :], w0)
    _nsa_finish(o_c, o_w, sm_ref[...], ng_ref, o_ref, l_scr, acc_scr)


def _nsa_prompt(nq, sm, y, kvb, ng, b, t_len):
    nqb = t_len // Q_BLOCK
    nsub = t_len // CMP_STRIDE
    row = lambda bi, i: (bi * nqb + i, 0)
    return pl.pallas_call(
        functools.partial(_nsa_prompt_kernel, t_len=t_len),
        grid=(b, nqb),
        in_specs=[pl.BlockSpec((Q_BLOCK, 512), row),
                  pl.BlockSpec((Q_BLOCK, LANE), row),
                  pl.BlockSpec((1, nsub, 2 * NSA_KV_W), lambda bi, i: (bi, 0, 0)),
                  pl.BlockSpec((1, t_len, NSA_KV_W), lambda bi, i: (bi, 0, 0)),
                  pl.BlockSpec((1, t_len, NSA_KV_W), lambda bi, i: (bi, 0, 1)),
                  pl.BlockSpec((1, NSA_HD), lambda bi, i: (0, 0))],
        out_specs=pl.BlockSpec((Q_BLOCK, 512), row),
        out_shape=jax.ShapeDtypeStruct((b * t_len, 512), F32),
        scratch_shapes=[pltpu.VMEM((nsub, NSA_KV_W), F32),
                        pltpu.VMEM((NSA_HEADS, Q_BLOCK, LANE), F32),
                        pltpu.VMEM((NSA_HEADS, Q_BLOCK, LANE), F32),
                        pltpu.VMEM((NSA_HEADS, Q_BLOCK, NSA_HD), F32)],
        compiler_params=_params(("parallel", "arbitrary")),
    )(nq, sm, y, kvb, kvb, ng)


def _nsa_sample_kernel(pt_ref, q_ref, sm_ref, ynew_ref, slcn_ref, winn_ref, winc_ref, ng_ref, *rest,
                       n_pages, past_len, dec_len):
    del pt_ref
    y_pages = rest[0:n_pages]
    s_pages = rest[n_pages:2 * n_pages]
    o_ref = rest[2 * n_pages]
    y_scr, slc_scr, win_scr, m_scr, l_scr, acc_scr = rest[2 * n_pages + 1:]
    sub_pp = PAGE_SIZE // CMP_STRIDE
    ncp = n_pages * sub_pp
    tk = slc_scr.shape[0]

    for p in range(n_pages):
        y_scr[p * sub_pp:(p + 1) * sub_pp, :] = y_pages[p][0]
    y_scr[ncp:ncp + 8, :] = ynew_ref[0]
    y = y_scr[...]
    ckv = (y[:, 0:NSA_KV_W] + pltpu.roll(y[:, NSA_KV_W:2 * NSA_KV_W], ncp + 7, 0))[0:ncp]

    for p in range(n_pages):
        slc_scr[p * PAGE_SIZE:(p + 1) * PAGE_SIZE, :] = s_pages[p][0].astype(BF16)
    pad = jnp.zeros((tk - past_len - dec_len, NSA_KV_W), F32)
    slc_scr[past_len:tk, :] = jnp.concatenate([slcn_ref[...], pad], axis=0).astype(BF16)
    wb = winc_ref.shape[1]
    win_scr[0:wb, :] = winc_ref[0].astype(BF16)
    padw = jnp.zeros((win_scr.shape[0] - wb - dec_len, NSA_KV_W), F32)
    win_scr[wb:, :] = jnp.concatenate([winn_ref[...], padw], axis=0).astype(BF16)

    pq = past_len + _iota((dec_len, 1), 0)
    qh = _q_heads(q_ref[...])
    o_c, imps = _cmp_branch(qh, pq, ckv, ncp)
    sel = [_select_blocks(imp, pq, False) for imp in imps]
    _slc_init(dec_len, m_scr, l_scr, acc_scr)
    _slc_tile(qh, sel, pq, slc_scr[...], 0, tk, True, m_scr, l_scr, acc_scr)
    o_w = _win_branch(qh, pq, win_scr[...], past_len - wb)
    _nsa_finish(o_c, o_w, sm_ref[...], ng_ref, o_ref, l_scr, acc_scr)


def _nsa_sample(page_table, nq, sm, ynew, slc_new, win_new, win_cache, ng, y_pool, slc_pool, layer, n_pool,
                past_len, dec_len):
    db, n_pages = page_table.shape
    wb = win_cache.shape[1]
    tk = past_len + PAGE_SIZE
    wk = wb + PAGE_SIZE
    sub_pp = PAGE_SIZE // CMP_STRIDE
    row = lambda b, pt: (b, 0)

    def page_map(k):
        return lambda b, pt: (layer * n_pool + pt[b, k], 0, 0)

    in_specs = [pl.BlockSpec((dec_len, 512), row),
                pl.BlockSpec((dec_len, LANE), row),
                pl.BlockSpec((1, 8, 2 * NSA_KV_W), lambda b, pt: (b, 0, 0)),
                pl.BlockSpec((dec_len, NSA_KV_W), row),
                pl.BlockSpec((dec_len, NSA_KV_W), row),
                pl.BlockSpec((1, wb, NSA_KV_W), lambda b, pt: (layer * db + b, 0, 0)),
                pl.BlockSpec((1, NSA_HD), lambda b, pt: (0, 0))]
    in_specs += [pl.BlockSpec((1, sub_pp, 2 * NSA_KV_W), page_map(k)) for k in range(n_pages)]
    in_specs += [pl.BlockSpec((1, PAGE_SIZE, NSA_KV_W), page_map(k)) for k in range(n_pages)]
    grid_spec = pltpu.PrefetchScalarGridSpec(
        num_scalar_prefetch=1,
        grid=(db,),
        in_specs=in_specs,
        out_specs=pl.BlockSpec((dec_len, 512), row),
        scratch_shapes=[pltpu.VMEM((n_pages * sub_pp + 8, 2 * NSA_KV_W), F32),
                        pltpu.VMEM((tk, NSA_KV_W), BF16),
                        pltpu.VMEM((wk, NSA_KV_W), BF16),
                        pltpu.VMEM((NSA_HEADS, dec_len, LANE), F32),
                        pltpu.VMEM((NSA_HEADS, dec_len, LANE), F32),
                        pltpu.VMEM((NSA_HEADS, dec_len, NSA_HD), F32)])
    return pl.pallas_call(
        functools.partial(_nsa_sample_kernel, n_pages=n_pages, past_len=past_len, dec_len=dec_len),
        grid_spec=grid_spec,
        out_shape=jax.ShapeDtypeStruct((db * dec_len, 512), F32),
        compiler_params=_params(("arbitrary",)),
    )(page_table, nq, sm, ynew, slc_new, win_new, win_cache, ng, *([y_pool] * n_pages), *([slc_pool] * n_pages))


def _proj_columns():
    sizes = (GDN_QKV, GDN_HEADS, GDN_HEADS, GDN_HEADS * GDN_DV, NSA_HEADS * NSA_HD,
             NSA_KV_W, NSA_KV_W, NSA_KV_W, 3 * NSA_HEADS)
    off = np.concatenate([[0], np.cumsum(sizes)])
    seg = lambda i: np.arange(off[i], off[i + 1])
    order = np.concatenate([seg(0), seg(3), seg(4), seg(5), seg(6), seg(7), seg(1), seg(2), seg(8)])
    return order, N_PROJ - order.size


def _cmp_weights(cmp_w):
    nl = cmp_w.shape[0]
    eye_g = jnp.eye(NSA_KV_HEADS, dtype=cmp_w.dtype)
    eye_c = jnp.eye(2, dtype=cmp_w.dtype)

    def half(w):
        big = jnp.einsum('nlcde,cx,gy->nlcgdxye', w, eye_c, eye_g)
        return big.reshape(nl, SUB_W, NSA_KV_W).astype(BF16)

    return half(cmp_w[:, :CMP_STRIDE]), half(cmp_w[:, CMP_STRIDE:])


def _cmp_pe(cmp_pe):
    nl = cmp_pe.shape[0]

    def half(p):
        return jnp.broadcast_to(p[:, :, :, None, :], (nl, CMP_STRIDE, 2, NSA_KV_HEADS, NSA_HD)).reshape(nl, SUB_W)

    return jnp.stack([half(cmp_pe[:, :CMP_STRIDE]), half(cmp_pe[:, CMP_STRIDE:])], axis=1)


def _small_t(sm, c):
    m = sm.shape[0]
    return jnp.transpose(sm[:, :8].reshape(m // c, c, 8), (0, 2, 1))


def kernel(x_prompt, x_sample, cache_cmp_kv, cache_slc_kv, page_table, cache_win_kv, state_gdn, state_conv,
           norm_mix, w_in, conv_w, gdn_a_log, gdn_dt_bias, gdn_norm, nsa_cmp_w, nsa_cmp_pe, nsa_norm,
           w_out, norm_ffn, w_ffn_in, w_ffn_out, norm_final):
    bp, tp, _ = x_prompt.shape
    db, ts, _ = x_sample.shape
    n_pool = cache_cmp_kv.shape[1]
    n_pages = page_table.shape[1]
    past_len = n_pages * PAGE_SIZE
    wb = cache_win_kv.shape[2]
    kv_row = (2, NSA_KV_HEADS, NSA_HD)
    assert GDN_CHUNK % ts == 0 and (db * ts) % GDN_CHUNK == 0 and tp % SLC_TILE == 0 and tp >= WIN_KEYS

    order, pad = _proj_columns()
    w_in_b = jnp.pad(w_in[:, :, order], ((0, 0), (0, 0), (0, pad))).astype(BF16)
    w_out_b = w_out.astype(BF16)
    w_ffn_in_b = w_ffn_in.astype(BF16)
    w_ffn_out_b = w_ffn_out.astype(BF16)
    cw = jnp.pad(conv_w, ((0, 0), (0, 8 - CONV_W), (0, 0)))
    alog_b = jnp.broadcast_to(jnp.pad(gdn_a_log, ((0, 0), (0, 8 - GDN_HEADS)))[:, :, None], (DEPTH, 8, LANE))
    dtb_b = jnp.broadcast_to(jnp.pad(gdn_dt_bias, ((0, 0), (0, 8 - GDN_HEADS)))[:, :, None], (DEPTH, 8, LANE))
    wlo, whi = _cmp_weights(nsa_cmp_w)
    pe2 = _cmp_pe(nsa_cmp_pe)

    pool_sub = cache_cmp_kv.reshape(DEPTH, n_pool * (PAGE_SIZE // CMP_STRIDE), SUB_W)
    y_pool = _compress(pool_sub, pe2, wlo, whi, 512).reshape(DEPTH * n_pool, PAGE_SIZE // CMP_STRIDE, 2 * NSA_KV_W)
    slc_pool = cache_slc_kv.reshape(DEPTH * n_pool, PAGE_SIZE, NSA_KV_W)
    win_cache = cache_win_kv.reshape(DEPTH * db, wb, NSA_KV_W)
    state_all = state_gdn.reshape(DEPTH * db, GDN_HEADS, GDN_DK, GDN_DV)
    seq_per_chunk = GDN_CHUNK // ts

    xp = x_prompt.reshape(bp * tp, D_MODEL)
    xs = x_sample.reshape(db * ts, D_MODEL)
    zero_hist = jnp.zeros((bp, 8, GDN_QKV), F32)
    zero_state = jnp.zeros((bp, GDN_HEADS, GDN_DK, GDN_DV), F32)
    outs = [[] for _ in range(10)]
    for l in range(DEPTH):
        g_mix = norm_mix[l][None, :]
        g_ffn = norm_ffn[l][None, :]
        gdn_g = gdn_norm[l][None, :]
        nsa_g = nsa_norm[l][None, :]

        qkv, gate, nq, ncmp, nslc, nwin, kvb, sm = _proj(xp, g_mix, w_in_b[l], 256)
        qkvn = _gdn_prep(qkv.reshape(bp, tp, GDN_QKV), zero_hist, cw[l], 256).reshape(bp * tp, GDN_QKV)
        o_gdn, s_new = _gdn(qkvn, gate, sm, _small_t(sm, GDN_CHUNK), alog_b[l], dtb_b[l], gdn_g, zero_state, 0,
                            bp, tp // GDN_CHUNK, GDN_CHUNK)
        y = _compress(ncmp.reshape(1, bp * tp // CMP_STRIDE, SUB_W), pe2[l:l + 1], wlo[l:l + 1],
                      whi[l:l + 1], 512).reshape(bp, tp // CMP_STRIDE, 2 * NSA_KV_W)
        o_nsa = _nsa_prompt(nq, sm, y, kvb.reshape(bp, tp, 2 * NSA_KV_W), nsa_g, bp, tp)
        xp = _out_proj(xp, o_gdn, o_nsa, w_out_b[l], 512)
        xp = _ffn(xp, g_ffn, w_ffn_in_b[l], w_ffn_out_b[l], 1024)
        outs[0].append(ncmp.reshape((bp, tp) + kv_row))
        outs[2].append(nslc.reshape((bp, tp) + kv_row))
        outs[4].append(nwin.reshape((bp, tp) + kv_row)[:, -min(WINDOW, tp):])
        outs[6].append(s_new)
        outs[8].append(qkv.reshape(bp, tp, GDN_QKV)[:, -(CONV_W - 1):])

        qkv, gate, nq, ncmp, nslc, nwin, kvb, sm = _proj(xs, g_mix, w_in_b[l], 256)
        hist = jnp.pad(state_conv[l], ((0, 0), (8 - (CONV_W - 1), 0), (0, 0)))
        qkvn = _gdn_prep(qkv.reshape(db, ts, GDN_QKV), hist, cw[l], ts).reshape(db * ts, GDN_QKV)
        o_gdn, s_new = _gdn(qkvn, gate, sm, _small_t(sm, GDN_CHUNK), alog_b[l], dtb_b[l], gdn_g, state_all,
                            l * (db // seq_per_chunk), db // seq_per_chunk, 1, ts)
        new_sub = jnp.pad(ncmp.reshape(db, 1, ts * NSA_KV_W), ((0, 0), (0, 7), (0, SUB_W - ts * NSA_KV_W)))
        ynew = _compress(new_sub.reshape(1, db * 8, SUB_W), pe2[l:l + 1], wlo[l:l + 1], whi[l:l + 1],
                         512).reshape(db, 8, 2 * NSA_KV_W)
        o_nsa = _nsa_sample(page_table, nq, sm, ynew, nslc, nwin, win_cache, nsa_g, y_pool, slc_pool, l, n_pool,
                            past_len, ts)
        xs = _out_proj(xs, o_gdn, o_nsa, w_out_b[l], 512)
        xs = _ffn(xs, g_ffn, w_ffn_in_b[l], w_ffn_out_b[l], 1024)
        outs[1].append(ncmp.reshape((db, ts) + kv_row))
        outs[3].append(nslc.reshape((db, ts) + kv_row))
        wkv = jnp.concatenate([cache_win_kv[l], nwin.reshape((db, ts) + kv_row)], axis=1)
        outs[5].append(wkv[:, -wb:])
        outs[7].append(s_new)
        outs[9].append(jnp.concatenate([state_conv[l], qkv.reshape(db, ts, GDN_QKV)], axis=1)[:, -(CONV_W - 1):])

    y_prompt = _final_norm(xp, norm_final[None, :], 512).reshape(bp, tp, D_MODEL)
    y_sample = _final_norm(xs, norm_final[None, :], 512).reshape(db, ts, D_MODEL)
    return (y_prompt, y_sample) + tuple(jnp.stack(o) for o in outs)
```

```python
import functools

import numpy as np
import jax
import jax.numpy as jnp
from jax import lax
from jax.experimental import pallas as pl
from jax.experimental.pallas import tpu as pltpu

F32 = jnp.float32
BF16 = jnp.bfloat16

D_MODEL = 1024
DEPTH = 4
PAGE_SIZE = 128
GDN_HEADS = 4
GDN_DK = 128
GDN_DV = 128
GDN_QKV = GDN_HEADS * (2 * GDN_DK + GDN_DV)
CONV_W = 4
NSA_HEADS = 8
NSA_KV_HEADS = 2
NSA_HD = 64
NSA_REP = NSA_HEADS // NSA_KV_HEADS
CMP_LEN = 32
CMP_STRIDE = 16
SEL_BLOCK = 64
N_SEL = 16
WINDOW = 512
NSA_KV_W = 2 * NSA_KV_HEADS * NSA_HD
D_FF = (8 * D_MODEL + 3 * 256 - 1) // (3 * 256) * 256
NEG_INF = -1e30
FORCE_BONUS = 1e4
EPS = 1e-6

LANE = 128
GDN_CHUNK = 128
INV_BLOCK = 16
Q_BLOCK = 128
SLC_TILE = 512
WIN_TILES = WINDOW // Q_BLOCK + 1
SUB_W = CMP_STRIDE * NSA_KV_W
SUB_PP = PAGE_SIZE // CMP_STRIDE
POOL_PAGES = 32
VMEM_LIMIT = 56 * 1024 * 1024

C_QKV, C_GATE, C_NQ, C_KV, C_SMALL = 0, 1536, 2048, 2560, 3328
N_PROJ = 3456
SM_B, SM_A, SM_G = 0, 4, 8

_NT = (((1,), (1,)), ((), ()))


def _params(sem):
    return pltpu.CompilerParams(dimension_semantics=sem, vmem_limit_bytes=VMEM_LIMIT)


def _dot(a, b):
    return jnp.dot(a.astype(BF16), b.astype(BF16), preferred_element_type=F32)


def _dot_nt(a, b):
    return lax.dot_general(a.astype(BF16), b.astype(BF16), _NT, preferred_element_type=F32)


def _split2(x):
    hi = x.astype(BF16)
    lo = (x - hi.astype(F32)).astype(BF16)
    return hi, lo


def _split3(x):
    hi = x.astype(BF16)
    r = x - hi.astype(F32)
    mid = r.astype(BF16)
    lo = (r - mid.astype(F32)).astype(BF16)
    return hi, mid, lo


def _dotx(a, b):
    ah, al = _split2(a)
    bh, bl = _split2(b)
    d = functools.partial(jnp.dot, preferred_element_type=F32)
    return d(ah, bh) + (d(ah, bl) + d(al, bh))


def _dot01_l(m01, x):
    m = m01.astype(BF16)
    hi, mid, lo = _split3(x)
    d = functools.partial(jnp.dot, preferred_element_type=F32)
    return d(m, hi) + (d(m, mid) + d(m, lo))


def _dot01_r(x, m01):
    m = m01.astype(BF16)
    hi, mid, lo = _split3(x)
    d = functools.partial(jnp.dot, preferred_element_type=F32)
    return d(hi, m) + (d(mid, m) + d(lo, m))


def _iota(shape, dim):
    return lax.broadcasted_iota(jnp.int32, shape, dim)


def _idiv(x, d):
    sh = int(d).bit_length() - 1
    assert (1 << sh) == d
    return lax.shift_right_logical(x, jnp.int32(sh))


def _rep(x, rep):
    return x if rep == 1 else jnp.concatenate([x] * rep, axis=0)


def _rms_rows(x, g):
    return x * lax.rsqrt(jnp.mean(x * x, axis=-1, keepdims=True) + EPS) * g


def _proj_kernel(x_ref, g_ref, w_ref, *rest, with_t):
    if with_t:
        wt_ref = rest[0]
        rest = rest[1:]
    qkv_ref, gate_ref, nq_ref, cmp_ref, slc_ref, win_ref, sm_ref = rest[:7]
    h = _rms_rows(x_ref[...], g_ref[...]).astype(BF16)
    d = functools.partial(jnp.dot, preferred_element_type=F32)
    qkv_ref[...] = d(h, w_ref[:, C_QKV:C_GATE])
    gate_ref[...] = d(h, w_ref[:, C_GATE:C_NQ])
    nq_ref[...] = d(h, w_ref[:, C_NQ:C_KV])
    kv = d(h, w_ref[:, C_KV:C_SMALL])
    cmp_ref[...] = kv[:, 0:NSA_KV_W]
    slc_ref[...] = kv[:, NSA_KV_W:2 * NSA_KV_W]
    win_ref[...] = kv[:, 2 * NSA_KV_W:3 * NSA_KV_W]
    sm_ref[...] = d(h, w_ref[:, C_SMALL:N_PROJ])
    if with_t:
        cmpt_ref, slct_ref, wint_ref, slctb_ref, wintb_ref = rest[7:]
        kvt = lax.dot_general(wt_ref[...], h, _NT, preferred_element_type=F32)
        cmpt_ref[0] = kvt[0:NSA_KV_W]
        slct = kvt[NSA_KV_W:2 * NSA_KV_W]
        wint = kvt[2 * NSA_KV_W:3 * NSA_KV_W]
        slct_ref[0] = slct
        wint_ref[0] = wint
        slctb_ref[0, 0] = slct.astype(BF16)
        for j in range(wintb_ref.shape[1]):
            wintb_ref[0, j] = wint[:, j * Q_BLOCK:(j + 1) * Q_BLOCK].astype(BF16)


def _proj(x, g, w, wt, tm, batch=None):
    m = x.shape[0]
    tm = min(tm, m)
    with_t = wt is not None
    widths = (GDN_QKV, 512, 512, NSA_KV_W, NSA_KV_W, NSA_KV_W, LANE)
    in_specs = [pl.BlockSpec((tm, D_MODEL), lambda i: (i, 0)),
                pl.BlockSpec((1, D_MODEL), lambda i: (0, 0)),
                pl.BlockSpec((D_MODEL, N_PROJ), lambda i: (0, 0))]
    out_specs = [pl.BlockSpec((tm, wd), lambda i: (i, 0)) for wd in widths]
    out_shape = [jax.ShapeDtypeStruct((m, wd), F32) for wd in widths]
    args = [x, g, w]
    if with_t:
        t_len = m // batch
        nt = t_len // tm
        assert tm == SLC_TILE and t_len % tm == 0
        in_specs.append(pl.BlockSpec((3 * NSA_KV_W, D_MODEL), lambda i: (0, 0)))
        args.append(wt)
        tmap = lambda i: (i // nt, 0, i % nt)
        tile = lambda i: (i // nt, i % nt, 0, 0)
        out_specs += [pl.BlockSpec((1, NSA_KV_W, tm), tmap)] * 3
        out_shape += [jax.ShapeDtypeStruct((batch, NSA_KV_W, t_len), F32)] * 3
        out_specs += [pl.BlockSpec((1, 1, NSA_KV_W, tm), tile),
                      pl.BlockSpec((1, tm // Q_BLOCK, NSA_KV_W, Q_BLOCK), tile)]
        out_shape += [jax.ShapeDtypeStruct((batch, nt, NSA_KV_W, tm), BF16),
                      jax.ShapeDtypeStruct((batch, t_len // Q_BLOCK, NSA_KV_W, Q_BLOCK), BF16)]
    return pl.pallas_call(
        functools.partial(_proj_kernel, with_t=with_t),
        grid=(m // tm,),
        in_specs=in_specs,
        out_specs=out_specs,
        out_shape=out_shape,
        compiler_params=_params(("parallel",)),
    )(*args)


def _out_kernel(x_ref, a1_ref, a2_ref, w_ref, o_ref):
    half = w_ref.shape[0] // 2
    o_ref[...] = x_ref[...] + (_dot(a1_ref[...], w_ref[:half, :]) + _dot(a2_ref[...], w_ref[half:, :]))


def _out_proj(x, a1, a2, w, tm):
    m = x.shape[0]
    tm = min(tm, m)
    return pl.pallas_call(
        _out_kernel,
        grid=(m // tm,),
        in_specs=[pl.BlockSpec((tm, D_MODEL), lambda i: (i, 0)),
                  pl.BlockSpec((tm, 512), lambda i: (i, 0)),
                  pl.BlockSpec((tm, 512), lambda i: (i, 0)),
                  pl.BlockSpec((D_MODEL, D_MODEL), lambda i: (0, 0))],
        out_specs=pl.BlockSpec((tm, D_MODEL), lambda i: (i, 0)),
        out_shape=jax.ShapeDtypeStruct((m, D_MODEL), F32),
        compiler_params=_params(("parallel",)),
    )(x, a1, a2, w)


def _ffn_kernel(x_ref, g_ref, wg_ref, wu_ref, wo_ref, o_ref, h_scr, acc_scr):
    f = pl.program_id(1)

    @pl.when(f == 0)
    def _():
        h_scr[...] = _rms_rows(x_ref[...], g_ref[...]).astype(BF16)
        acc_scr[...] = jnp.zeros_like(acc_scr)

    h = h_scr[...]
    gt = jnp.dot(h, wg_ref[...], preferred_element_type=F32)
    up = jnp.dot(h, wu_ref[...], preferred_element_type=F32)
    act = (gt * jax.nn.sigmoid(gt)) * up
    acc_scr[...] += jnp.dot(act.astype(BF16), wo_ref[...], preferred_element_type=F32)

    @pl.when(f == pl.num_programs(1) - 1)
    def _():
        o_ref[...] = x_ref[...] + acc_scr[...]


def _ffn(x, g, w_in, w_out, tm):
    m = x.shape[0]
    tm = min(tm, m)
    nf = 2
    tf = D_FF // nf
    return pl.pallas_call(
        _ffn_kernel,
        grid=(m // tm, nf),
        in_specs=[pl.BlockSpec((tm, D_MODEL), lambda i, f: (i, 0)),
                  pl.BlockSpec((1, D_MODEL), lambda i, f: (0, 0)),
                  pl.BlockSpec((D_MODEL, tf), lambda i, f: (0, f)),
                  pl.BlockSpec((D_MODEL, tf), lambda i, f: (0, nf + f)),
                  pl.BlockSpec((tf, D_MODEL), lambda i, f: (f, 0))],
        out_specs=pl.BlockSpec((tm, D_MODEL), lambda i, f: (i, 0)),
        out_shape=jax.ShapeDtypeStruct((m, D_MODEL), F32),
        scratch_shapes=[pltpu.VMEM((tm, D_MODEL), BF16), pltpu.VMEM((tm, D_MODEL), F32)],
        compiler_params=_params(("parallel", "arbitrary")),
    )(x, g, w_in, w_in, w_out)


def _final_norm_kernel(x_ref, g_ref, o_ref):
    o_ref[...] = _rms_rows(x_ref[...], g_ref[...])


def _final_norm(x, g, tm):
    m = x.shape[0]
    tm = min(tm, m)
    return pl.pallas_call(
        _final_norm_kernel,
        grid=(m // tm,),
        in_specs=[pl.BlockSpec((tm, D_MODEL), lambda i: (i, 0)),
                  pl.BlockSpec((1, D_MODEL), lambda i: (0, 0))],
        out_specs=pl.BlockSpec((tm, D_MODEL), lambda i: (i, 0)),
        out_shape=jax.ShapeDtypeStruct((m, D_MODEL), F32),
        compiler_params=_params(("parallel",)),
    )(x, g)


def _prep_kernel(raw_ref, hist_ref, cw_ref, o_ref, ext_scr, *, tc):
    @pl.when(pl.program_id(1) == 0)
    def _():
        ext_scr[0:8, :] = hist_ref[0]

    ext_scr[8:8 + tc, :] = raw_ref[0]
    y = jnp.zeros((tc, GDN_QKV), F32)
    for j in range(CONV_W):
        y = y + ext_scr[pl.ds(8 - (CONV_W - 1) + j, tc), :] * cw_ref[j:j + 1, :]
    tail = ext_scr[tc:tc + 8, :]
    ext_scr[0:8, :] = tail
    y = y * jax.nn.sigmoid(y)
    for s in range(GDN_QKV // LANE):
        blk = y[:, s * LANE:(s + 1) * LANE]
        if s < 2 * GDN_HEADS:
            blk = blk * lax.rsqrt(jnp.sum(blk * blk, axis=-1, keepdims=True) + EPS)
            if s < GDN_HEADS:
                blk = blk * (GDN_DK ** -0.5)
        o_ref[0, :, s * LANE:(s + 1) * LANE] = blk


def _gdn_prep(raw, hist, cw, tc):
    b, t, _ = raw.shape
    return pl.pallas_call(
        functools.partial(_prep_kernel, tc=tc),
        grid=(b, t // tc),
        in_specs=[pl.BlockSpec((1, tc, GDN_QKV), lambda i, j: (i, j, 0)),
                  pl.BlockSpec((1, 8, GDN_QKV), lambda i, j: (i, 0, 0)),
                  pl.BlockSpec((8, GDN_QKV), lambda i, j: (0, 0))],
        out_specs=pl.BlockSpec((1, tc, GDN_QKV), lambda i, j: (i, j, 0)),
        out_shape=jax.ShapeDtypeStruct((b, t, GDN_QKV), F32),
        scratch_shapes=[pltpu.VMEM((tc + 8, GDN_QKV), F32)],
        compiler_params=_params(("parallel", "arbitrary")),
    )(raw, hist, cw)


def _tri_inv(mats, ri, ci):
    c = mats[0].shape[0]
    eye = (ri == ci).astype(F32)
    bd = _idiv(ri, INV_BLOCK) == _idiv(ci, INV_BLOCK)
    ad = [jnp.where(bd, a, 0.0) for a in mats]
    ao = [a - d for a, d in zip(mats, ad)]
    pw = ad
    td = [eye - d for d in ad]
    k = 2
    while k < INV_BLOCK:
        pw = [_dotx(p, p) for p in pw]
        td = [_dotx(t, eye + p) for t, p in zip(td, pw)]
        k *= 2
    n = [_dotx(t, o) for t, o in zip(td, ao)]
    tn = [eye - x for x in n]
    pw = n
    k = 2
    while k < c // INV_BLOCK:
        pw = [_dotx(p, p) for p in pw]
        tn = [_dotx(t, eye + p) for t, p in zip(tn, pw)]
        k *= 2
    return [_dotx(a, b) for a, b in zip(tn, td)]


def _softplus(x):
    return jnp.maximum(x, 0.0) + jnp.log1p(jnp.exp(-jnp.abs(x)))


def _gdn_kernel(qkv_ref, gate_ref, sm_ref, smt_ref, alog_ref, dtb_ref, ng_ref, s0_ref,
                o_ref, sout_ref, *, ls):
    c = GDN_CHUNK
    nseq = c // ls
    heads = range(GDN_HEADS)

    @pl.when(pl.program_id(1) == 0)
    def _():
        sout_ref[...] = s0_ref[...]

    ri = _iota((c, c), 0)
    ci = _iota((c, c), 1)
    same = _idiv(ri, ls) == _idiv(ci, ls)
    lower = same & (ci <= ri)
    strict = same & (ci < ri)
    lo_m = lower.astype(F32)
    up_m = (same & (ri <= ci)).astype(F32)
    same_m = same.astype(F32)
    sm = sm_ref[...]
    smt = smt_ref[0]
    q = [qkv_ref[:, h * LANE:(h + 1) * LANE] for h in heads]
    k = [qkv_ref[:, (GDN_HEADS + h) * LANE:(GDN_HEADS + h + 1) * LANE] for h in heads]
    v = [qkv_ref[:, (2 * GDN_HEADS + h) * LANE:(2 * GDN_HEADS + h + 1) * LANE] for h in heads]
    neg_a = [-jnp.exp(alog_ref[h:h + 1, :]) for h in heads]
    dtb = [dtb_ref[h:h + 1, :] for h in heads]
    beta = [jax.nn.sigmoid(jnp.broadcast_to(sm[:, SM_B + h:SM_B + h + 1], (c, LANE))) for h in heads]
    g_col = [neg_a[h] * _softplus(jnp.broadcast_to(sm[:, SM_A + h:SM_A + h + 1], (c, LANE)) + dtb[h])
             for h in heads]
    g_row = [neg_a[h] * _softplus(jnp.broadcast_to(smt[SM_A + h:SM_A + h + 1, :], (c, c)) + dtb[h])
             for h in heads]
    dcy = [_dot01_l(lo_m, g_col[h]) for h in heads]
    dcy_row = [_dot01_r(g_row[h], up_m) for h in heads]
    dtot = [_dot01_l(same_m, g_col[h]) for h in heads]
    dm = [jnp.where(lower, jnp.exp(jnp.where(lower, dcy[h] - dcy_row[h], 0.0)), 0.0) for h in heads]
    kb = [k[h] * beta[h] for h in heads]
    a_mat = [jnp.where(strict, _dot_nt(kb[h], k[h]) * dm[h], 0.0) for h in heads]
    attn = [jnp.where(lower, _dot_nt(q[h], k[h]) * dm[h], 0.0) for h in heads]
    edc = [jnp.exp(dcy[h]) for h in heads]
    t_inv = _tri_inv(a_mat, ri, ci)
    u = [_dotx(t_inv[h], v[h] * beta[h]) for h in heads]
    w = [_dotx(t_inv[h], kb[h] * edc[h]) for h in heads]
    qd = [q[h] * edc[h] for h in heads]
    kdt = [(k[h] * jnp.exp(dtot[h] - dcy[h])).T for h in heads]
    gl = [jnp.exp(dtot[h]) for h in heads]
    o_all = []
    for h in heads:
        if nseq == 1:
            s_old = sout_ref[0, h]
            v_new = u[h] - _dot(w[h], s_old)
            o = _dot(qd[h], s_old) + _dot(attn[h], v_new)
            sout_ref[0, h] = s_old * gl[h][0:1, :] + _dot(kdt[h], v_new)
        else:
            vn, oq = [], []
            for s in range(nseq):
                s_old = sout_ref[s, h]
                rows = slice(s * ls, (s + 1) * ls)
                vn.append(u[h][rows] - _dot(w[h][rows], s_old))
                oq.append(_dot(qd[h][rows], s_old))
            v_new = jnp.concatenate(vn, axis=0)
            o = jnp.concatenate(oq, axis=0) + _dot(attn[h], v_new)
            for s in range(nseq):
                kdt_s = jnp.where(_idiv(ci, ls) == s, kdt[h], 0.0)
                sout_ref[s, h] = sout_ref[s, h] * gl[h][s * ls:s * ls + 1, :] + _dot(kdt_s, v_new)
        o_all.append(o)
    for h in heads:
        o = o_all[h]
        o = o * lax.rsqrt(jnp.mean(o * o, axis=-1, keepdims=True) + EPS) * ng_ref[...]
        gt = gate_ref[:, h * LANE:(h + 1) * LANE]
        o_ref[:, h * LANE:(h + 1) * LANE] = o * (gt * jax.nn.sigmoid(gt))


def _gdn(qkv, gate, sm, smt, alog_b, dtb_b, ng, s0, s0_off, nb, nt, ls):
    m = qkv.shape[0]
    c = GDN_CHUNK
    nseq = c // ls
    row = lambda i, j: (i * nt + j, 0)
    n_state = nb * nseq
    return pl.pallas_call(
        functools.partial(_gdn_kernel, ls=ls),
        grid=(nb, nt),
        in_specs=[pl.BlockSpec((c, GDN_QKV), row),
                  pl.BlockSpec((c, 512), row),
                  pl.BlockSpec((c, LANE), row),
                  pl.BlockSpec((1, 8, c), lambda i, j: (i * nt + j, 0, 0)),
                  pl.BlockSpec((8, LANE), lambda i, j: (0, 0)),
                  pl.BlockSpec((8, LANE), lambda i, j: (0, 0)),
                  pl.BlockSpec((1, LANE), lambda i, j: (0, 0)),
                  pl.BlockSpec((nseq, GDN_HEADS, GDN_DK, GDN_DV), lambda i, j: (s0_off + i, 0, 0, 0))],
        out_specs=[pl.BlockSpec((c, 512), row),
                   pl.BlockSpec((nseq, GDN_HEADS, GDN_DK, GDN_DV), lambda i, j: (i, 0, 0, 0))],
        out_shape=[jax.ShapeDtypeStruct((m, 512), F32),
                   jax.ShapeDtypeStruct((n_state, GDN_HEADS, GDN_DK, GDN_DV), F32)],
        compiler_params=_params(("parallel", "arbitrary")),
    )(qkv, gate, sm, smt, alog_b, dtb_b, ng, s0)


def _cmp_kernel(x_ref, pe_ref, wlo_ref, whi_ref, y_ref):
    x = x_ref[0]
    y_ref[0, :, 0:NSA_KV_W] = _dot(x + pe_ref[0, 0:1, :], wlo_ref[0])
    y_ref[0, :, NSA_KV_W:2 * NSA_KV_W] = _dot(x + pe_ref[0, 1:2, :], whi_ref[0])


def _compress(x, pe, wlo, whi, tr):
    nl, r, _ = x.shape
    tr = min(tr, r)
    return pl.pallas_call(
        _cmp_kernel,
        grid=(nl, r // tr),
        in_specs=[pl.BlockSpec((1, tr, SUB_W), lambda l, i: (l, i, 0)),
                  pl.BlockSpec((1, 2, SUB_W), lambda l, i: (l, 0, 0)),
                  pl.BlockSpec((1, SUB_W, NSA_KV_W), lambda l, i: (l, 0, 0)),
                  pl.BlockSpec((1, SUB_W, NSA_KV_W), lambda l, i: (l, 0, 0))],
        out_specs=pl.BlockSpec((1, tr, 2 * NSA_KV_W), lambda l, i: (l, i, 0)),
        out_shape=jax.ShapeDtypeStruct((nl, r, 2 * NSA_KV_W), F32),
        compiler_params=_params(("parallel", "parallel")),
    )(x, pe, wlo, whi)


def _cmp_pool_kernel(p_ref, pe_ref, wlo_ref, whi_ref, y_ref, x_scr, *, npg):
    for j in range(npg):
        x = p_ref[0, j].T
        for hf in range(2):
            x_scr[hf, j * PAGE_SIZE:(j + 1) * PAGE_SIZE, :] = x[:, hf * LANE:(hf + 1) * LANE]
    n = npg * SUB_PP
    for hf in range(2):
        cols = slice(hf * LANE, (hf + 1) * LANE)
        ylo = jnp.zeros((n, LANE), F32)
        yhi = jnp.zeros((n, LANE), F32)
        for l in range(CMP_STRIDE):
            xl = x_scr[hf, pl.ds(l, n, stride=CMP_STRIDE), :]
            ylo = ylo + _dot(xl + pe_ref[0, 0, l:l + 1, cols], wlo_ref[0, l, cols, cols])
            yhi = yhi + _dot(xl + pe_ref[0, 1, l:l + 1, cols], whi_ref[0, l, cols, cols])
        y_ref[0, :, hf * LANE:(hf + 1) * LANE] = ylo
        y_ref[0, :, NSA_KV_W + hf * LANE:NSA_KV_W + (hf + 1) * LANE] = yhi


def _compress_pool(pages_t, pe, wlo, whi, npg):
    nl, npool = pages_t.shape[:2]
    assert npool % npg == 0
    n = npg * SUB_PP
    return pl.pallas_call(
        functools.partial(_cmp_pool_kernel, npg=npg),
        grid=(nl, npool // npg),
        in_specs=[pl.BlockSpec((1, npg, NSA_KV_W, PAGE_SIZE), lambda l, i: (l, i, 0, 0)),
                  pl.BlockSpec((1, 2, CMP_STRIDE, NSA_KV_W), lambda l, i: (l, 0, 0, 0)),
                  pl.BlockSpec((1, CMP_STRIDE, NSA_KV_W, NSA_KV_W), lambda l, i: (l, 0, 0, 0)),
                  pl.BlockSpec((1, CMP_STRIDE, NSA_KV_W, NSA_KV_W), lambda l, i: (l, 0, 0, 0))],
        out_specs=pl.BlockSpec((1, n, 2 * NSA_KV_W), lambda l, i: (l, i, 0)),
        out_shape=jax.ShapeDtypeStruct((nl, npool * SUB_PP, 2 * NSA_KV_W), F32),
        scratch_shapes=[pltpu.VMEM((2, npg * PAGE_SIZE, LANE), F32)],
        compiler_params=_params(("parallel", "parallel")),
    )(pages_t, pe, wlo, whi)


def _slope(h):
    return 2.0 ** (-8.0 * (h + 1) / NSA_HEADS)


def _make_units(q, stacked):
    qb = q.shape[0]
    units = []
    for g in range(NSA_KV_HEADS):
        hs = range(g * NSA_REP, (g + 1) * NSA_REP)
        slabs = [q[:, h * NSA_HD:(h + 1) * NSA_HD] * (NSA_HD ** -0.5) for h in hs]
        if stacked:
            rows = jnp.concatenate(slabs, axis=0).astype(BF16)
            slope = jnp.concatenate([jnp.full((qb, 1), _slope(h), F32) for h in hs], axis=0)
            units.append([(rows, slope, NSA_REP)])
        else:
            units.append([(s.astype(BF16), _slope(h), 1) for s, h in zip(slabs, hs)])
    return units


def _cmp_branch(units, pq, ckv, ncp):
    qb = pq.shape[0]
    n_ix = _iota((1, ncp), 1)
    c_end = n_ix * CMP_STRIDE + (CMP_LEN - 1)
    c_ctr = (n_ix * CMP_STRIDE).astype(F32) + 0.5 * (CMP_LEN - 1)
    pool = (_idiv(_iota((ncp, LANE), 0), SEL_BLOCK // CMP_STRIDE) == _iota((ncp, LANE), 1)).astype(F32)
    ckv_b = ckv.astype(BF16)
    geo = {}
    outs, imps = [], []
    for g in range(NSA_KV_HEADS):
        kc = ckv_b[:, g * NSA_HD:(g + 1) * NSA_HD]
        vc = ckv_b[:, (NSA_KV_HEADS + g) * NSA_HD:(NSA_KV_HEADS + g + 1) * NSA_HD]
        psum = jnp.zeros((qb, ncp), F32)
        og = []
        for rows, slope, rep in units[g]:
            if rep not in geo:
                pqr = _rep(pq, rep)
                geo[rep] = (c_end <= pqr, pqr.astype(F32) - c_ctr)
            ok, dist = geo[rep]
            s = _dot_nt(rows, kc) - slope * dist
            s = jnp.where(ok, s, NEG_INF)
            m = jnp.max(s, axis=-1, keepdims=True)
            e = jnp.where(ok, jnp.exp(s - m), 0.0)
            den = jnp.sum(e, axis=-1, keepdims=True)
            p = e / jnp.where(den > 0.0, den, 1.0)
            og.append(_dot(p, vc))
            for r in range(rep):
                psum = psum + p[r * qb:(r + 1) * qb]
        outs.append(og)
        imps.append(_dot01_r(psum, pool))
    return outs, imps


def _select_blocks(imp, pq, axis_t):
    qb = imp.shape[0]
    blk = _iota((qb, LANE), 1)
    cur = _idiv(pq, SEL_BLOCK)
    forced = (blk == 0) | (blk == cur) | (blk == cur - 1)
    score = jnp.where(blk <= cur, imp + jnp.where(forced, FORCE_BONUS, 0.0), NEG_INF)
    if axis_t:
        work = score.T
        idx = _iota(work.shape, 0).astype(F32)
        ax = 0
    else:
        work = score
        idx = blk.astype(F32)
        ax = 1
    sel = jnp.zeros(work.shape, F32)
    for _ in range(N_SEL):
        m = jnp.max(work, axis=ax, keepdims=True)
        first = jnp.min(jnp.where(work == m, idx, 2.0 * LANE), axis=ax, keepdims=True)
        hit = idx == first
        sel = jnp.where(hit, 1.0, sel)
        work = jnp.where(hit, -jnp.inf, work)
    return sel.T if axis_t else sel


def _slc_tile(units, sel, pq, kvt, k0, tk, causal, flags, m_scr, l_scr, acc_scr):
    key = k0 + _iota((1, tk), 1)
    dist_i = pq - key
    dist = dist_i.astype(F32)
    e_tile = (_iota((LANE, tk), 0) == _idiv(k0 + _iota((LANE, tk), 1), SEL_BLOCK)).astype(BF16)
    ui = 0
    for g in range(NSA_KV_HEADS):
        def group(g=g, ui=ui):
            on = jnp.dot(sel[g].astype(BF16), e_tile, preferred_element_type=F32) > 0.5
            if causal:
                on = on & (dist_i >= 0)
            bias = jnp.where(on, 0.0, NEG_INF)
            kt = kvt[g * NSA_HD:(g + 1) * NSA_HD, :]
            vt = kvt[(NSA_KV_HEADS + g) * NSA_HD:(NSA_KV_HEADS + g + 1) * NSA_HD, :]
            for j, (rows, slope, rep) in enumerate(units[g]):
                u = ui + j
                s = jnp.dot(rows, kt, preferred_element_type=F32) + (_rep(bias, rep) - slope * _rep(dist, rep))
                m_prev = m_scr[u][:, 0:1]
                m_new = jnp.maximum(m_prev, jnp.max(s, axis=-1, keepdims=True))
                alpha = jnp.exp(m_prev - m_new)
                p = jnp.exp(s - m_new)
                l_new = alpha * l_scr[u][:, 0:1] + jnp.sum(p, axis=-1, keepdims=True)
                pv = lax.dot_general(p.astype(BF16), vt, _NT, preferred_element_type=F32)
                acc_scr[u] = alpha * acc_scr[u] + pv
                m_scr[u] = jnp.broadcast_to(m_new, m_scr.shape[1:])
                l_scr[u] = jnp.broadcast_to(l_new, l_scr.shape[1:])

        if flags is None:
            group()
        else:
            pl.when(flags[g])(group)
        ui += len(units[g])


def _win_branch(units, pq, kvt, w0):
    wk = kvt.shape[1]
    dist_i = pq - (w0 + _iota((1, wk), 1))
    dist = dist_i.astype(F32)
    bias = jnp.where((dist_i >= 0) & (dist_i < WINDOW), 0.0, NEG_INF)
    outs = []
    for g in range(NSA_KV_HEADS):
        kt = kvt[g * NSA_HD:(g + 1) * NSA_HD, :]
        vt = kvt[(NSA_KV_HEADS + g) * NSA_HD:(NSA_KV_HEADS + g + 1) * NSA_HD, :]
        og = []
        for rows, slope, rep in units[g]:
            s = jnp.dot(rows, kt, preferred_element_type=F32) + (_rep(bias, rep) - slope * _rep(dist, rep))
            m = jnp.max(s, axis=-1, keepdims=True)
            p = jnp.exp(s - m)
            pv = lax.dot_general(p.astype(BF16), vt, _NT, preferred_element_type=F32)
            og.append(pv / jnp.sum(p, axis=-1, keepdims=True))
        outs.append(og)
    return outs


def _nsa_finish(units, o_c, o_w, qb, sm, ng_ref, o_ref, l_scr, acc_scr):
    gates = jax.nn.sigmoid(sm)
    per_group = len(units[0])
    for h in range(NSA_HEADS):
        g, r = divmod(h, NSA_REP)
        if per_group == 1:
            rows = slice(r * qb, (r + 1) * qb)
            u, j = g, 0
        else:
            rows = slice(0, qb)
            u, j = h, r
        oc = o_c[g][j][rows]
        ow = o_w[g][j][rows]
        o_s = acc_scr[u][rows] / l_scr[u][rows][:, 0:1]
        c0 = SM_G + 3 * h
        o = gates[:, c0:c0 + 1] * oc + gates[:, c0 + 1:c0 + 2] * o_s + gates[:, c0 + 2:c0 + 3] * ow
        o = o * lax.rsqrt(jnp.mean(o * o, axis=-1, keepdims=True) + EPS) * ng_ref[...]
        o_ref[:, h * NSA_HD:(h + 1) * NSA_HD] = o


def _slc_init(m_scr, l_scr, acc_scr):
    m_scr[...] = jnp.full(m_scr.shape, NEG_INF, F32)
    l_scr[...] = jnp.zeros(l_scr.shape, F32)
    acc_scr[...] = jnp.zeros(acc_scr.shape, F32)


def _nsa_prompt_kernel(q_ref, sm_ref, y_ref, slc_ref, win_ref, ng_ref, o_ref,
                       ckv_scr, m_scr, l_scr, acc_scr, *, t_len):
    i = pl.program_id(1)
    nsub = t_len // CMP_STRIDE

    @pl.when(i == 0)
    def _():
        y = y_ref[0]
        ckv_scr[...] = y[:, 0:NSA_KV_W] + pltpu.roll(y[:, NSA_KV_W:2 * NSA_KV_W], nsub - 1, 0)

    pos0 = i * Q_BLOCK
    pq = pos0 + _iota((Q_BLOCK, 1), 0)
    units = _make_units(q_ref[...], False)
    o_c, imps = _cmp_branch(units, pq, ckv_scr[...], nsub)
    sel = [_select_blocks(imp, pq, True) for imp in imps]
    used = [jnp.max(s, axis=0, keepdims=True) for s in sel]
    tile_of_blk = _idiv(_iota((1, LANE), 1), SLC_TILE // SEL_BLOCK)

    _slc_init(m_scr, l_scr, acc_scr)
    n_tiles = (pos0 + Q_BLOCK + SLC_TILE - 1) // SLC_TILE

    def body(kt, carry):
        k0 = kt * SLC_TILE
        flags = [jnp.max(jnp.where(tile_of_blk == kt, u, 0.0)) > 0.5 for u in used]
        _slc_tile(units, sel, pq, slc_ref[0, kt], k0, SLC_TILE, False, flags, m_scr, l_scr, acc_scr)
        return carry

    lax.fori_loop(0, n_tiles - 1, body, 0)
    _slc_tile(units, sel, pq, slc_ref[0, n_tiles - 1], (n_tiles - 1) * SLC_TILE, SLC_TILE, True, None,
              m_scr, l_scr, acc_scr)

    wt0 = jnp.maximum(i - (WIN_TILES - 1), 0)
    win = jnp.concatenate([win_ref[0, wt0 + j] for j in range(WIN_TILES)], axis=1)
    o_w = _win_branch(units, pq, win, wt0 * Q_BLOCK)
    _nsa_finish(units, o_c, o_w, Q_BLOCK, sm_ref[...], ng_ref, o_ref, l_scr, acc_scr)


def _nsa_prompt(nq, sm, y, slc_tb, win_tb, ng, b, t_len):
    nqb = t_len // Q_BLOCK
    nsub = t_len // CMP_STRIDE
    row = lambda bi, i: (bi * nqb + i, 0)
    return pl.pallas_call(
        functools.partial(_nsa_prompt_kernel, t_len=t_len),
        grid=(b, nqb),
        in_specs=[pl.BlockSpec((Q_BLOCK, 512), row),
                  pl.BlockSpec((Q_BLOCK, LANE), row),
                  pl.BlockSpec((1, nsub, 2 * NSA_KV_W), lambda bi, i: (bi, 0, 0)),
                  pl.BlockSpec((1, t_len // SLC_TILE, NSA_KV_W, SLC_TILE), lambda bi, i: (bi, 0, 0, 0)),
                  pl.BlockSpec((1, nqb, NSA_KV_W, Q_BLOCK), lambda bi, i: (bi, 0, 0, 0)),
                  pl.BlockSpec((1, NSA_HD), lambda bi, i: (0, 0))],
        out_specs=pl.BlockSpec((Q_BLOCK, 512), row),
        out_shape=jax.ShapeDtypeStruct((b * t_len, 512), F32),
        scratch_shapes=[pltpu.VMEM((nsub, NSA_KV_W), F32),
                        pltpu.VMEM((NSA_HEADS, Q_BLOCK, LANE), F32),
                        pltpu.VMEM((NSA_HEADS, Q_BLOCK, LANE), F32),
                        pltpu.VMEM((NSA_HEADS, Q_BLOCK, NSA_HD), F32)],
        compiler_params=_params(("parallel", "arbitrary")),
    )(nq, sm, y, slc_tb, win_tb, ng)


def _nsa_sample_kernel(pt_ref, q_ref, sm_ref, ynew_ref, slcn_ref, winn_ref, winc_ref, ng_ref, *rest,
                       n_pages, past_len, dec_len):
    del pt_ref
    y_pages = rest[0:n_pages]
    s_pages = rest[n_pages:2 * n_pages]
    o_ref = rest[2 * n_pages]
    y_scr, slc_scr, win_scr, m_scr, l_scr, acc_scr = rest[2 * n_pages + 1:]
    ncp = n_pages * SUB_PP
    tk = slc_scr.shape[1]
    wb = winc_ref.shape[2]

    for p in range(n_pages):
        y_scr[p * SUB_PP:(p + 1) * SUB_PP, :] = y_pages[p][0]
    y_scr[ncp:ncp + 8, :] = ynew_ref[0]
    y = y_scr[...]
    ckv = (y[:, 0:NSA_KV_W] + pltpu.roll(y[:, NSA_KV_W:2 * NSA_KV_W], ncp + 7, 0))[0:ncp]

    def new_cols(ref):
        pad = jnp.zeros((PAGE_SIZE - dec_len, NSA_KV_W), F32)
        return jnp.concatenate([ref[...], pad], axis=0).T.astype(BF16)

    for p in range(n_pages):
        slc_scr[:, p * PAGE_SIZE:(p + 1) * PAGE_SIZE] = s_pages[p][0].astype(BF16)
    slc_scr[:, past_len:tk] = new_cols(slcn_ref)
    win_scr[:, 0:wb] = winc_ref[0].astype(BF16)
    win_scr[:, wb:wb + PAGE_SIZE] = new_cols(winn_ref)

    pq = past_len + _iota((dec_len, 1), 0)
    units = _make_units(q_ref[...], True)
    o_c, imps = _cmp_branch(units, pq, ckv, ncp)
    sel = [_select_blocks(imp, pq, False) for imp in imps]
    _slc_init(m_scr, l_scr, acc_scr)
    _slc_tile(units, sel, pq, slc_scr[...], 0, tk, True, None, m_scr, l_scr, acc_scr)
    o_w = _win_branch(units, pq, win_scr[...], past_len - wb)
    _nsa_finish(units, o_c, o_w, dec_len, sm_ref[...], ng_ref, o_ref, l_scr, acc_scr)


def _nsa_sample(page_table, nq, sm, ynew, slc_new, win_new, win_cache_t, ng, y_pool, slc_pool_t, layer, n_pool,
                past_len, dec_len):
    db, n_pages = page_table.shape
    wb = win_cache_t.shape[2]
    tk = past_len + PAGE_SIZE
    wk = wb + PAGE_SIZE
    rows = NSA_REP * dec_len
    row = lambda b, pt: (b, 0)

    def page_map(k):
        return lambda b, pt: (layer * n_pool + pt[b, k], 0, 0)

    in_specs = [pl.BlockSpec((dec_len, 512), row),
                pl.BlockSpec((dec_len, LANE), row),
                pl.BlockSpec((1, 8, 2 * NSA_KV_W), lambda b, pt: (b, 0, 0)),
                pl.BlockSpec((dec_len, NSA_KV_W), row),
                pl.BlockSpec((dec_len, NSA_KV_W), row),
                pl.BlockSpec((1, NSA_KV_W, wb), lambda b, pt: (layer * db + b, 0, 0)),
                pl.BlockSpec((1, NSA_HD), lambda b, pt: (0, 0))]
    in_specs += [pl.BlockSpec((1, SUB_PP, 2 * NSA_KV_W), page_map(k)) for k in range(n_pages)]
    in_specs += [pl.BlockSpec((1, NSA_KV_W, PAGE_SIZE), page_map(k)) for k in range(n_pages)]
    grid_spec = pltpu.PrefetchScalarGridSpec(
        num_scalar_prefetch=1,
        grid=(db,),
        in_specs=in_specs,
        out_specs=pl.BlockSpec((dec_len, 512), row),
        scratch_shapes=[pltpu.VMEM((n_pages * SUB_PP + 8, 2 * NSA_KV_W), F32),
                        pltpu.VMEM((NSA_KV_W, tk), BF16),
                        pltpu.VMEM((NSA_KV_W, wk), BF16),
                        pltpu.VMEM((NSA_KV_HEADS, rows, LANE), F32),
                        pltpu.VMEM((NSA_KV_HEADS, rows, LANE), F32),
                        pltpu.VMEM((NSA_KV_HEADS, rows, NSA_HD), F32)])
    return pl.pallas_call(
        functools.partial(_nsa_sample_kernel, n_pages=n_pages, past_len=past_len, dec_len=dec_len),
        grid_spec=grid_spec,
        out_shape=jax.ShapeDtypeStruct((db * dec_len, 512), F32),
        compiler_params=_params(("arbitrary",)),
    )(page_table, nq, sm, ynew, slc_new, win_new, win_cache_t, ng,
      *([y_pool] * n_pages), *([slc_pool_t] * n_pages))


def _proj_columns():
    sizes = (GDN_QKV, GDN_HEADS, GDN_HEADS, GDN_HEADS * GDN_DV, NSA_HEADS * NSA_HD,
             NSA_KV_W, NSA_KV_W, NSA_KV_W, 3 * NSA_HEADS)
    off = np.concatenate([[0], np.cumsum(sizes)])
    seg = lambda i: np.arange(off[i], off[i + 1])
    order = np.concatenate([seg(0), seg(3), seg(4), seg(5), seg(6), seg(7), seg(1), seg(2), seg(8)])
    return order, N_PROJ - order.size


def _cmp_weights(cmp_w):
    nl = cmp_w.shape[0]
    eye_g = jnp.eye(NSA_KV_HEADS, dtype=cmp_w.dtype)
    eye_c = jnp.eye(2, dtype=cmp_w.dtype)

    def half(w):
        big = jnp.einsum('nlcde,cx,gy->nlcgdxye', w, eye_c, eye_g)
        return big.reshape(nl, SUB_W, NSA_KV_W).astype(BF16)

    return half(cmp_w[:, :CMP_STRIDE]), half(cmp_w[:, CMP_STRIDE:])


def _cmp_pe(cmp_pe):
    nl = cmp_pe.shape[0]

    def half(p):
        return jnp.broadcast_to(p[:, :, :, None, :], (nl, CMP_STRIDE, 2, NSA_KV_HEADS, NSA_HD)).reshape(nl, SUB_W)

    return jnp.stack([half(cmp_pe[:, :CMP_STRIDE]), half(cmp_pe[:, CMP_STRIDE:])], axis=1)


def _small_t(sm, c):
    m = sm.shape[0]
    return jnp.transpose(sm[:, :8].reshape(m // c, c, 8), (0, 2, 1))


def _feature_major(a):
    nd = a.ndim
    perm = tuple(range(nd - 4)) + (nd - 3, nd - 2, nd - 1, nd - 4)
    t = jnp.transpose(a, perm)
    return t.reshape(t.shape[:nd - 4] + (NSA_KV_W, t.shape[-1]))


def _row_major_kv(a_t):
    lead = a_t.shape[:-2]
    n = len(lead)
    t = a_t.reshape(lead + (2, NSA_KV_HEADS, NSA_HD, a_t.shape[-1]))
    return jnp.transpose(t, tuple(range(n)) + (n + 3, n, n + 1, n + 2))


def kernel(x_prompt, x_sample, cache_cmp_kv, cache_slc_kv, page_table, cache_win_kv, state_gdn, state_conv,
           norm_mix, w_in, conv_w, gdn_a_log, gdn_dt_bias, gdn_norm, nsa_cmp_w, nsa_cmp_pe, nsa_norm,
           w_out, norm_ffn, w_ffn_in, w_ffn_out, norm_final):
    bp, tp, _ = x_prompt.shape
    db, ts, _ = x_sample.shape
    n_pool = cache_cmp_kv.shape[1]
    n_pages = page_table.shape[1]
    past_len = n_pages * PAGE_SIZE
    wb = cache_win_kv.shape[2]
    kv_row = (2, NSA_KV_HEADS, NSA_HD)
    wkeep = min(WINDOW, tp)
    assert GDN_CHUNK % ts == 0 and (db * ts) % GDN_CHUNK == 0 and tp % SLC_TILE == 0
    assert tp >= WIN_TILES * Q_BLOCK and n_pool % POOL_PAGES == 0

    order, pad = _proj_columns()
    w_in_b = jnp.pad(w_in[:, :, order], ((0, 0), (0, 0), (0, pad))).astype(BF16)
    w_kv_t = jnp.transpose(w_in_b[:, :, C_KV:C_SMALL], (0, 2, 1))
    w_out_b = w_out.astype(BF16)
    w_ffn_in_b = w_ffn_in.astype(BF16)
    w_ffn_out_b = w_ffn_out.astype(BF16)
    cw = jnp.pad(conv_w, ((0, 0), (0, 8 - CONV_W), (0, 0)))
    alog_b = jnp.broadcast_to(jnp.pad(gdn_a_log, ((0, 0), (0, 8 - GDN_HEADS)))[:, :, None], (DEPTH, 8, LANE))
    dtb_b = jnp.broadcast_to(jnp.pad(gdn_dt_bias, ((0, 0), (0, 8 - GDN_HEADS)))[:, :, None], (DEPTH, 8, LANE))
    wlo, whi = _cmp_weights(nsa_cmp_w)
    pe2 = _cmp_pe(nsa_cmp_pe)

    y_pool = _compress_pool(_feature_major(cache_cmp_kv), pe2.reshape(DEPTH, 2, CMP_STRIDE, NSA_KV_W),
                            wlo.reshape(DEPTH, CMP_STRIDE, NSA_KV_W, NSA_KV_W),
                            whi.reshape(DEPTH, CMP_STRIDE, NSA_KV_W, NSA_KV_W), POOL_PAGES)
    y_pool = y_pool.reshape(DEPTH * n_pool, SUB_PP, 2 * NSA_KV_W)
    slc_pool_t = _feature_major(cache_slc_kv).reshape(DEPTH * n_pool, NSA_KV_W, PAGE_SIZE)
    win_cache_t = _feature_major(cache_win_kv)
    state_all = state_gdn.reshape(DEPTH * db, GDN_HEADS, GDN_DK, GDN_DV)
    seq_per_chunk = GDN_CHUNK // ts

    xp = x_prompt.reshape(bp * tp, D_MODEL)
    xs = x_sample.reshape(db * ts, D_MODEL)
    zero_hist = jnp.zeros((bp, 8, GDN_QKV), F32)
    zero_state = jnp.zeros((bp, GDN_HEADS, GDN_DK, GDN_DV), F32)
    outs = [[] for _ in range(10)]
    for l in range(DEPTH):
        g_mix = norm_mix[l][None, :]
        g_ffn = norm_ffn[l][None, :]
        gdn_g = gdn_norm[l][None, :]
        nsa_g = nsa_norm[l][None, :]

        (qkv, gate, nq, ncmp, _, _, sm, cmp_t, slc_t, win_t, slc_tb, win_tb) = _proj(
            xp, g_mix, w_in_b[l], w_kv_t[l], SLC_TILE, bp)
        qkvn = _gdn_prep(qkv.reshape(bp, tp, GDN_QKV), zero_hist, cw[l], 256).reshape(bp * tp, GDN_QKV)
        o_gdn, s_new = _gdn(qkvn, gate, sm, _small_t(sm, GDN_CHUNK), alog_b[l], dtb_b[l], gdn_g, zero_state, 0,
                            bp, tp // GDN_CHUNK, GDN_CHUNK)
        y = _compress(ncmp.reshape(1, bp * tp // CMP_STRIDE, SUB_W), pe2[l:l + 1], wlo[l:l + 1],
                      whi[l:l + 1], 512).reshape(bp, tp // CMP_STRIDE, 2 * NSA_KV_W)
        o_nsa = _nsa_prompt(nq, sm, y, slc_tb, win_tb, nsa_g, bp, tp)
        xp = _out_proj(xp, o_gdn, o_nsa, w_out_b[l], 512)
        xp = _ffn(xp, g_ffn, w_ffn_in_b[l], w_ffn_out_b[l], 1024)
        outs[0].append(cmp_t)
        outs[2].append(slc_t)
        outs[4].append(win_t[:, :, tp - wkeep:])
        outs[6].append(s_new)
        outs[8].append(qkv.reshape(bp, tp, GDN_QKV)[:, -(CONV_W - 1):])

        qkv, gate, nq, ncmp, nslc, nwin, sm = _proj(xs, g_mix, w_in_b[l], None, 512)
        hist = jnp.pad(state_conv[l], ((0, 0), (8 - (CONV_W - 1), 0), (0, 0)))
        qkvn = _gdn_prep(qkv.reshape(db, ts, GDN_QKV), hist, cw[l], ts).reshape(db * ts, GDN_QKV)
        o_gdn, s_new = _gdn(qkvn, gate, sm, _small_t(sm, GDN_CHUNK), alog_b[l], dtb_b[l], gdn_g, state_all,
                            l * (db // seq_per_chunk), db // seq_per_chunk, 1, ts)
        new_sub = jnp.pad(ncmp.reshape(db, 1, ts * NSA_KV_W), ((0, 0), (0, 7), (0, SUB_W - ts * NSA_KV_W)))
        ynew = _compress(new_sub.reshape(1, db * 8, SUB_W), pe2[l:l + 1], wlo[l:l + 1], whi[l:l + 1],
                         512).reshape(db, 8, 2 * NSA_KV_W)
        o_nsa = _nsa_sample(page_table, nq, sm, ynew, nslc, nwin, win_cache_t.reshape(DEPTH * db, NSA_KV_W, wb),
                            nsa_g, y_pool, slc_pool_t, l, n_pool, past_len, ts)
        xs = _out_proj(xs, o_gdn, o_nsa, w_out_b[l], 512)
        xs = _ffn(xs, g_ffn, w_ffn_in_b[l], w_ffn_out_b[l], 1024)
        outs[1].append(ncmp.reshape((db, ts) + kv_row))
        outs[3].append(nslc.reshape((db, ts) + kv_row))
        nwin_t = jnp.transpose(nwin.reshape(db, ts, NSA_KV_W), (0, 2, 1))
        outs[5].append(jnp.concatenate([win_cache_t[l], nwin_t], axis=2)[:, :, ts:])
        outs[7].append(s_new)
        outs[9].append(jnp.concatenate([state_conv[l], qkv.reshape(db, ts, GDN_QKV)], axis=1)[:, -(CONV_W - 1):])

    y_prompt = _final_norm(xp, norm_final[None, :], 512).reshape(bp, tp, D_MODEL)
    y_sample = _final_norm(xs, norm_final[None, :], 512).reshape(db, ts, D_MODEL)
    res = [jnp.stack(o) for o in outs]
    for i in (0, 2, 4, 5):
        res[i] = _row_major_kv(res[i])
    return (y_prompt, y_sample) + tuple(res)
```

```python
import functools

import numpy as np
import jax
import jax.numpy as jnp
from jax import lax
from jax.experimental import pallas as pl
from jax.experimental.pallas import tpu as pltpu

F32 = jnp.float32
BF16 = jnp.bfloat16

D_MODEL = 1024
DEPTH = 4
PAGE_SIZE = 128
GDN_HEADS = 4
GDN_DK = 128
GDN_DV = 128
GDN_QKV = GDN_HEADS * (2 * GDN_DK + GDN_DV)
CONV_W = 4
NSA_HEADS = 8
NSA_KV_HEADS = 2
NSA_HD = 64
NSA_REP = NSA_HEADS // NSA_KV_HEADS
CMP_LEN = 32
CMP_STRIDE = 16
SEL_BLOCK = 64
N_SEL = 16
WINDOW = 512
NSA_KV_W = 2 * NSA_KV_HEADS * NSA_HD
D_FF = (8 * D_MODEL + 3 * 256 - 1) // (3 * 256) * 256
NEG_INF = -1e30
FORCE_BONUS = 1e4
EPS = 1e-6

LANE = 128
GDN_CHUNK = 128
GDN_NCH = 2
INV_BLOCK = 16
Q_BLOCK = 128
SLC_TILE = 512
WIN_TILES = WINDOW // Q_BLOCK + 1
SUB_W = CMP_STRIDE * NSA_KV_W
SUB_PP = PAGE_SIZE // CMP_STRIDE
POOL_PAGES = 32
VMEM_LIMIT = 56 * 1024 * 1024

C_QKV, C_GATE, C_NQ, C_KV, C_SMALL = 0, 1536, 2048, 2560, 3328
N_PROJ = 3456
SM_B, SM_A, SM_G = 0, 4, 8

_NT = (((1,), (1,)), ((), ()))


def _params(sem):
    return pltpu.CompilerParams(dimension_semantics=sem, vmem_limit_bytes=VMEM_LIMIT)


def _dot(a, b):
    return jnp.dot(a.astype(BF16), b.astype(BF16), preferred_element_type=F32)


def _dot_nt(a, b):
    return lax.dot_general(a.astype(BF16), b.astype(BF16), _NT, preferred_element_type=F32)


def _split2(x):
    hi = x.astype(BF16)
    lo = (x - hi.astype(F32)).astype(BF16)
    return hi, lo


def _split3(x):
    hi = x.astype(BF16)
    r = x - hi.astype(F32)
    mid = r.astype(BF16)
    lo = (r - mid.astype(F32)).astype(BF16)
    return hi, mid, lo


def _dotx(a, b):
    ah, al = _split2(a)
    bh, bl = _split2(b)
    d = functools.partial(jnp.dot, preferred_element_type=F32)
    return d(ah, bh) + (d(ah, bl) + d(al, bh))


def _dot01_l(m01, x):
    m = m01.astype(BF16)
    hi, mid, lo = _split3(x)
    d = functools.partial(jnp.dot, preferred_element_type=F32)
    return d(m, hi) + (d(m, mid) + d(m, lo))


def _dot01_r(x, m01):
    m = m01.astype(BF16)
    hi, mid, lo = _split3(x)
    d = functools.partial(jnp.dot, preferred_element_type=F32)
    return d(hi, m) + (d(mid, m) + d(lo, m))


def _iota(shape, dim):
    return lax.broadcasted_iota(jnp.int32, shape, dim)


def _idiv(x, d):
    sh = int(d).bit_length() - 1
    assert (1 << sh) == d
    return lax.shift_right_logical(x, jnp.int32(sh))


def _rep(x, rep):
    return x if rep == 1 else jnp.concatenate([x] * rep, axis=0)


def _rms_rows(x, g):
    return x * lax.rsqrt(jnp.mean(x * x, axis=-1, keepdims=True) + EPS) * g


def _proj_kernel(x_ref, g_ref, w_ref, *rest, with_t):
    if with_t:
        wt_ref = rest[0]
        rest = rest[1:]
    qkv_ref, gate_ref, nq_ref, cmp_ref, slc_ref, win_ref, sm_ref = rest[:7]
    h = _rms_rows(x_ref[...], g_ref[...]).astype(BF16)
    d = functools.partial(jnp.dot, preferred_element_type=F32)
    qkv_ref[...] = d(h, w_ref[:, C_QKV:C_GATE])
    gate_ref[...] = d(h, w_ref[:, C_GATE:C_NQ])
    nq_ref[...] = d(h, w_ref[:, C_NQ:C_KV])
    kv = d(h, w_ref[:, C_KV:C_SMALL])
    cmp_ref[...] = kv[:, 0:NSA_KV_W]
    slc_ref[...] = kv[:, NSA_KV_W:2 * NSA_KV_W]
    win_ref[...] = kv[:, 2 * NSA_KV_W:3 * NSA_KV_W]
    sm_ref[...] = d(h, w_ref[:, C_SMALL:N_PROJ])
    if with_t:
        cmpt_ref, slct_ref, wint_ref, nqt_ref, sk_ref, sv_ref, wk_ref, wv_ref = rest[7:]
        kvt = lax.dot_general(wt_ref[...], h, _NT, preferred_element_type=F32)
        cmpt_ref[0] = kvt[0:NSA_KV_W]
        slct = kvt[NSA_KV_W:2 * NSA_KV_W]
        wint = kvt[2 * NSA_KV_W:3 * NSA_KV_W]
        slct_ref[0] = slct
        wint_ref[0] = wint
        nqt_ref[0] = kvt[3 * NSA_KV_W:3 * NSA_KV_W + 512]
        half = NSA_KV_W // 2
        sk_ref[0, 0] = kv[:, NSA_KV_W:NSA_KV_W + half].astype(BF16)
        sv_ref[0, 0] = slct[half:].astype(BF16)
        for j in range(wk_ref.shape[1]):
            rows = slice(j * Q_BLOCK, (j + 1) * Q_BLOCK)
            wk_ref[0, j] = kv[rows, 2 * NSA_KV_W:2 * NSA_KV_W + half].astype(BF16)
            wv_ref[0, j] = wint[half:, rows].astype(BF16)


def _proj(x, g, w, wt, tm, batch=None):
    m = x.shape[0]
    tm = min(tm, m)
    with_t = wt is not None
    widths = (GDN_QKV, 512, 512, NSA_KV_W, NSA_KV_W, NSA_KV_W, LANE)
    in_specs = [pl.BlockSpec((tm, D_MODEL), lambda i: (i, 0)),
                pl.BlockSpec((1, D_MODEL), lambda i: (0, 0)),
                pl.BlockSpec((D_MODEL, N_PROJ), lambda i: (0, 0))]
    out_specs = [pl.BlockSpec((tm, wd), lambda i: (i, 0)) for wd in widths]
    out_shape = [jax.ShapeDtypeStruct((m, wd), F32) for wd in widths]
    args = [x, g, w]
    if with_t:
        t_len = m // batch
        nt = t_len // tm
        assert tm == SLC_TILE and t_len % tm == 0
        in_specs.append(pl.BlockSpec((wt.shape[0], D_MODEL), lambda i: (0, 0)))
        args.append(wt)
        tmap = lambda i: (i // nt, 0, i % nt)
        tile = lambda i: (i // nt, i % nt, 0, 0)
        half = NSA_KV_W // 2
        nwt = tm // Q_BLOCK
        out_specs += [pl.BlockSpec((1, NSA_KV_W, tm), tmap)] * 3 + [pl.BlockSpec((1, 512, tm), tmap)]
        out_shape += [jax.ShapeDtypeStruct((batch, NSA_KV_W, t_len), F32)] * 3
        out_shape += [jax.ShapeDtypeStruct((batch, 512, t_len), F32)]
        out_specs += [pl.BlockSpec((1, 1, tm, half), tile),
                      pl.BlockSpec((1, 1, half, tm), tile),
                      pl.BlockSpec((1, nwt, Q_BLOCK, half), tile),
                      pl.BlockSpec((1, nwt, half, Q_BLOCK), tile)]
        out_shape += [jax.ShapeDtypeStruct((batch, nt, tm, half), BF16),
                      jax.ShapeDtypeStruct((batch, nt, half, tm), BF16),
                      jax.ShapeDtypeStruct((batch, t_len // Q_BLOCK, Q_BLOCK, half), BF16),
                      jax.ShapeDtypeStruct((batch, t_len // Q_BLOCK, half, Q_BLOCK), BF16)]
    return pl.pallas_call(
        functools.partial(_proj_kernel, with_t=with_t),
        grid=(m // tm,),
        in_specs=in_specs,
        out_specs=out_specs,
        out_shape=out_shape,
        compiler_params=_params(("parallel",)),
    )(*args)


def _out_kernel(x_ref, a1_ref, a2_ref, w_ref, o_ref):
    half = w_ref.shape[0] // 2
    o_ref[...] = x_ref[...] + (_dot(a1_ref[...], w_ref[:half, :]) + _dot(a2_ref[...], w_ref[half:, :]))


def _out_proj(x, a1, a2, w, tm):
    m = x.shape[0]
    tm = min(tm, m)
    return pl.pallas_call(
        _out_kernel,
        grid=(m // tm,),
        in_specs=[pl.BlockSpec((tm, D_MODEL), lambda i: (i, 0)),
                  pl.BlockSpec((tm, 512), lambda i: (i, 0)),
                  pl.BlockSpec((tm, 512), lambda i: (i, 0)),
                  pl.BlockSpec((D_MODEL, D_MODEL), lambda i: (0, 0))],
        out_specs=pl.BlockSpec((tm, D_MODEL), lambda i: (i, 0)),
        out_shape=jax.ShapeDtypeStruct((m, D_MODEL), F32),
        compiler_params=_params(("parallel",)),
    )(x, a1, a2, w)


def _ffn_kernel(x_ref, g_ref, wg_ref, wu_ref, wo_ref, o_ref, h_scr, acc_scr):
    f = pl.program_id(1)

    @pl.when(f == 0)
    def _():
        h_scr[...] = _rms_rows(x_ref[...], g_ref[...]).astype(BF16)
        acc_scr[...] = jnp.zeros_like(acc_scr)

    h = h_scr[...]
    gt = jnp.dot(h, wg_ref[...], preferred_element_type=F32)
    up = jnp.dot(h, wu_ref[...], preferred_element_type=F32)
    act = (gt * jax.nn.sigmoid(gt)) * up
    acc_scr[...] += jnp.dot(act.astype(BF16), wo_ref[...], preferred_element_type=F32)

    @pl.when(f == pl.num_programs(1) - 1)
    def _():
        o_ref[...] = x_ref[...] + acc_scr[...]


def _ffn(x, g, w_in, w_out, tm):
    m = x.shape[0]
    tm = min(tm, m)
    nf = 2
    tf = D_FF // nf
    return pl.pallas_call(
        _ffn_kernel,
        grid=(m // tm, nf),
        in_specs=[pl.BlockSpec((tm, D_MODEL), lambda i, f: (i, 0)),
                  pl.BlockSpec((1, D_MODEL), lambda i, f: (0, 0)),
                  pl.BlockSpec((D_MODEL, tf), lambda i, f: (0, f)),
                  pl.BlockSpec((D_MODEL, tf), lambda i, f: (0, nf + f)),
                  pl.BlockSpec((tf, D_MODEL), lambda i, f: (f, 0))],
        out_specs=pl.BlockSpec((tm, D_MODEL), lambda i, f: (i, 0)),
        out_shape=jax.ShapeDtypeStruct((m, D_MODEL), F32),
        scratch_shapes=[pltpu.VMEM((tm, D_MODEL), BF16), pltpu.VMEM((tm, D_MODEL), F32)],
        compiler_params=_params(("parallel", "arbitrary")),
    )(x, g, w_in, w_in, w_out)


def _final_norm_kernel(x_ref, g_ref, o_ref):
    o_ref[...] = _rms_rows(x_ref[...], g_ref[...])


def _final_norm(x, g, tm):
    m = x.shape[0]
    tm = min(tm, m)
    return pl.pallas_call(
        _final_norm_kernel,
        grid=(m // tm,),
        in_specs=[pl.BlockSpec((tm, D_MODEL), lambda i: (i, 0)),
                  pl.BlockSpec((1, D_MODEL), lambda i: (0, 0))],
        out_specs=pl.BlockSpec((tm, D_MODEL), lambda i: (i, 0)),
        out_shape=jax.ShapeDtypeStruct((m, D_MODEL), F32),
        compiler_params=_params(("parallel",)),
    )(x, g)


def _prep_kernel(raw_ref, hist_ref, cw_ref, o_ref, ext_scr, *, tc):
    @pl.when(pl.program_id(1) == 0)
    def _():
        ext_scr[0:8, :] = hist_ref[0]

    ext_scr[8:8 + tc, :] = raw_ref[0]
    y = jnp.zeros((tc, GDN_QKV), F32)
    for j in range(CONV_W):
        y = y + ext_scr[pl.ds(8 - (CONV_W - 1) + j, tc), :] * cw_ref[j:j + 1, :]
    tail = ext_scr[tc:tc + 8, :]
    ext_scr[0:8, :] = tail
    y = y * jax.nn.sigmoid(y)
    for s in range(GDN_QKV // LANE):
        blk = y[:, s * LANE:(s + 1) * LANE]
        if s < 2 * GDN_HEADS:
            blk = blk * lax.rsqrt(jnp.sum(blk * blk, axis=-1, keepdims=True) + EPS)
            if s < GDN_HEADS:
                blk = blk * (GDN_DK ** -0.5)
        o_ref[0, :, s * LANE:(s + 1) * LANE] = blk


def _gdn_prep(raw, hist, cw, tc):
    b, t, _ = raw.shape
    return pl.pallas_call(
        functools.partial(_prep_kernel, tc=tc),
        grid=(b, t // tc),
        in_specs=[pl.BlockSpec((1, tc, GDN_QKV), lambda i, j: (i, j, 0)),
                  pl.BlockSpec((1, 8, GDN_QKV), lambda i, j: (i, 0, 0)),
                  pl.BlockSpec((8, GDN_QKV), lambda i, j: (0, 0))],
        out_specs=pl.BlockSpec((1, tc, GDN_QKV), lambda i, j: (i, j, 0)),
        out_shape=jax.ShapeDtypeStruct((b, t, GDN_QKV), F32),
        scratch_shapes=[pltpu.VMEM((tc + 8, GDN_QKV), F32)],
        compiler_params=_params(("parallel", "arbitrary")),
    )(raw, hist, cw)


def _tri_inv(mats, ri, ci):
    c = mats[0].shape[0]
    eye = (ri == ci).astype(F32)
    bd = _idiv(ri, INV_BLOCK) == _idiv(ci, INV_BLOCK)
    ad = [jnp.where(bd, a, 0.0) for a in mats]
    ao = [a - d for a, d in zip(mats, ad)]
    pw = ad
    td = [eye - d for d in ad]
    k = 2
    while k < INV_BLOCK:
        pw = [_dotx(p, p) for p in pw]
        td = [_dotx(t, eye + p) for t, p in zip(td, pw)]
        k *= 2
    n = [_dotx(t, o) for t, o in zip(td, ao)]
    tn = [eye - x for x in n]
    pw = n
    k = 2
    while k < c // INV_BLOCK:
        pw = [_dotx(p, p) for p in pw]
        tn = [_dotx(t, eye + p) for t, p in zip(tn, pw)]
        k *= 2
    return [_dotx(a, b) for a, b in zip(tn, td)]


def _softplus(x):
    return jnp.maximum(x, 0.0) + jnp.log1p(jnp.exp(-jnp.abs(x)))


def _gdn_kernel(qkv_ref, gate_ref, sm_ref, smt_ref, alog_ref, dtb_ref, ng_ref, s0_ref,
                o_ref, sout_ref, *, ls, nch):
    c = GDN_CHUNK
    nseq = c // ls
    assert nseq == 1 or nch == 1
    heads = range(GDN_HEADS)
    pairs = [(ch, h) for ch in range(nch) for h in heads]

    @pl.when(pl.program_id(1) == 0)
    def _():
        sout_ref[...] = s0_ref[...]

    ri = _iota((c, c), 0)
    ci = _iota((c, c), 1)
    same = _idiv(ri, ls) == _idiv(ci, ls)
    lower = same & (ci <= ri)
    strict = same & (ci < ri)
    lo_m = lower.astype(F32)
    up_m = (same & (ri <= ci)).astype(F32)
    same_m = same.astype(F32)
    sm = [sm_ref[ch * c:(ch + 1) * c, :] for ch in range(nch)]
    smt = [smt_ref[ch] for ch in range(nch)]

    def slab(ch, s):
        return qkv_ref[ch * c:(ch + 1) * c, s * LANE:(s + 1) * LANE]

    q = [slab(ch, h) for ch, h in pairs]
    k = [slab(ch, GDN_HEADS + h) for ch, h in pairs]
    v = [slab(ch, 2 * GDN_HEADS + h) for ch, h in pairs]
    neg_a = [-jnp.exp(alog_ref[h:h + 1, :]) for h in heads]
    dtb = [dtb_ref[h:h + 1, :] for h in heads]
    beta = [jax.nn.sigmoid(jnp.broadcast_to(sm[ch][:, SM_B + h:SM_B + h + 1], (c, LANE))) for ch, h in pairs]
    g_col = [neg_a[h] * _softplus(jnp.broadcast_to(sm[ch][:, SM_A + h:SM_A + h + 1], (c, LANE)) + dtb[h])
             for ch, h in pairs]
    g_row = [neg_a[h] * _softplus(jnp.broadcast_to(smt[ch][SM_A + h:SM_A + h + 1, :], (c, c)) + dtb[h])
             for ch, h in pairs]
    n = range(len(pairs))
    dcy = [_dot01_l(lo_m, g_col[i]) for i in n]
    dcy_row = [_dot01_r(g_row[i], up_m) for i in n]
    dtot = [_dot01_l(same_m, g_col[i]) for i in n]
    dm = [jnp.where(lower, jnp.exp(jnp.where(lower, dcy[i] - dcy_row[i], 0.0)), 0.0) for i in n]
    kb = [k[i] * beta[i] for i in n]
    a_mat = [jnp.where(strict, _dot_nt(kb[i], k[i]) * dm[i], 0.0) for i in n]
    attn = [jnp.where(lower, _dot_nt(q[i], k[i]) * dm[i], 0.0) for i in n]
    edc = [jnp.exp(dcy[i]) for i in n]
    t_inv = _tri_inv(a_mat, ri, ci)
    u = [_dotx(t_inv[i], v[i] * beta[i]) for i in n]
    w = [_dotx(t_inv[i], kb[i] * edc[i]) for i in n]
    qd = [q[i] * edc[i] for i in n]
    kdt = [(k[i] * jnp.exp(dtot[i] - dcy[i])).T for i in n]
    gl = [jnp.exp(dtot[i]) for i in n]
    o_all = []
    if nseq == 1:
        s_cur = [sout_ref[0, h] for h in heads]
        for ch in range(nch):
            ix = [ch * GDN_HEADS + h for h in heads]
            v_new = [u[i] - _dot(w[i], s_cur[h]) for h, i in zip(heads, ix)]
            o_all += [_dot(qd[i], s_cur[h]) + _dot(attn[i], v_new[h]) for h, i in zip(heads, ix)]
            s_cur = [s_cur[h] * gl[i][0:1, :] + _dot(kdt[i], v_new[h]) for h, i in zip(heads, ix)]
        for h in heads:
            sout_ref[0, h] = s_cur[h]
    else:
        for h in heads:
            vn, oq = [], []
            for s in range(nseq):
                s_old = sout_ref[s, h]
                rows = slice(s * ls, (s + 1) * ls)
                vn.append(u[h][rows] - _dot(w[h][rows], s_old))
                oq.append(_dot(qd[h][rows], s_old))
            v_new = jnp.concatenate(vn, axis=0)
            o_all.append(jnp.concatenate(oq, axis=0) + _dot(attn[h], v_new))
            for s in range(nseq):
                kdt_s = jnp.where(_idiv(ci, ls) == s, kdt[h], 0.0)
                sout_ref[s, h] = sout_ref[s, h] * gl[h][s * ls:s * ls + 1, :] + _dot(kdt_s, v_new)
    for i, (ch, h) in enumerate(pairs):
        o = o_all[i]
        o = o * lax.rsqrt(jnp.mean(o * o, axis=-1, keepdims=True) + EPS) * ng_ref[...]
        gt = gate_ref[ch * c:(ch + 1) * c, h * LANE:(h + 1) * LANE]
        o_ref[ch * c:(ch + 1) * c, h * LANE:(h + 1) * LANE] = o * (gt * jax.nn.sigmoid(gt))


def _gdn(qkv, gate, sm, smt, alog_b, dtb_b, ng, s0, s0_off, nb, nt, ls, nch):
    m = qkv.shape[0]
    c = GDN_CHUNK * nch
    nseq = GDN_CHUNK // ls
    row = lambda i, j: (i * nt + j, 0)
    n_state = nb * nseq
    return pl.pallas_call(
        functools.partial(_gdn_kernel, ls=ls, nch=nch),
        grid=(nb, nt),
        in_specs=[pl.BlockSpec((c, GDN_QKV), row),
                  pl.BlockSpec((c, 512), row),
                  pl.BlockSpec((c, LANE), row),
                  pl.BlockSpec((nch, 8, GDN_CHUNK), lambda i, j: (i * nt + j, 0, 0)),
                  pl.BlockSpec((8, LANE), lambda i, j: (0, 0)),
                  pl.BlockSpec((8, LANE), lambda i, j: (0, 0)),
                  pl.BlockSpec((1, LANE), lambda i, j: (0, 0)),
                  pl.BlockSpec((nseq, GDN_HEADS, GDN_DK, GDN_DV), lambda i, j: (s0_off + i, 0, 0, 0))],
        out_specs=[pl.BlockSpec((c, 512), row),
                   pl.BlockSpec((nseq, GDN_HEADS, GDN_DK, GDN_DV), lambda i, j: (i, 0, 0, 0))],
        out_shape=[jax.ShapeDtypeStruct((m, 512), F32),
                   jax.ShapeDtypeStruct((n_state, GDN_HEADS, GDN_DK, GDN_DV), F32)],
        compiler_params=_params(("parallel", "arbitrary")),
    )(qkv, gate, sm, smt, alog_b, dtb_b, ng, s0)


def _cmp_kernel(x_ref, pe_ref, wlo_ref, whi_ref, y_ref):
    x = x_ref[0]
    y_ref[0, :, 0:NSA_KV_W] = _dot(x + pe_ref[0, 0:1, :], wlo_ref[0])
    y_ref[0, :, NSA_KV_W:2 * NSA_KV_W] = _dot(x + pe_ref[0, 1:2, :], whi_ref[0])


def _compress(x, pe, wlo, whi, tr):
    nl, r, _ = x.shape
    tr = min(tr, r)
    return pl.pallas_call(
        _cmp_kernel,
        grid=(nl, r // tr),
        in_specs=[pl.BlockSpec((1, tr, SUB_W), lambda l, i: (l, i, 0)),
                  pl.BlockSpec((1, 2, SUB_W), lambda l, i: (l, 0, 0)),
                  pl.BlockSpec((1, SUB_W, NSA_KV_W), lambda l, i: (l, 0, 0)),
                  pl.BlockSpec((1, SUB_W, NSA_KV_W), lambda l, i: (l, 0, 0))],
        out_specs=pl.BlockSpec((1, tr, 2 * NSA_KV_W), lambda l, i: (l, i, 0)),
        out_shape=jax.ShapeDtypeStruct((nl, r, 2 * NSA_KV_W), F32),
        compiler_params=_params(("parallel", "parallel")),
    )(x, pe, wlo, whi)


def _cmp_pool_kernel(p_ref, pe_ref, wlo_ref, whi_ref, y_ref, x_scr, *, npg):
    for j in range(npg):
        x = p_ref[0, j].T
        for hf in range(2):
            x_scr[hf, j * PAGE_SIZE:(j + 1) * PAGE_SIZE, :] = x[:, hf * LANE:(hf + 1) * LANE]
    n = npg * SUB_PP
    for hf in range(2):
        cols = slice(hf * LANE, (hf + 1) * LANE)
        ylo = jnp.zeros((n, LANE), F32)
        yhi = jnp.zeros((n, LANE), F32)
        for l in range(CMP_STRIDE):
            xl = x_scr[hf, pl.ds(l, n, stride=CMP_STRIDE), :]
            ylo = ylo + _dot(xl + pe_ref[0, 0, l:l + 1, cols], wlo_ref[0, l, cols, cols])
            yhi = yhi + _dot(xl + pe_ref[0, 1, l:l + 1, cols], whi_ref[0, l, cols, cols])
        y_ref[0, :, hf * LANE:(hf + 1) * LANE] = ylo
        y_ref[0, :, NSA_KV_W + hf * LANE:NSA_KV_W + (hf + 1) * LANE] = yhi


def _compress_pool(pages_t, pe, wlo, whi, npg):
    nl, npool = pages_t.shape[:2]
    assert npool % npg == 0
    n = npg * SUB_PP
    return pl.pallas_call(
        functools.partial(_cmp_pool_kernel, npg=npg),
        grid=(nl, npool // npg),
        in_specs=[pl.BlockSpec((1, npg, NSA_KV_W, PAGE_SIZE), lambda l, i: (l, i, 0, 0)),
                  pl.BlockSpec((1, 2, CMP_STRIDE, NSA_KV_W), lambda l, i: (l, 0, 0, 0)),
                  pl.BlockSpec((1, CMP_STRIDE, NSA_KV_W, NSA_KV_W), lambda l, i: (l, 0, 0, 0)),
                  pl.BlockSpec((1, CMP_STRIDE, NSA_KV_W, NSA_KV_W), lambda l, i: (l, 0, 0, 0))],
        out_specs=pl.BlockSpec((1, n, 2 * NSA_KV_W), lambda l, i: (l, i, 0)),
        out_shape=jax.ShapeDtypeStruct((nl, npool * SUB_PP, 2 * NSA_KV_W), F32),
        scratch_shapes=[pltpu.VMEM((2, npg * PAGE_SIZE, LANE), F32)],
        compiler_params=_params(("parallel", "parallel")),
    )(pages_t, pe, wlo, whi)


def _slope(h):
    return 2.0 ** (-8.0 * (h + 1) / NSA_HEADS)


def _make_units(q, stacked):
    qb = q.shape[0]
    units = []
    for g in range(NSA_KV_HEADS):
        hs = range(g * NSA_REP, (g + 1) * NSA_REP)
        slabs = [q[:, h * NSA_HD:(h + 1) * NSA_HD] * (NSA_HD ** -0.5) for h in hs]
        if stacked:
            rows = jnp.concatenate(slabs, axis=0).astype(BF16)
            slope = jnp.concatenate([jnp.full((qb, 1), _slope(h), F32) for h in hs], axis=0)
            units.append([(rows, slope, NSA_REP)])
        else:
            units.append([(s.astype(BF16), _slope(h), 1) for s, h in zip(slabs, hs)])
    return units


def _cmp_branch(units, pq, ckv, ncp):
    qb = pq.shape[0]
    n_ix = _iota((1, ncp), 1)
    c_end = n_ix * CMP_STRIDE + (CMP_LEN - 1)
    c_ctr = (n_ix * CMP_STRIDE).astype(F32) + 0.5 * (CMP_LEN - 1)
    pool = (_idiv(_iota((ncp, LANE), 0), SEL_BLOCK // CMP_STRIDE) == _iota((ncp, LANE), 1)).astype(F32)
    ckv_b = ckv.astype(BF16)
    geo = {}
    for g in range(NSA_KV_HEADS):
        for _, _, rep in units[g]:
            if rep not in geo:
                pqr = _rep(pq, rep)
                geo[rep] = (c_end <= pqr, pqr.astype(F32) - c_ctr)
    flat = [(g, rows, slope, rep) for g in range(NSA_KV_HEADS) for rows, slope, rep in units[g]]
    kc = [ckv_b[:, g * NSA_HD:(g + 1) * NSA_HD] for g in range(NSA_KV_HEADS)]
    vc = [ckv_b[:, (NSA_KV_HEADS + g) * NSA_HD:(NSA_KV_HEADS + g + 1) * NSA_HD] for g in range(NSA_KV_HEADS)]
    s = [_dot_nt(rows, kc[g]) for g, rows, _, _ in flat]
    s = [jnp.where(geo[rep][0], si - slope * geo[rep][1], NEG_INF) for si, (_, _, slope, rep) in zip(s, flat)]
    m = [jnp.max(si, axis=-1, keepdims=True) for si in s]
    e = [jnp.where(geo[rep][0], jnp.exp(si - mi), 0.0) for si, mi, (_, _, _, rep) in zip(s, m, flat)]
    den = [jnp.sum(ei, axis=-1, keepdims=True) for ei in e]
    p = [ei / jnp.where(di > 0.0, di, 1.0) for ei, di in zip(e, den)]
    o = [_dot(pi, vc[g]) for pi, (g, _, _, _) in zip(p, flat)]
    outs, imps = [], []
    for g in range(NSA_KV_HEADS):
        psum = jnp.zeros((qb, ncp), F32)
        og = []
        for pi, oi, (gi, _, _, rep) in zip(p, o, flat):
            if gi == g:
                og.append(oi)
                for r in range(rep):
                    psum = psum + pi[r * qb:(r + 1) * qb]
        outs.append(og)
        imps.append(psum)
    imps = [_dot01_r(ps, pool) for ps in imps]
    return outs, imps


def _select_blocks(imps, pq, ax):
    blk = _iota(imps[0].shape, ax)
    cur = _idiv(pq, SEL_BLOCK)
    forced = (blk == 0) | (blk == cur) | (blk == cur - 1)
    bonus = jnp.where(forced, FORCE_BONUS, 0.0)
    work = [jnp.where(blk <= cur, imp + bonus, NEG_INF) for imp in imps]
    idx = blk.astype(F32)
    sel = [jnp.zeros(w.shape, F32) for w in work]
    for _ in range(N_SEL):
        m = [jnp.max(w, axis=ax, keepdims=True) for w in work]
        first = [jnp.min(jnp.where(w == mi, idx, 2.0 * LANE), axis=ax, keepdims=True) for w, mi in zip(work, m)]
        hit = [idx == f for f in first]
        sel = [jnp.where(h, 1.0, s) for h, s in zip(hit, sel)]
        work = [jnp.where(h, -jnp.inf, w) for h, w in zip(hit, work)]
    return sel


def _slc_tile(units, sel, pq, kvt, k0, tk, causal, flags, m_scr, l_scr, acc_scr):
    key = k0 + _iota((1, tk), 1)
    dist_i = pq - key
    dist = dist_i.astype(F32)
    e_tile = (_iota((LANE, tk), 0) == _idiv(k0 + _iota((LANE, tk), 1), SEL_BLOCK)).astype(BF16)
    ui = 0
    for g in range(NSA_KV_HEADS):
        def group(g=g, ui=ui):
            on = jnp.dot(sel[g].astype(BF16), e_tile, preferred_element_type=F32) > 0.5
            if causal:
                on = on & (dist_i >= 0)
            bias = jnp.where(on, 0.0, NEG_INF)
            kt = kvt[g * NSA_HD:(g + 1) * NSA_HD, :]
            vt = kvt[(NSA_KV_HEADS + g) * NSA_HD:(NSA_KV_HEADS + g + 1) * NSA_HD, :]
            us = units[g]
            ids = [ui + j for j in range(len(us))]
            s = [jnp.dot(rows, kt, preferred_element_type=F32) for rows, _, _ in us]
            s = [si + (_rep(bias, rep) - slope * _rep(dist, rep)) for si, (_, slope, rep) in zip(s, us)]
            m_prev = [m_scr[u][:, 0:1] for u in ids]
            m_new = [jnp.maximum(mp, jnp.max(si, axis=-1, keepdims=True)) for mp, si in zip(m_prev, s)]
            alpha = [jnp.exp(mp - mn) for mp, mn in zip(m_prev, m_new)]
            p = [jnp.exp(si - mn) for si, mn in zip(s, m_new)]
            l_new = [a * l_scr[u][:, 0:1] + jnp.sum(pi, axis=-1, keepdims=True) for a, u, pi in zip(alpha, ids, p)]
            pv = [lax.dot_general(pi.astype(BF16), vt, _NT, preferred_element_type=F32) for pi in p]
            for u, a, pvi, mn, ln in zip(ids, alpha, pv, m_new, l_new):
                acc_scr[u] = a * acc_scr[u] + pvi
                m_scr[u] = jnp.broadcast_to(mn, m_scr.shape[1:])
                l_scr[u] = jnp.broadcast_to(ln, l_scr.shape[1:])

        if flags is None:
            group()
        else:
            pl.when(flags[g])(group)
        ui += len(units[g])


def _win_branch(units, pq, kvt, w0):
    wk = kvt.shape[1]
    dist_i = pq - (w0 + _iota((1, wk), 1))
    dist = dist_i.astype(F32)
    bias = jnp.where((dist_i >= 0) & (dist_i < WINDOW), 0.0, NEG_INF)
    flat = [(g, rows, slope, rep) for g in range(NSA_KV_HEADS) for rows, slope, rep in units[g]]
    kt = [kvt[g * NSA_HD:(g + 1) * NSA_HD, :] for g in range(NSA_KV_HEADS)]
    vt = [kvt[(NSA_KV_HEADS + g) * NSA_HD:(NSA_KV_HEADS + g + 1) * NSA_HD, :] for g in range(NSA_KV_HEADS)]
    s = [jnp.dot(rows, kt[g], preferred_element_type=F32) for g, rows, _, _ in flat]
    s = [si + (_rep(bias, rep) - slope * _rep(dist, rep)) for si, (_, _, slope, rep) in zip(s, flat)]
    m = [jnp.max(si, axis=-1, keepdims=True) for si in s]
    p = [jnp.exp(si - mi) for si, mi in zip(s, m)]
    den = [jnp.sum(pi, axis=-1, keepdims=True) for pi in p]
    pv = [lax.dot_general(pi.astype(BF16), vt[g], _NT, preferred_element_type=F32) for pi, (g, _, _, _) in zip(p, flat)]
    o = [pvi / di for pvi, di in zip(pv, den)]
    return [[oi for oi, (gi, _, _, _) in zip(o, flat) if gi == g] for g in range(NSA_KV_HEADS)]


def _nsa_finish(units, o_c, o_w, qb, sm, ng_ref, o_ref, l_scr, acc_scr):
    gates = jax.nn.sigmoid(sm)
    per_group = len(units[0])
    for h in range(NSA_HEADS):
        g, r = divmod(h, NSA_REP)
        if per_group == 1:
            rows = slice(r * qb, (r + 1) * qb)
            u, j = g, 0
        else:
            rows = slice(0, qb)
            u, j = h, r
        oc = o_c[g][j][rows]
        ow = o_w[g][j][rows]
        o_s = acc_scr[u][rows] / l_scr[u][rows][:, 0:1]
        c0 = SM_G + 3 * h
        o = gates[:, c0:c0 + 1] * oc + gates[:, c0 + 1:c0 + 2] * o_s + gates[:, c0 + 2:c0 + 3] * ow
        o = o * lax.rsqrt(jnp.mean(o * o, axis=-1, keepdims=True) + EPS) * ng_ref[...]
        o_ref[:, h * NSA_HD:(h + 1) * NSA_HD] = o


def _slc_init(m_scr, l_scr, acc_scr):
    m_scr[...] = jnp.full(m_scr.shape, NEG_INF, F32)
    l_scr[...] = jnp.zeros(l_scr.shape, F32)
    acc_scr[...] = jnp.zeros(acc_scr.shape, F32)


def _key_aug(t_col, lane, qblk):
    rel = (_idiv(t_col, SEL_BLOCK) - qblk).astype(F32)
    off = (t_col & (SEL_BLOCK - 1)).astype(F32)
    return jnp.where(lane == 0, rel, jnp.where(lane == 1, off, 0.0)).astype(BF16)


def _nsa_prompt_kernel(qt_ref, smt_ref, y_ref, sk_ref, sv_ref, wk_ref, wv_ref, ngb_ref, o_ref,
                       kc_scr, vct_scr, selb_scr, m_scr, l_scr, acc_scr, *, t_len):
    i = pl.program_id(1)
    nsub = t_len // CMP_STRIDE
    gw = NSA_REP * Q_BLOCK
    half = NSA_KV_W // 2
    groups = range(NSA_KV_HEADS)

    @pl.when(i == 0)
    def _():
        y = y_ref[0]
        ckv = y[:, 0:NSA_KV_W] + pltpu.roll(y[:, NSA_KV_W:2 * NSA_KV_W], nsub - 1, 0)
        kc_scr[...] = ckv[:, 0:half].astype(BF16)
        vct_scr[...] = ckv[:, half:].T.astype(BF16)

    pos0 = i * Q_BLOCK
    qblk = pos0 // SEL_BLOCK
    pq_row = pos0 + _iota((1, Q_BLOCK), 1)
    pq_g = jnp.concatenate([pq_row] * NSA_REP, axis=1)
    qt = qt_ref[0] * (NSA_HD ** -0.5)
    zeros = jnp.zeros((NSA_HD, gw), F32)
    slope_g, rq, rfull = [], [], []
    aug_row = _iota((LANE, gw), 0)
    for g in groups:
        hs = range(g * NSA_REP, (g + 1) * NSA_REP)
        qg = jnp.concatenate([qt[h * NSA_HD:(h + 1) * NSA_HD, :] for h in hs], axis=1)
        rq.append(jnp.concatenate([qg, zeros] if g == 0 else [zeros, qg], axis=0).astype(BF16))
        sl = jnp.concatenate([jnp.full((1, Q_BLOCK), _slope(h), F32) for h in hs], axis=1)
        slope_g.append(sl)
        raug = jnp.where(aug_row == 0, SEL_BLOCK * sl, jnp.where(aug_row == 1, sl, 0.0)).astype(BF16)
        rfull.append(jnp.concatenate([rq[g], raug], axis=0))

    n_col = _iota((nsub, gw), 0)
    ok = (n_col * CMP_STRIDE + (CMP_LEN - 1)) <= pq_g
    dist = pq_g.astype(F32) - ((n_col * CMP_STRIDE).astype(F32) + 0.5 * (CMP_LEN - 1))
    kc = kc_scr[...]
    s = [jnp.dot(kc, rq[g], preferred_element_type=F32) for g in groups]
    s = [jnp.where(ok, s[g] - slope_g[g] * dist, NEG_INF) for g in groups]
    m = [jnp.max(x, axis=0, keepdims=True) for x in s]
    e = [jnp.where(ok, jnp.exp(s[g] - m[g]), 0.0) for g in groups]
    den = [jnp.sum(x, axis=0, keepdims=True) for x in e]
    p = [e[g] / jnp.where(den[g] > 0.0, den[g], 1.0) for g in groups]
    o_c = [jnp.dot(vct_scr[g * NSA_HD:(g + 1) * NSA_HD, :], p[g].astype(BF16), preferred_element_type=F32)
           for g in groups]
    psum = [sum(p[g][:, r * Q_BLOCK:(r + 1) * Q_BLOCK] for r in range(NSA_REP)) for g in groups]
    pool_t = (_iota((LANE, nsub), 0) == _idiv(_iota((LANE, nsub), 1), SEL_BLOCK // CMP_STRIDE)).astype(F32)
    imp_t = [_dot01_l(pool_t, psum[g]) for g in groups]

    sel = _select_blocks(imp_t, pq_row, 0)
    for g in groups:
        selb_scr[g] = jnp.where(sel[g] > 0.5, 0.0, NEG_INF)
    used = [jnp.max(x, axis=1, keepdims=True) for x in sel]
    tile_of_blk = _idiv(_iota((LANE, 1), 0), SLC_TILE // SEL_BLOCK)

    m_scr[...] = jnp.full(m_scr.shape, NEG_INF, F32)
    l_scr[...] = jnp.zeros(l_scr.shape, F32)
    acc_scr[...] = jnp.zeros(acc_scr.shape, F32)
    n_tiles = (pos0 + Q_BLOCK + SLC_TILE - 1) // SLC_TILE
    blk_per_tile = SLC_TILE // SEL_BLOCK

    def slc_tile(kt, causal, flags):
        k0 = kt * SLC_TILE
        t_col = k0 + _iota((SLC_TILE, LANE), 0)
        lhs = jnp.concatenate([sk_ref[0, kt], _key_aug(t_col, _iota((SLC_TILE, LANE), 1), qblk)], axis=1)
        for g in groups:
            def group(g=g):
                bias = jnp.concatenate(
                    [jnp.broadcast_to(selb_scr[g, pl.ds(kt * blk_per_tile + j, 1), :], (SEL_BLOCK, Q_BLOCK))
                     for j in range(blk_per_tile)], axis=0)
                if causal:
                    bias = jnp.where(t_col <= pq_row, bias, NEG_INF)
                sc = jnp.dot(lhs, rfull[g], preferred_element_type=F32) + jnp.concatenate([bias] * NSA_REP, axis=1)
                m_prev = m_scr[g, 0:1, :]
                m_new = jnp.maximum(m_prev, jnp.max(sc, axis=0, keepdims=True))
                alpha = jnp.exp(m_prev - m_new)
                pr = jnp.exp(sc - m_new)
                l_scr[g, 0:1, :] = alpha * l_scr[g, 0:1, :] + jnp.sum(pr, axis=0, keepdims=True)
                pv = jnp.dot(sv_ref[0, kt, g * NSA_HD:(g + 1) * NSA_HD, :], pr.astype(BF16),
                             preferred_element_type=F32)
                acc_scr[g] = alpha * acc_scr[g] + pv
                m_scr[g, 0:1, :] = m_new

            if flags is None:
                group()
            else:
                pl.when(flags[g])(group)

    def body(kt, carry):
        flags = [jnp.max(jnp.where(tile_of_blk == kt, u, 0.0)) > 0.5 for u in used]
        slc_tile(kt, False, flags)
        return carry

    lax.fori_loop(0, n_tiles - 1, body, 0)
    slc_tile(n_tiles - 1, True, None)

    wt0 = jnp.maximum(i - (WIN_TILES - 1), 0)
    wkeys = WIN_TILES * Q_BLOCK
    kw = jnp.concatenate([wk_ref[0, wt0 + j] for j in range(WIN_TILES)], axis=0)
    vw = jnp.concatenate([wv_ref[0, wt0 + j] for j in range(WIN_TILES)], axis=1)
    tw = wt0 * Q_BLOCK + _iota((wkeys, LANE), 0)
    lhs_w = jnp.concatenate([kw, _key_aug(tw, _iota((wkeys, LANE), 1), qblk)], axis=1)
    dist_w = pq_row - tw
    bias_w = jnp.where((dist_w >= 0) & (dist_w < WINDOW), 0.0, NEG_INF)
    bias_w = jnp.concatenate([bias_w] * NSA_REP, axis=1)
    sw = [jnp.dot(lhs_w, rfull[g], preferred_element_type=F32) + bias_w for g in groups]
    mw = [jnp.max(x, axis=0, keepdims=True) for x in sw]
    pw = [jnp.exp(sw[g] - mw[g]) for g in groups]
    dw = [jnp.sum(x, axis=0, keepdims=True) for x in pw]
    o_w = [jnp.dot(vw[g * NSA_HD:(g + 1) * NSA_HD, :], pw[g].astype(BF16), preferred_element_type=F32) / dw[g]
           for g in groups]

    gates = jax.nn.sigmoid(smt_ref[0])
    outs = []
    for h in range(NSA_HEADS):
        g, r = divmod(h, NSA_REP)
        lanes = slice(r * Q_BLOCK, (r + 1) * Q_BLOCK)
        o_s = acc_scr[g][:, lanes] / l_scr[g, 0:1, lanes]
        c0 = SM_G + 3 * h
        o = gates[c0:c0 + 1] * o_c[g][:, lanes] + gates[c0 + 1:c0 + 2] * o_s + gates[c0 + 2:c0 + 3] * o_w[g][:, lanes]
        outs.append(o * lax.rsqrt(jnp.mean(o * o, axis=0, keepdims=True) + EPS) * ngb_ref[...])
    o_ref[...] = jnp.concatenate(outs, axis=0).T


def _nsa_prompt(nq_t, smt, y, sk, sv, wk, wv, ngb, b, t_len):
    nqb = t_len // Q_BLOCK
    nsub = t_len // CMP_STRIDE
    nst = t_len // SLC_TILE
    half = NSA_KV_W // 2
    gw = NSA_REP * Q_BLOCK
    whole = lambda bi, i: (bi, 0, 0, 0)
    return pl.pallas_call(
        functools.partial(_nsa_prompt_kernel, t_len=t_len),
        grid=(b, nqb),
        in_specs=[pl.BlockSpec((1, 512, Q_BLOCK), lambda bi, i: (bi, 0, i)),
                  pl.BlockSpec((1, 32, Q_BLOCK), lambda bi, i: (bi * nqb + i, 0, 0)),
                  pl.BlockSpec((1, nsub, 2 * NSA_KV_W), lambda bi, i: (bi, 0, 0)),
                  pl.BlockSpec((1, nst, SLC_TILE, half), whole),
                  pl.BlockSpec((1, nst, half, SLC_TILE), whole),
                  pl.BlockSpec((1, nqb, Q_BLOCK, half), whole),
                  pl.BlockSpec((1, nqb, half, Q_BLOCK), whole),
                  pl.BlockSpec((NSA_HD, Q_BLOCK), lambda bi, i: (0, 0))],
        out_specs=pl.BlockSpec((Q_BLOCK, 512), lambda bi, i: (bi * nqb + i, 0)),
        out_shape=jax.ShapeDtypeStruct((b * t_len, 512), F32),
        scratch_shapes=[pltpu.VMEM((nsub, half), BF16),
                        pltpu.VMEM((half, nsub), BF16),
                        pltpu.VMEM((NSA_KV_HEADS, LANE, Q_BLOCK), F32),
                        pltpu.VMEM((NSA_KV_HEADS, 8, gw), F32),
                        pltpu.VMEM((NSA_KV_HEADS, 8, gw), F32),
                        pltpu.VMEM((NSA_KV_HEADS, NSA_HD, gw), F32)],
        compiler_params=_params(("parallel", "arbitrary")),
    )(nq_t, smt, y, sk, sv, wk, wv, ngb)


def _nsa_sample_kernel(pt_ref, q_ref, sm_ref, ynew_ref, slcn_ref, winn_ref, winc_ref, ng_ref, *rest,
                       n_pages, past_len, dec_len):
    del pt_ref
    y_pages = rest[0:n_pages]
    s_pages = rest[n_pages:2 * n_pages]
    o_ref = rest[2 * n_pages]
    y_scr, slc_scr, win_scr, m_scr, l_scr, acc_scr = rest[2 * n_pages + 1:]
    ncp = n_pages * SUB_PP
    tk = slc_scr.shape[1]
    wb = winc_ref.shape[2]

    for p in range(n_pages):
        y_scr[p * SUB_PP:(p + 1) * SUB_PP, :] = y_pages[p][0]
    y_scr[ncp:ncp + 8, :] = ynew_ref[0]
    y = y_scr[...]
    ckv = (y[:, 0:NSA_KV_W] + pltpu.roll(y[:, NSA_KV_W:2 * NSA_KV_W], ncp + 7, 0))[0:ncp]

    def new_cols(ref):
        pad = jnp.zeros((PAGE_SIZE - dec_len, NSA_KV_W), F32)
        return jnp.concatenate([ref[...], pad], axis=0).T.astype(BF16)

    for p in range(n_pages):
        slc_scr[:, p * PAGE_SIZE:(p + 1) * PAGE_SIZE] = s_pages[p][0].astype(BF16)
    slc_scr[:, past_len:tk] = new_cols(slcn_ref)
    win_scr[:, 0:wb] = winc_ref[0].astype(BF16)
    win_scr[:, wb:wb + PAGE_SIZE] = new_cols(winn_ref)

    pq = past_len + _iota((dec_len, 1), 0)
    units = _make_units(q_ref[...], True)
    o_c, imps = _cmp_branch(units, pq, ckv, ncp)
    sel = _select_blocks(imps, pq, 1)
    _slc_init(m_scr, l_scr, acc_scr)
    _slc_tile(units, sel, pq, slc_scr[...], 0, tk, True, None, m_scr, l_scr, acc_scr)
    o_w = _win_branch(units, pq, win_scr[...], past_len - wb)
    _nsa_finish(units, o_c, o_w, dec_len, sm_ref[...], ng_ref, o_ref, l_scr, acc_scr)


def _nsa_sample(page_table, nq, sm, ynew, slc_new, win_new, win_cache_t, ng, y_pool, slc_pool_t, layer, n_pool,
                past_len, dec_len):
    db, n_pages = page_table.shape
    wb = win_cache_t.shape[2]
    tk = past_len + PAGE_SIZE
    wk = wb + PAGE_SIZE
    rows = NSA_REP * dec_len
    row = lambda b, pt: (b, 0)

    def page_map(k):
        return lambda b, pt: (layer * n_pool + pt[b, k], 0, 0)

    in_specs = [pl.BlockSpec((dec_len, 512), row),
                pl.BlockSpec((dec_len, LANE), row),
                pl.BlockSpec((1, 8, 2 * NSA_KV_W), lambda b, pt: (b, 0, 0)),
                pl.BlockSpec((dec_len, NSA_KV_W), row),
                pl.BlockSpec((dec_len, NSA_KV_W), row),
                pl.BlockSpec((1, NSA_KV_W, wb), lambda b, pt: (layer * db + b, 0, 0)),
                pl.BlockSpec((1, NSA_HD), lambda b, pt: (0, 0))]
    in_specs += [pl.BlockSpec((1, SUB_PP, 2 * NSA_KV_W), page_map(k)) for k in range(n_pages)]
    in_specs += [pl.BlockSpec((1, NSA_KV_W, PAGE_SIZE), page_map(k)) for k in range(n_pages)]
    grid_spec = pltpu.PrefetchScalarGridSpec(
        num_scalar_prefetch=1,
        grid=(db,),
        in_specs=in_specs,
        out_specs=pl.BlockSpec((dec_len, 512), row),
        scratch_shapes=[pltpu.VMEM((n_pages * SUB_PP + 8, 2 * NSA_KV_W), F32),
                        pltpu.VMEM((NSA_KV_W, tk), BF16),
                        pltpu.VMEM((NSA_KV_W, wk), BF16),
                        pltpu.VMEM((NSA_KV_HEADS, rows, LANE), F32),
                        pltpu.VMEM((NSA_KV_HEADS, rows, LANE), F32),
                        pltpu.VMEM((NSA_KV_HEADS, rows, NSA_HD), F32)])
    return pl.pallas_call(
        functools.partial(_nsa_sample_kernel, n_pages=n_pages, past_len=past_len, dec_len=dec_len),
        grid_spec=grid_spec,
        out_shape=jax.ShapeDtypeStruct((db * dec_len, 512), F32),
        compiler_params=_params(("arbitrary",)),
    )(page_table, nq, sm, ynew, slc_new, win_new, win_cache_t, ng,
      *([y_pool] * n_pages), *([slc_pool_t] * n_pages))


def _proj_columns():
    sizes = (GDN_QKV, GDN_HEADS, GDN_HEADS, GDN_HEADS * GDN_DV, NSA_HEADS * NSA_HD,
             NSA_KV_W, NSA_KV_W, NSA_KV_W, 3 * NSA_HEADS)
    off = np.concatenate([[0], np.cumsum(sizes)])
    seg = lambda i: np.arange(off[i], off[i + 1])
    order = np.concatenate([seg(0), seg(3), seg(4), seg(5), seg(6), seg(7), seg(1), seg(2), seg(8)])
    return order, N_PROJ - order.size


def _cmp_weights(cmp_w):
    nl = cmp_w.shape[0]
    eye_g = jnp.eye(NSA_KV_HEADS, dtype=cmp_w.dtype)
    eye_c = jnp.eye(2, dtype=cmp_w.dtype)

    def half(w):
        big = jnp.einsum('nlcde,cx,gy->nlcgdxye', w, eye_c, eye_g)
        return big.reshape(nl, SUB_W, NSA_KV_W).astype(BF16)

    return half(cmp_w[:, :CMP_STRIDE]), half(cmp_w[:, CMP_STRIDE:])


def _cmp_pe(cmp_pe):
    nl = cmp_pe.shape[0]

    def half(p):
        return jnp.broadcast_to(p[:, :, :, None, :], (nl, CMP_STRIDE, 2, NSA_KV_HEADS, NSA_HD)).reshape(nl, SUB_W)

    return jnp.stack([half(cmp_pe[:, :CMP_STRIDE]), half(cmp_pe[:, CMP_STRIDE:])], axis=1)


def _small_t(sm, c):
    m = sm.shape[0]
    return jnp.transpose(sm[:, :32].reshape(m // c, c, 32), (0, 2, 1))


def _feature_major(a):
    nd = a.ndim
    perm = tuple(range(nd - 4)) + (nd - 3, nd - 2, nd - 1, nd - 4)
    t = jnp.transpose(a, perm)
    return t.reshape(t.shape[:nd - 4] + (NSA_KV_W, t.shape[-1]))


def _row_major_kv(a_t):
    lead = a_t.shape[:-2]
    n = len(lead)
    t = a_t.reshape(lead + (2, NSA_KV_HEADS, NSA_HD, a_t.shape[-1]))
    return jnp.transpose(t, tuple(range(n)) + (n + 3, n, n + 1, n + 2))


def kernel(x_prompt, x_sample, cache_cmp_kv, cache_slc_kv, page_table, cache_win_kv, state_gdn, state_conv,
           norm_mix, w_in, conv_w, gdn_a_log, gdn_dt_bias, gdn_norm, nsa_cmp_w, nsa_cmp_pe, nsa_norm,
           w_out, norm_ffn, w_ffn_in, w_ffn_out, norm_final):
    bp, tp, _ = x_prompt.shape
    db, ts, _ = x_sample.shape
    n_pool = cache_cmp_kv.shape[1]
    n_pages = page_table.shape[1]
    past_len = n_pages * PAGE_SIZE
    wb = cache_win_kv.shape[2]
    kv_row = (2, NSA_KV_HEADS, NSA_HD)
    wkeep = min(WINDOW, tp)
    assert GDN_CHUNK % ts == 0 and (db * ts) % GDN_CHUNK == 0 and tp % SLC_TILE == 0
    assert tp >= WIN_TILES * Q_BLOCK and n_pool % POOL_PAGES == 0

    order, pad = _proj_columns()
    w_in_b = jnp.pad(w_in[:, :, order], ((0, 0), (0, 0), (0, pad))).astype(BF16)
    w_t = jnp.transpose(jnp.concatenate([w_in_b[:, :, C_KV:C_SMALL], w_in_b[:, :, C_NQ:C_KV]], axis=2),
                        (0, 2, 1))
    w_out_b = w_out.astype(BF16)
    w_ffn_in_b = w_ffn_in.astype(BF16)
    w_ffn_out_b = w_ffn_out.astype(BF16)
    cw = jnp.pad(conv_w, ((0, 0), (0, 8 - CONV_W), (0, 0)))
    alog_b = jnp.broadcast_to(jnp.pad(gdn_a_log, ((0, 0), (0, 8 - GDN_HEADS)))[:, :, None], (DEPTH, 8, LANE))
    dtb_b = jnp.broadcast_to(jnp.pad(gdn_dt_bias, ((0, 0), (0, 8 - GDN_HEADS)))[:, :, None], (DEPTH, 8, LANE))
    wlo, whi = _cmp_weights(nsa_cmp_w)
    pe2 = _cmp_pe(nsa_cmp_pe)

    y_pool = _compress_pool(_feature_major(cache_cmp_kv), pe2.reshape(DEPTH, 2, CMP_STRIDE, NSA_KV_W),
                            wlo.reshape(DEPTH, CMP_STRIDE, NSA_KV_W, NSA_KV_W),
                            whi.reshape(DEPTH, CMP_STRIDE, NSA_KV_W, NSA_KV_W), POOL_PAGES)
    y_pool = y_pool.reshape(DEPTH * n_pool, SUB_PP, 2 * NSA_KV_W)
    slc_pool_t = _feature_major(cache_slc_kv).reshape(DEPTH * n_pool, NSA_KV_W, PAGE_SIZE)
    win_cache_t = _feature_major(cache_win_kv)
    state_all = state_gdn.reshape(DEPTH * db, GDN_HEADS, GDN_DK, GDN_DV)
    seq_per_chunk = GDN_CHUNK // ts

    xp = x_prompt.reshape(bp * tp, D_MODEL)
    xs = x_sample.reshape(db * ts, D_MODEL)
    zero_hist = jnp.zeros((bp, 8, GDN_QKV), F32)
    zero_state = jnp.zeros((bp, GDN_HEADS, GDN_DK, GDN_DV), F32)
    outs = [[] for _ in range(10)]
    for l in range(DEPTH):
        g_mix = norm_mix[l][None, :]
        g_ffn = norm_ffn[l][None, :]
        gdn_g = gdn_norm[l][None, :]
        nsa_g = nsa_norm[l][None, :]

        (qkv, gate, _, ncmp, _, _, sm, cmp_t, slc_t, win_t, nq_t, sk, sv, wk, wv) = _proj(
            xp, g_mix, w_in_b[l], w_t[l], SLC_TILE, bp)
        smt = _small_t(sm, GDN_CHUNK)
        qkvn = _gdn_prep(qkv.reshape(bp, tp, GDN_QKV), zero_hist, cw[l], 256).reshape(bp * tp, GDN_QKV)
        o_gdn, s_new = _gdn(qkvn, gate, sm, smt, alog_b[l], dtb_b[l], gdn_g, zero_state, 0,
                            bp, tp // (GDN_CHUNK * GDN_NCH), GDN_CHUNK, GDN_NCH)
        y = _compress(ncmp.reshape(1, bp * tp // CMP_STRIDE, SUB_W), pe2[l:l + 1], wlo[l:l + 1],
                      whi[l:l + 1], 512).reshape(bp, tp // CMP_STRIDE, 2 * NSA_KV_W)
        o_nsa = _nsa_prompt(nq_t, smt, y, sk, sv, wk, wv, jnp.broadcast_to(nsa_norm[l][:, None], (NSA_HD, Q_BLOCK)),
                            bp, tp)
        xp = _out_proj(xp, o_gdn, o_nsa, w_out_b[l], 512)
        xp = _ffn(xp, g_ffn, w_ffn_in_b[l], w_ffn_out_b[l], 1024)
        outs[0].append(cmp_t)
        outs[2].append(slc_t)
        outs[4].append(win_t[:, :, tp - wkeep:])
        outs[6].append(s_new)
        outs[8].append(qkv.reshape(bp, tp, GDN_QKV)[:, -(CONV_W - 1):])

        qkv, gate, nq, ncmp, nslc, nwin, sm = _proj(xs, g_mix, w_in_b[l], None, 512)
        hist = jnp.pad(state_conv[l], ((0, 0), (8 - (CONV_W - 1), 0), (0, 0)))
        qkvn = _gdn_prep(qkv.reshape(db, ts, GDN_QKV), hist, cw[l], ts).reshape(db * ts, GDN_QKV)
        o_gdn, s_new = _gdn(qkvn, gate, sm, _small_t(sm, GDN_CHUNK), alog_b[l], dtb_b[l], gdn_g, state_all,
                            l * (db // seq_per_chunk), db // seq_per_chunk, 1, ts, 1)
        new_sub = jnp.pad(ncmp.reshape(db, 1, ts * NSA_KV_W), ((0, 0), (0, 7), (0, SUB_W - ts * NSA_KV_W)))
        ynew = _compress(new_sub.reshape(1, db * 8, SUB_W), pe2[l:l + 1], wlo[l:l + 1], whi[l:l + 1],
                         512).reshape(db, 8, 2 * NSA_KV_W)
        o_nsa = _nsa_sample(page_table, nq, sm, ynew, nslc, nwin, win_cache_t.reshape(DEPTH * db, NSA_KV_W, wb),
                            nsa_g, y_pool, slc_pool_t, l, n_pool, past_len, ts)
        xs = _out_proj(xs, o_gdn, o_nsa, w_out_b[l], 512)
        xs = _ffn(xs, g_ffn, w_ffn_in_b[l], w_ffn_out_b[l], 1024)
        outs[1].append(ncmp.reshape((db, ts) + kv_row))
        outs[3].append(nslc.reshape((db, ts) + kv_row))
        nwin_t = jnp.transpose(nwin.reshape(db, ts, NSA_KV_W), (0, 2, 1))
        outs[5].append(jnp.concatenate([win_cache_t[l], nwin_t], axis=2)[:, :, ts:])
        outs[7].append(s_new)
        outs[9].append(jnp.concatenate([state_conv[l], qkv.reshape(db, ts, GDN_QKV)], axis=1)[:, -(CONV_W - 1):])

    y_prompt = _final_norm(xp, norm_final[None, :], 512).reshape(bp, tp, D_MODEL)
    y_sample = _final_norm(xs, norm_final[None, :], 512).reshape(db, ts, D_MODEL)
    res = [jnp.stack(o) for o in outs]
    for i in (0, 2, 4, 5):
        res[i] = _row_major_kv(res[i])
    return (y_prompt, y_sample) + tuple(res)
```

```python
import functools

import numpy as np
import jax
import jax.numpy as jnp
from jax import lax
from jax.experimental import pallas as pl
from jax.experimental.pallas import tpu as pltpu

F32 = jnp.float32
BF16 = jnp.bfloat16

D_MODEL = 1024
DEPTH = 4
PAGE_SIZE = 128
GDN_HEADS = 4
GDN_DK = 128
GDN_DV = 128
GDN_QKV = GDN_HEADS * (2 * GDN_DK + GDN_DV)
CONV_W = 4
NSA_HEADS = 8
NSA_KV_HEADS = 2
NSA_HD = 64
NSA_REP = NSA_HEADS // NSA_KV_HEADS
CMP_LEN = 32
CMP_STRIDE = 16
SEL_BLOCK = 64
N_SEL = 16
WINDOW = 512
NSA_KV_W = 2 * NSA_KV_HEADS * NSA_HD
D_FF = (8 * D_MODEL + 3 * 256 - 1) // (3 * 256) * 256
NEG_INF = -1e30
FORCE_BONUS = 1e4
EPS = 1e-6

LANE = 128
GDN_CHUNK = 128
GDN_NCH = 2
INV_BLOCK = 16
Q_BLOCK = 128
SLC_TILE = 512
WIN_TILES = WINDOW // Q_BLOCK + 1
SUB_W = CMP_STRIDE * NSA_KV_W
SUB_PP = PAGE_SIZE // CMP_STRIDE
POOL_PAGES = 32
VMEM_LIMIT = 56 * 1024 * 1024

C_QKV, C_GATE, C_NQ, C_KV, C_SMALL = 0, 1536, 2048, 2560, 3328
N_PROJ = 3456
SM_B, SM_A, SM_G = 0, 4, 8

_NT = (((1,), (1,)), ((), ()))


def _params(sem):
    return pltpu.CompilerParams(dimension_semantics=sem, vmem_limit_bytes=VMEM_LIMIT)


def _dot(a, b):
    return jnp.dot(a.astype(BF16), b.astype(BF16), preferred_element_type=F32)


def _dot_nt(a, b):
    return lax.dot_general(a.astype(BF16), b.astype(BF16), _NT, preferred_element_type=F32)


def _split2(x):
    hi = x.astype(BF16)
    lo = (x - hi.astype(F32)).astype(BF16)
    return hi, lo


def _split3(x):
    hi = x.astype(BF16)
    r = x - hi.astype(F32)
    mid = r.astype(BF16)
    lo = (r - mid.astype(F32)).astype(BF16)
    return hi, mid, lo


def _dotx(a, b):
    ah, al = _split2(a)
    bh, bl = _split2(b)
    d = functools.partial(jnp.dot, preferred_element_type=F32)
    return d(jnp.concatenate([ah, al], axis=1), jnp.concatenate([bh, bh], axis=0)) + d(ah, bl)


def _dot01_l(m01, x):
    m = m01.astype(BF16)
    hi, mid, lo = _split3(x)
    d = functools.partial(jnp.dot, preferred_element_type=F32)
    return d(jnp.concatenate([m, m], axis=1), jnp.concatenate([hi, mid], axis=0)) + d(m, lo)


def _dot01_r(x, m01):
    m = m01.astype(BF16)
    hi, mid, lo = _split3(x)
    d = functools.partial(jnp.dot, preferred_element_type=F32)
    return d(hi, m) + (d(mid, m) + d(lo, m))


def _iota(shape, dim):
    return lax.broadcasted_iota(jnp.int32, shape, dim)


def _idiv(x, d):
    sh = int(d).bit_length() - 1
    assert (1 << sh) == d
    return lax.shift_right_logical(x, jnp.int32(sh))


def _rep(x, rep):
    return x if rep == 1 else jnp.concatenate([x] * rep, axis=0)


def _rms_rows(x, g):
    return x * lax.rsqrt(jnp.mean(x * x, axis=-1, keepdims=True) + EPS) * g


def _proj_kernel(x_ref, g_ref, w_ref, *rest, with_t):
    if with_t:
        wt_ref = rest[0]
        rest = rest[1:]
    qkv_ref, gate_ref, nq_ref, cmp_ref, slc_ref, win_ref, sm_ref = rest[:7]
    h = _rms_rows(x_ref[...], g_ref[...]).astype(BF16)
    d = functools.partial(jnp.dot, preferred_element_type=F32)
    qkv_ref[...] = d(h, w_ref[:, C_QKV:C_GATE])
    gate_ref[...] = d(h, w_ref[:, C_GATE:C_NQ])
    nq_ref[...] = d(h, w_ref[:, C_NQ:C_KV])
    kv = d(h, w_ref[:, C_KV:C_SMALL])
    cmp_ref[...] = kv[:, 0:NSA_KV_W]
    slc_ref[...] = kv[:, NSA_KV_W:2 * NSA_KV_W]
    win_ref[...] = kv[:, 2 * NSA_KV_W:3 * NSA_KV_W]
    sm_ref[...] = d(h, w_ref[:, C_SMALL:N_PROJ])
    if with_t:
        cmpt_ref, slct_ref, wint_ref, nqt_ref, sk_ref, sv_ref, wk_ref, wv_ref = rest[7:]
        kvt = lax.dot_general(wt_ref[...], h, _NT, preferred_element_type=F32)
        cmpt_ref[0] = kvt[0:NSA_KV_W]
        slct = kvt[NSA_KV_W:2 * NSA_KV_W]
        wint = kvt[2 * NSA_KV_W:3 * NSA_KV_W]
        slct_ref[0] = slct
        wint_ref[0] = wint
        nqt_ref[0] = kvt[3 * NSA_KV_W:3 * NSA_KV_W + 512]
        half = NSA_KV_W // 2
        sk_ref[0, 0] = kv[:, NSA_KV_W:NSA_KV_W + half].astype(BF16)
        sv_ref[0, 0] = slct[half:].astype(BF16)
        for j in range(wk_ref.shape[1]):
            rows = slice(j * Q_BLOCK, (j + 1) * Q_BLOCK)
            wk_ref[0, j] = kv[rows, 2 * NSA_KV_W:2 * NSA_KV_W + half].astype(BF16)
            wv_ref[0, j] = wint[half:, rows].astype(BF16)


def _proj(x, g, w, wt, tm, batch=None):
    m = x.shape[0]
    tm = min(tm, m)
    with_t = wt is not None
    widths = (GDN_QKV, 512, 512, NSA_KV_W, NSA_KV_W, NSA_KV_W, LANE)
    in_specs = [pl.BlockSpec((tm, D_MODEL), lambda i: (i, 0)),
                pl.BlockSpec((1, D_MODEL), lambda i: (0, 0)),
                pl.BlockSpec((D_MODEL, N_PROJ), lambda i: (0, 0))]
    out_specs = [pl.BlockSpec((tm, wd), lambda i: (i, 0)) for wd in widths]
    out_shape = [jax.ShapeDtypeStruct((m, wd), F32) for wd in widths]
    args = [x, g, w]
    if with_t:
        t_len = m // batch
        nt = t_len // tm
        assert tm == SLC_TILE and t_len % tm == 0
        in_specs.append(pl.BlockSpec((wt.shape[0], D_MODEL), lambda i: (0, 0)))
        args.append(wt)
        tmap = lambda i: (i // nt, 0, i % nt)
        tile = lambda i: (i // nt, i % nt, 0, 0)
        half = NSA_KV_W // 2
        nwt = tm // Q_BLOCK
        out_specs += [pl.BlockSpec((1, NSA_KV_W, tm), tmap)] * 3 + [pl.BlockSpec((1, 512, tm), tmap)]
        out_shape += [jax.ShapeDtypeStruct((batch, NSA_KV_W, t_len), F32)] * 3
        out_shape += [jax.ShapeDtypeStruct((batch, 512, t_len), F32)]
        out_specs += [pl.BlockSpec((1, 1, tm, half), tile),
                      pl.BlockSpec((1, 1, half, tm), tile),
                      pl.BlockSpec((1, nwt, Q_BLOCK, half), tile),
                      pl.BlockSpec((1, nwt, half, Q_BLOCK), tile)]
        out_shape += [jax.ShapeDtypeStruct((batch, nt, tm, half), BF16),
                      jax.ShapeDtypeStruct((batch, nt, half, tm), BF16),
                      jax.ShapeDtypeStruct((batch, t_len // Q_BLOCK, Q_BLOCK, half), BF16),
                      jax.ShapeDtypeStruct((batch, t_len // Q_BLOCK, half, Q_BLOCK), BF16)]
    return pl.pallas_call(
        functools.partial(_proj_kernel, with_t=with_t),
        grid=(m // tm,),
        in_specs=in_specs,
        out_specs=out_specs,
        out_shape=out_shape,
        compiler_params=_params(("parallel",)),
    )(*args)


def _out_kernel(x_ref, a1_ref, a2_ref, w_ref, o_ref):
    half = w_ref.shape[0] // 2
    o_ref[...] = x_ref[...] + (_dot(a1_ref[...], w_ref[:half, :]) + _dot(a2_ref[...], w_ref[half:, :]))


def _out_proj(x, a1, a2, w, tm):
    m = x.shape[0]
    tm = min(tm, m)
    return pl.pallas_call(
        _out_kernel,
        grid=(m // tm,),
        in_specs=[pl.BlockSpec((tm, D_MODEL), lambda i: (i, 0)),
                  pl.BlockSpec((tm, 512), lambda i: (i, 0)),
                  pl.BlockSpec((tm, 512), lambda i: (i, 0)),
                  pl.BlockSpec((D_MODEL, D_MODEL), lambda i: (0, 0))],
        out_specs=pl.BlockSpec((tm, D_MODEL), lambda i: (i, 0)),
        out_shape=jax.ShapeDtypeStruct((m, D_MODEL), F32),
        compiler_params=_params(("parallel",)),
    )(x, a1, a2, w)


def _ffn_kernel(x_ref, g_ref, wg_ref, wu_ref, wo_ref, o_ref, h_scr, acc_scr):
    f = pl.program_id(1)

    @pl.when(f == 0)
    def _():
        h_scr[...] = _rms_rows(x_ref[...], g_ref[...]).astype(BF16)
        acc_scr[...] = jnp.zeros_like(acc_scr)

    h = h_scr[...]
    gt = jnp.dot(h, wg_ref[...], preferred_element_type=F32)
    up = jnp.dot(h, wu_ref[...], preferred_element_type=F32)
    act = (gt * jax.nn.sigmoid(gt)) * up
    acc_scr[...] += jnp.dot(act.astype(BF16), wo_ref[...], preferred_element_type=F32)

    @pl.when(f == pl.num_programs(1) - 1)
    def _():
        o_ref[...] = x_ref[...] + acc_scr[...]


def _ffn(x, g, w_in, w_out, tm):
    m = x.shape[0]
    tm = min(tm, m)
    nf = 2
    tf = D_FF // nf
    return pl.pallas_call(
        _ffn_kernel,
        grid=(m // tm, nf),
        in_specs=[pl.BlockSpec((tm, D_MODEL), lambda i, f: (i, 0)),
                  pl.BlockSpec((1, D_MODEL), lambda i, f: (0, 0)),
                  pl.BlockSpec((D_MODEL, tf), lambda i, f: (0, f)),
                  pl.BlockSpec((D_MODEL, tf), lambda i, f: (0, nf + f)),
                  pl.BlockSpec((tf, D_MODEL), lambda i, f: (f, 0))],
        out_specs=pl.BlockSpec((tm, D_MODEL), lambda i, f: (i, 0)),
        out_shape=jax.ShapeDtypeStruct((m, D_MODEL), F32),
        scratch_shapes=[pltpu.VMEM((tm, D_MODEL), BF16), pltpu.VMEM((tm, D_MODEL), F32)],
        compiler_params=_params(("parallel", "arbitrary")),
    )(x, g, w_in, w_in, w_out)


def _final_norm_kernel(x_ref, g_ref, o_ref):
    o_ref[...] = _rms_rows(x_ref[...], g_ref[...])


def _final_norm(x, g, tm):
    m = x.shape[0]
    tm = min(tm, m)
    return pl.pallas_call(
        _final_norm_kernel,
        grid=(m // tm,),
        in_specs=[pl.BlockSpec((tm, D_MODEL), lambda i: (i, 0)),
                  pl.BlockSpec((1, D_MODEL), lambda i: (0, 0))],
        out_specs=pl.BlockSpec((tm, D_MODEL), lambda i: (i, 0)),
        out_shape=jax.ShapeDtypeStruct((m, D_MODEL), F32),
        compiler_params=_params(("parallel",)),
    )(x, g)


def _prep_kernel(raw_ref, hist_ref, cw_ref, o_ref, ext_scr, *, tc):
    @pl.when(pl.program_id(1) == 0)
    def _():
        ext_scr[0:8, :] = hist_ref[0]

    ext_scr[8:8 + tc, :] = raw_ref[0]
    y = jnp.zeros((tc, GDN_QKV), F32)
    for j in range(CONV_W):
        y = y + ext_scr[pl.ds(8 - (CONV_W - 1) + j, tc), :] * cw_ref[j:j + 1, :]
    tail = ext_scr[tc:tc + 8, :]
    ext_scr[0:8, :] = tail
    y = y * jax.nn.sigmoid(y)
    for s in range(GDN_QKV // LANE):
        blk = y[:, s * LANE:(s + 1) * LANE]
        if s < 2 * GDN_HEADS:
            blk = blk * lax.rsqrt(jnp.sum(blk * blk, axis=-1, keepdims=True) + EPS)
            if s < GDN_HEADS:
                blk = blk * (GDN_DK ** -0.5)
        o_ref[0, :, s * LANE:(s + 1) * LANE] = blk


def _gdn_prep(raw, hist, cw, tc):
    b, t, _ = raw.shape
    return pl.pallas_call(
        functools.partial(_prep_kernel, tc=tc),
        grid=(b, t // tc),
        in_specs=[pl.BlockSpec((1, tc, GDN_QKV), lambda i, j: (i, j, 0)),
                  pl.BlockSpec((1, 8, GDN_QKV), lambda i, j: (i, 0, 0)),
                  pl.BlockSpec((8, GDN_QKV), lambda i, j: (0, 0))],
        out_specs=pl.BlockSpec((1, tc, GDN_QKV), lambda i, j: (i, j, 0)),
        out_shape=jax.ShapeDtypeStruct((b, t, GDN_QKV), F32),
        scratch_shapes=[pltpu.VMEM((tc + 8, GDN_QKV), F32)],
        compiler_params=_params(("parallel", "arbitrary")),
    )(raw, hist, cw)


def _tri_inv(mats, ri, ci):
    c = mats[0].shape[0]
    eye = (ri == ci).astype(F32)
    bd = _idiv(ri, INV_BLOCK) == _idiv(ci, INV_BLOCK)
    ad = [jnp.where(bd, a, 0.0) for a in mats]
    ao = [a - d for a, d in zip(mats, ad)]
    pw = ad
    td = [eye - d for d in ad]
    k = 2
    while k < INV_BLOCK:
        pw = [_dotx(p, p) for p in pw]
        td = [_dotx(t, eye + p) for t, p in zip(td, pw)]
        k *= 2
    n = [_dotx(t, o) for t, o in zip(td, ao)]
    tn = [eye - x for x in n]
    pw = n
    k = 2
    while k < c // INV_BLOCK:
        pw = [_dotx(p, p) for p in pw]
        tn = [_dotx(t, eye + p) for t, p in zip(tn, pw)]
        k *= 2
    return [_dotx(a, b) for a, b in zip(tn, td)]


def _softplus(x):
    return jnp.maximum(x, 0.0) + jnp.log1p(jnp.exp(-jnp.abs(x)))


def _gdn_kernel(qkv_ref, gate_ref, sm_ref, alog_ref, dtb_ref, ng_ref, s0_ref,
                o_ref, sout_ref, *, ls, nch):
    c = GDN_CHUNK
    nseq = c // ls
    assert nseq == 1 or nch == 1
    heads = range(GDN_HEADS)
    pairs = [(ch, h) for ch in range(nch) for h in heads]

    @pl.when(pl.program_id(1) == 0)
    def _():
        sout_ref[...] = s0_ref[...]

    ri = _iota((c, c), 0)
    ci = _iota((c, c), 1)
    same = _idiv(ri, ls) == _idiv(ci, ls)
    lower = same & (ci <= ri)
    strict = same & (ci < ri)
    lo_m = lower.astype(F32)
    same_m = same.astype(F32)
    sm = [sm_ref[ch * c:(ch + 1) * c, :] for ch in range(nch)]

    def slab(ch, s):
        return qkv_ref[ch * c:(ch + 1) * c, s * LANE:(s + 1) * LANE]

    q = [slab(ch, h) for ch, h in pairs]
    k = [slab(ch, GDN_HEADS + h) for ch, h in pairs]
    v = [slab(ch, 2 * GDN_HEADS + h) for ch, h in pairs]
    neg_a = [-jnp.exp(alog_ref[h:h + 1, :]) for h in heads]
    dtb = [dtb_ref[h:h + 1, :] for h in heads]
    beta = [jax.nn.sigmoid(jnp.broadcast_to(sm[ch][:, SM_B + h:SM_B + h + 1], (c, LANE))) for ch, h in pairs]
    g_col = [neg_a[h] * _softplus(jnp.broadcast_to(sm[ch][:, SM_A + h:SM_A + h + 1], (c, LANE)) + dtb[h])
             for ch, h in pairs]
    n = range(len(pairs))
    dcy = [_dot01_l(lo_m, g_col[i]) for i in n]
    dcy_row = [x.T for x in dcy]
    if nseq == 1:
        dtot = [jnp.broadcast_to(x[c - 1:c, :], (c, LANE)) for x in dcy]
    else:
        dtot = [_dot01_l(same_m, g_col[i]) for i in n]
    dm = [jnp.where(lower, jnp.exp(jnp.where(lower, dcy[i] - dcy_row[i], 0.0)), 0.0) for i in n]
    kb = [k[i] * beta[i] for i in n]
    kk = [_dot_nt(jnp.concatenate([kb[i], q[i]], axis=0), k[i]) for i in n]
    a_mat = [jnp.where(strict, kk[i][0:c] * dm[i], 0.0) for i in n]
    attn = [jnp.where(lower, kk[i][c:2 * c] * dm[i], 0.0) for i in n]
    edc = [jnp.exp(dcy[i]) for i in n]
    t_inv = _tri_inv(a_mat, ri, ci)
    uw = [_dotx(t_inv[i], jnp.concatenate([v[i] * beta[i], kb[i] * edc[i]], axis=1)) for i in n]
    u = [x[:, 0:LANE] for x in uw]
    w = [x[:, LANE:2 * LANE] for x in uw]
    qd = [q[i] * edc[i] for i in n]
    kdt = [(k[i] * jnp.exp(dtot[i] - dcy[i])).T for i in n]
    gl = [jnp.exp(dtot[i]) for i in n]
    o_all = []
    if nseq == 1:
        s_cur = [sout_ref[0, h] for h in heads]
        for ch in range(nch):
            ix = [ch * GDN_HEADS + h for h in heads]
            ws = [_dot(jnp.concatenate([w[i], qd[i]], axis=0), s_cur[h]) for h, i in zip(heads, ix)]
            v_new = [u[i] - ws[h][0:c] for h, i in zip(heads, ix)]
            o_all += [ws[h][c:2 * c] + _dot(attn[i], v_new[h]) for h, i in zip(heads, ix)]
            s_cur = [s_cur[h] * gl[i][0:1, :] + _dot(kdt[i], v_new[h]) for h, i in zip(heads, ix)]
        for h in heads:
            sout_ref[0, h] = s_cur[h]
    else:
        for h in heads:
            vn, oq = [], []
            for s in range(nseq):
                s_old = sout_ref[s, h]
                rows = slice(s * ls, (s + 1) * ls)
                vn.append(u[h][rows] - _dot(w[h][rows], s_old))
                oq.append(_dot(qd[h][rows], s_old))
            v_new = jnp.concatenate(vn, axis=0)
            o_all.append(jnp.concatenate(oq, axis=0) + _dot(attn[h], v_new))
            for s in range(nseq):
                kdt_s = jnp.where(_idiv(ci, ls) == s, kdt[h], 0.0)
                sout_ref[s, h] = sout_ref[s, h] * gl[h][s * ls:s * ls + 1, :] + _dot(kdt_s, v_new)
    for i, (ch, h) in enumerate(pairs):
        o = o_all[i]
        o = o * lax.rsqrt(jnp.mean(o * o, axis=-1, keepdims=True) + EPS) * ng_ref[...]
        gt = gate_ref[ch * c:(ch + 1) * c, h * LANE:(h + 1) * LANE]
        o_ref[ch * c:(ch + 1) * c, h * LANE:(h + 1) * LANE] = o * (gt * jax.nn.sigmoid(gt))


def _gdn(qkv, gate, sm, alog_b, dtb_b, ng, s0, s0_off, nb, nt, ls, nch):
    m = qkv.shape[0]
    c = GDN_CHUNK * nch
    nseq = GDN_CHUNK // ls
    row = lambda i, j: (i * nt + j, 0)
    n_state = nb * nseq
    return pl.pallas_call(
        functools.partial(_gdn_kernel, ls=ls, nch=nch),
        grid=(nb, nt),
        in_specs=[pl.BlockSpec((c, GDN_QKV), row),
                  pl.BlockSpec((c, 512), row),
                  pl.BlockSpec((c, LANE), row),
                  pl.BlockSpec((8, LANE), lambda i, j: (0, 0)),
                  pl.BlockSpec((8, LANE), lambda i, j: (0, 0)),
                  pl.BlockSpec((1, LANE), lambda i, j: (0, 0)),
                  pl.BlockSpec((nseq, GDN_HEADS, GDN_DK, GDN_DV), lambda i, j: (s0_off + i, 0, 0, 0))],
        out_specs=[pl.BlockSpec((c, 512), row),
                   pl.BlockSpec((nseq, GDN_HEADS, GDN_DK, GDN_DV), lambda i, j: (i, 0, 0, 0))],
        out_shape=[jax.ShapeDtypeStruct((m, 512), F32),
                   jax.ShapeDtypeStruct((n_state, GDN_HEADS, GDN_DK, GDN_DV), F32)],
        compiler_params=_params(("parallel", "arbitrary")),
    )(qkv, gate, sm, alog_b, dtb_b, ng, s0)


def _cmp_kernel(x_ref, pe_ref, wlo_ref, whi_ref, y_ref):
    x = x_ref[0]
    y_ref[0, :, 0:NSA_KV_W] = _dot(x + pe_ref[0, 0:1, :], wlo_ref[0])
    y_ref[0, :, NSA_KV_W:2 * NSA_KV_W] = _dot(x + pe_ref[0, 1:2, :], whi_ref[0])


def _compress(x, pe, wlo, whi, tr):
    nl, r, _ = x.shape
    tr = min(tr, r)
    return pl.pallas_call(
        _cmp_kernel,
        grid=(nl, r // tr),
        in_specs=[pl.BlockSpec((1, tr, SUB_W), lambda l, i: (l, i, 0)),
                  pl.BlockSpec((1, 2, SUB_W), lambda l, i: (l, 0, 0)),
                  pl.BlockSpec((1, SUB_W, NSA_KV_W), lambda l, i: (l, 0, 0)),
                  pl.BlockSpec((1, SUB_W, NSA_KV_W), lambda l, i: (l, 0, 0))],
        out_specs=pl.BlockSpec((1, tr, 2 * NSA_KV_W), lambda l, i: (l, i, 0)),
        out_shape=jax.ShapeDtypeStruct((nl, r, 2 * NSA_KV_W), F32),
        compiler_params=_params(("parallel", "parallel")),
    )(x, pe, wlo, whi)


def _cmp_pool_kernel(p_ref, pe_ref, wlo_ref, whi_ref, y_ref, x_scr, *, npg):
    for j in range(npg):
        x = p_ref[0, j].T
        for hf in range(2):
            x_scr[hf, j * PAGE_SIZE:(j + 1) * PAGE_SIZE, :] = x[:, hf * LANE:(hf + 1) * LANE]
    n = npg * SUB_PP
    for hf in range(2):
        cols = slice(hf * LANE, (hf + 1) * LANE)
        ylo = jnp.zeros((n, LANE), F32)
        yhi = jnp.zeros((n, LANE), F32)
        for l in range(CMP_STRIDE):
            xl = x_scr[hf, pl.ds(l, n, stride=CMP_STRIDE), :]
            ylo = ylo + _dot(xl + pe_ref[0, 0, l:l + 1, cols], wlo_ref[0, l, cols, cols])
            yhi = yhi + _dot(xl + pe_ref[0, 1, l:l + 1, cols], whi_ref[0, l, cols, cols])
        y_ref[0, :, hf * LANE:(hf + 1) * LANE] = ylo
        y_ref[0, :, NSA_KV_W + hf * LANE:NSA_KV_W + (hf + 1) * LANE] = yhi


def _compress_pool(pages_t, pe, wlo, whi, npg):
    nl, npool = pages_t.shape[:2]
    assert npool % npg == 0
    n = npg * SUB_PP
    return pl.pallas_call(
        functools.partial(_cmp_pool_kernel, npg=npg),
        grid=(nl, npool // npg),
        in_specs=[pl.BlockSpec((1, npg, NSA_KV_W, PAGE_SIZE), lambda l, i: (l, i, 0, 0)),
                  pl.BlockSpec((1, 2, CMP_STRIDE, NSA_KV_W), lambda l, i: (l, 0, 0, 0)),
                  pl.BlockSpec((1, CMP_STRIDE, NSA_KV_W, NSA_KV_W), lambda l, i: (l, 0, 0, 0)),
                  pl.BlockSpec((1, CMP_STRIDE, NSA_KV_W, NSA_KV_W), lambda l, i: (l, 0, 0, 0))],
        out_specs=pl.BlockSpec((1, n, 2 * NSA_KV_W), lambda l, i: (l, i, 0)),
        out_shape=jax.ShapeDtypeStruct((nl, npool * SUB_PP, 2 * NSA_KV_W), F32),
        scratch_shapes=[pltpu.VMEM((2, npg * PAGE_SIZE, LANE), F32)],
        compiler_params=_params(("parallel", "parallel")),
    )(pages_t, pe, wlo, whi)


def _slope(h):
    return 2.0 ** (-8.0 * (h + 1) / NSA_HEADS)


def _make_units(q, stacked):
    qb = q.shape[0]
    units = []
    for g in range(NSA_KV_HEADS):
        hs = range(g * NSA_REP, (g + 1) * NSA_REP)
        slabs = [q[:, h * NSA_HD:(h + 1) * NSA_HD] * (NSA_HD ** -0.5) for h in hs]
        if stacked:
            rows = jnp.concatenate(slabs, axis=0).astype(BF16)
            slope = jnp.concatenate([jnp.full((qb, 1), _slope(h), F32) for h in hs], axis=0)
            units.append([(rows, slope, NSA_REP)])
        else:
            units.append([(s.astype(BF16), _slope(h), 1) for s, h in zip(slabs, hs)])
    return units


def _cmp_branch(units, pq, ckv, ncp):
    qb = pq.shape[0]
    n_ix = _iota((1, ncp), 1)
    c_end = n_ix * CMP_STRIDE + (CMP_LEN - 1)
    c_ctr = (n_ix * CMP_STRIDE).astype(F32) + 0.5 * (CMP_LEN - 1)
    pool = (_idiv(_iota((ncp, LANE), 0), SEL_BLOCK // CMP_STRIDE) == _iota((ncp, LANE), 1)).astype(F32)
    ckv_b = ckv.astype(BF16)
    geo = {}
    for g in range(NSA_KV_HEADS):
        for _, _, rep in units[g]:
            if rep not in geo:
                pqr = _rep(pq, rep)
                geo[rep] = (c_end <= pqr, pqr.astype(F32) - c_ctr)
    flat = [(g, rows, slope, rep) for g in range(NSA_KV_HEADS) for rows, slope, rep in units[g]]
    kc = [ckv_b[:, g * NSA_HD:(g + 1) * NSA_HD] for g in range(NSA_KV_HEADS)]
    vc = [ckv_b[:, (NSA_KV_HEADS + g) * NSA_HD:(NSA_KV_HEADS + g + 1) * NSA_HD] for g in range(NSA_KV_HEADS)]
    s = [_dot_nt(rows, kc[g]) for g, rows, _, _ in flat]
    s = [jnp.where(geo[rep][0], si - slope * geo[rep][1], NEG_INF) for si, (_, _, slope, rep) in zip(s, flat)]
    m = [jnp.max(si, axis=-1, keepdims=True) for si in s]
    e = [jnp.where(geo[rep][0], jnp.exp(si - mi), 0.0) for si, mi, (_, _, _, rep) in zip(s, m, flat)]
    den = [jnp.sum(ei, axis=-1, keepdims=True) for ei in e]
    p = [ei / jnp.where(di > 0.0, di, 1.0) for ei, di in zip(e, den)]
    o = [_dot(pi, vc[g]) for pi, (g, _, _, _) in zip(p, flat)]
    outs, imps = [], []
    for g in range(NSA_KV_HEADS):
        psum = jnp.zeros((qb, ncp), F32)
        og = []
        for pi, oi, (gi, _, _, rep) in zip(p, o, flat):
            if gi == g:
                og.append(oi)
                for r in range(rep):
                    psum = psum + pi[r * qb:(r + 1) * qb]
        outs.append(og)
        imps.append(psum)
    imps = [_dot01_r(ps, pool) for ps in imps]
    return outs, imps


def _select_blocks(imps, pq, ax):
    blk = _iota(imps[0].shape, ax)
    cur = _idiv(pq, SEL_BLOCK)
    forced = (blk == 0) | (blk == cur) | (blk == cur - 1)
    bonus = jnp.where(forced, FORCE_BONUS, 0.0)
    work = [jnp.where(blk <= cur, imp + bonus, NEG_INF) for imp in imps]
    idx = blk.astype(F32)
    sel = [jnp.zeros(w.shape, F32) for w in work]
    for _ in range(N_SEL):
        m = [jnp.max(w, axis=ax, keepdims=True) for w in work]
        first = [jnp.min(jnp.where(w == mi, idx, 2.0 * LANE), axis=ax, keepdims=True) for w, mi in zip(work, m)]
        hit = [idx == f for f in first]
        sel = [jnp.where(h, 1.0, s) for h, s in zip(hit, sel)]
        work = [jnp.where(h, -jnp.inf, w) for h, w in zip(hit, work)]
    return sel


def _select_blocks_rank(imps, pq, nb):
    qb = imps[0].shape[0]
    blk = _iota((qb, LANE), 1)
    cur = _idiv(pq, SEL_BLOCK)
    forced = (blk == 0) | (blk == cur) | (blk == cur - 1)
    bonus = jnp.where(forced, FORCE_BONUS, 0.0)
    score = [jnp.where(blk <= cur, imp + bonus, NEG_INF) for imp in imps]
    fill = jnp.full((LANE - len(imps) * qb, LANE), NEG_INF, F32)
    st = jnp.concatenate(score + [fill], axis=0).T[0:nb]
    jrow = _iota((nb, LANE), 0)
    cnt = jnp.zeros((nb, LANE), F32)
    for j in range(nb):
        row = st[j:j + 1, :]
        cnt = cnt + jnp.where(jrow > j, jnp.where(row >= st, 1.0, 0.0), jnp.where(row > st, 1.0, 0.0))
    sel_t = jnp.where(cnt < N_SEL, 1.0, 0.0)
    sel = jnp.concatenate([sel_t, jnp.zeros((LANE - nb, LANE), F32)], axis=0).T
    return [sel[g * qb:(g + 1) * qb] for g in range(len(imps))]


def _slc_tile(units, sel, pq, kvt, k0, tk, causal, flags, m_scr, l_scr, acc_scr):
    key = k0 + _iota((1, tk), 1)
    dist_i = pq - key
    dist = dist_i.astype(F32)
    e_tile = (_iota((LANE, tk), 0) == _idiv(k0 + _iota((LANE, tk), 1), SEL_BLOCK)).astype(BF16)
    ui = 0
    for g in range(NSA_KV_HEADS):
        def group(g=g, ui=ui):
            on = jnp.dot(sel[g].astype(BF16), e_tile, preferred_element_type=F32) > 0.5
            if causal:
                on = on & (dist_i >= 0)
            bias = jnp.where(on, 0.0, NEG_INF)
            kt = kvt[g * NSA_HD:(g + 1) * NSA_HD, :]
            vt = kvt[(NSA_KV_HEADS + g) * NSA_HD:(NSA_KV_HEADS + g + 1) * NSA_HD, :]
            us = units[g]
            ids = [ui + j for j in range(len(us))]
            s = [jnp.dot(rows, kt, preferred_element_type=F32) for rows, _, _ in us]
            s = [si + (_rep(bias, rep) - slope * _rep(dist, rep)) for si, (_, slope, rep) in zip(s, us)]
            m_prev = [m_scr[u][:, 0:1] for u in ids]
            m_new = [jnp.maximum(mp, jnp.max(si, axis=-1, keepdims=True)) for mp, si in zip(m_prev, s)]
            alpha = [jnp.exp(mp - mn) for mp, mn in zip(m_prev, m_new)]
            p = [jnp.exp(si - mn) for si, mn in zip(s, m_new)]
            l_new = [a * l_scr[u][:, 0:1] + jnp.sum(pi, axis=-1, keepdims=True) for a, u, pi in zip(alpha, ids, p)]
            pv = [lax.dot_general(pi.astype(BF16), vt, _NT, preferred_element_type=F32) for pi in p]
            for u, a, pvi, mn, ln in zip(ids, alpha, pv, m_new, l_new):
                acc_scr[u] = a * acc_scr[u] + pvi
                m_scr[u] = jnp.broadcast_to(mn, m_scr.shape[1:])
                l_scr[u] = jnp.broadcast_to(ln, l_scr.shape[1:])

        if flags is None:
            group()
        else:
            pl.when(flags[g])(group)
        ui += len(units[g])


def _win_branch(units, pq, kvt, w0):
    wk = kvt.shape[1]
    dist_i = pq - (w0 + _iota((1, wk), 1))
    dist = dist_i.astype(F32)
    bias = jnp.where((dist_i >= 0) & (dist_i < WINDOW), 0.0, NEG_INF)
    flat = [(g, rows, slope, rep) for g in range(NSA_KV_HEADS) for rows, slope, rep in units[g]]
    kt = [kvt[g * NSA_HD:(g + 1) * NSA_HD, :] for g in range(NSA_KV_HEADS)]
    vt = [kvt[(NSA_KV_HEADS + g) * NSA_HD:(NSA_KV_HEADS + g + 1) * NSA_HD, :] for g in range(NSA_KV_HEADS)]
    s = [jnp.dot(rows, kt[g], preferred_element_type=F32) for g, rows, _, _ in flat]
    s = [si + (_rep(bias, rep) - slope * _rep(dist, rep)) for si, (_, _, slope, rep) in zip(s, flat)]
    m = [jnp.max(si, axis=-1, keepdims=True) for si in s]
    p = [jnp.exp(si - mi) for si, mi in zip(s, m)]
    den = [jnp.sum(pi, axis=-1, keepdims=True) for pi in p]
    pv = [lax.dot_general(pi.astype(BF16), vt[g], _NT, preferred_element_type=F32) for pi, (g, _, _, _) in zip(p, flat)]
    o = [pvi / di for pvi, di in zip(pv, den)]
    return [[oi for oi, (gi, _, _, _) in zip(o, flat) if gi == g] for g in range(NSA_KV_HEADS)]


def _nsa_finish(units, o_c, o_w, qb, sm, ng_ref, o_ref, l_scr, acc_scr):
    gates = jax.nn.sigmoid(sm)
    per_group = len(units[0])
    for h in range(NSA_HEADS):
        g, r = divmod(h, NSA_REP)
        if per_group == 1:
            rows = slice(r * qb, (r + 1) * qb)
            u, j = g, 0
        else:
            rows = slice(0, qb)
            u, j = h, r
        oc = o_c[g][j][rows]
        ow = o_w[g][j][rows]
        o_s = acc_scr[u][rows] / l_scr[u][rows][:, 0:1]
        c0 = SM_G + 3 * h
        o = gates[:, c0:c0 + 1] * oc + gates[:, c0 + 1:c0 + 2] * o_s + gates[:, c0 + 2:c0 + 3] * ow
        o = o * lax.rsqrt(jnp.mean(o * o, axis=-1, keepdims=True) + EPS) * ng_ref[...]
        o_ref[:, h * NSA_HD:(h + 1) * NSA_HD] = o


def _slc_init(m_scr, l_scr, acc_scr):
    m_scr[...] = jnp.full(m_scr.shape, NEG_INF, F32)
    l_scr[...] = jnp.zeros(l_scr.shape, F32)
    acc_scr[...] = jnp.zeros(acc_scr.shape, F32)


def _key_aug(t_col, lane, qblk):
    rel = (_idiv(t_col, SEL_BLOCK) - qblk).astype(F32)
    off = (t_col & (SEL_BLOCK - 1)).astype(F32)
    return jnp.where(lane == 0, rel, jnp.where(lane == 1, off, 0.0)).astype(BF16)


def _nsa_prompt_kernel(qt_ref, smt_ref, y_ref, sk_ref, sv_ref, wk_ref, wv_ref, ngb_ref, o_ref,
                       kc_scr, vct_scr, selb_scr, m_scr, l_scr, acc_scr, *, t_len):
    i = pl.program_id(1)
    nsub = t_len // CMP_STRIDE
    gw = NSA_REP * Q_BLOCK
    half = NSA_KV_W // 2
    groups = range(NSA_KV_HEADS)

    @pl.when(i == 0)
    def _():
        y = y_ref[0]
        ckv = y[:, 0:NSA_KV_W] + pltpu.roll(y[:, NSA_KV_W:2 * NSA_KV_W], nsub - 1, 0)
        kc_scr[...] = ckv[:, 0:half].astype(BF16)
        vct_scr[...] = ckv[:, half:].T.astype(BF16)

    pos0 = i * Q_BLOCK
    qblk = pos0 // SEL_BLOCK
    pq_row = pos0 + _iota((1, Q_BLOCK), 1)
    pq_g = jnp.concatenate([pq_row] * NSA_REP, axis=1)
    qt = qt_ref[0] * (NSA_HD ** -0.5)
    zeros = jnp.zeros((NSA_HD, gw), F32)
    slope_g, rq, rfull = [], [], []
    aug_row = _iota((LANE, gw), 0)
    for g in groups:
        hs = range(g * NSA_REP, (g + 1) * NSA_REP)
        qg = jnp.concatenate([qt[h * NSA_HD:(h + 1) * NSA_HD, :] for h in hs], axis=1)
        rq.append(jnp.concatenate([qg, zeros] if g == 0 else [zeros, qg], axis=0).astype(BF16))
        sl = jnp.concatenate([jnp.full((1, Q_BLOCK), _slope(h), F32) for h in hs], axis=1)
        slope_g.append(sl)
        raug = jnp.where(aug_row == 0, SEL_BLOCK * sl, jnp.where(aug_row == 1, sl, 0.0)).astype(BF16)
        rfull.append(jnp.concatenate([rq[g], raug], axis=0))

    n_col = _iota((nsub, gw), 0)
    ok = (n_col * CMP_STRIDE + (CMP_LEN - 1)) <= pq_g
    dist = pq_g.astype(F32) - ((n_col * CMP_STRIDE).astype(F32) + 0.5 * (CMP_LEN - 1))
    kc = kc_scr[...]
    s = [jnp.dot(kc, rq[g], preferred_element_type=F32) for g in groups]
    s = [jnp.where(ok, s[g] - slope_g[g] * dist, NEG_INF) for g in groups]
    m = [jnp.max(x, axis=0, keepdims=True) for x in s]
    e = [jnp.where(ok, jnp.exp(s[g] - m[g]), 0.0) for g in groups]
    den = [jnp.sum(x, axis=0, keepdims=True) for x in e]
    p = [e[g] / jnp.where(den[g] > 0.0, den[g], 1.0) for g in groups]
    o_c = [jnp.dot(vct_scr[g * NSA_HD:(g + 1) * NSA_HD, :], p[g].astype(BF16), preferred_element_type=F32)
           for g in groups]
    psum = [sum(p[g][:, r * Q_BLOCK:(r + 1) * Q_BLOCK] for r in range(NSA_REP)) for g in groups]
    pool_t = (_iota((LANE, nsub), 0) == _idiv(_iota((LANE, nsub), 1), SEL_BLOCK // CMP_STRIDE)).astype(F32)
    imp_t = [_dot01_l(pool_t, psum[g]) for g in groups]

    sel = _select_blocks(imp_t, pq_row, 0)
    for g in groups:
        selb_scr[g] = jnp.where(sel[g] > 0.5, 0.0, NEG_INF)
    used = [jnp.max(x, axis=1, keepdims=True) for x in sel]
    tile_of_blk = _idiv(_iota((LANE, 1), 0), SLC_TILE // SEL_BLOCK)

    m_scr[...] = jnp.full(m_scr.shape, NEG_INF, F32)
    l_scr[...] = jnp.zeros(l_scr.shape, F32)
    acc_scr[...] = jnp.zeros(acc_scr.shape, F32)
    n_tiles = (pos0 + Q_BLOCK + SLC_TILE - 1) // SLC_TILE
    blk_per_tile = SLC_TILE // SEL_BLOCK

    def slc_tile(kt, causal, flags):
        k0 = kt * SLC_TILE
        t_col = k0 + _iota((SLC_TILE, LANE), 0)
        lhs = jnp.concatenate([sk_ref[0, kt], _key_aug(t_col, _iota((SLC_TILE, LANE), 1), qblk)], axis=1)
        for g in groups:
            def group(g=g):
                bias = jnp.concatenate(
                    [jnp.broadcast_to(selb_scr[g, pl.ds(kt * blk_per_tile + j, 1), :], (SEL_BLOCK, Q_BLOCK))
                     for j in range(blk_per_tile)], axis=0)
                if causal:
                    bias = jnp.where(t_col <= pq_row, bias, NEG_INF)
                sc = jnp.dot(lhs, rfull[g], preferred_element_type=F32) + jnp.concatenate([bias] * NSA_REP, axis=1)
                m_prev = m_scr[g, 0:1, :]
                m_new = jnp.maximum(m_prev, jnp.max(sc, axis=0, keepdims=True))
                alpha = jnp.exp(m_prev - m_new)
                pr = jnp.exp(sc - m_new)
                l_scr[g, 0:1, :] = alpha * l_scr[g, 0:1, :] + jnp.sum(pr, axis=0, keepdims=True)
                pv = jnp.dot(sv_ref[0, kt, g * NSA_HD:(g + 1) * NSA_HD, :], pr.astype(BF16),
                             preferred_element_type=F32)
                acc_scr[g] = alpha * acc_scr[g] + pv
                m_scr[g, 0:1, :] = m_new

            if flags is None:
                group()
            else:
                pl.when(flags[g])(group)

    def body(kt, carry):
        flags = [jnp.max(jnp.where(tile_of_blk == kt, u, 0.0)) > 0.5 for u in used]
        slc_tile(kt, False, flags)
        return carry

    lax.fori_loop(0, n_tiles - 1, body, 0)
    slc_tile(n_tiles - 1, True, None)

    wt0 = jnp.maximum(i - (WIN_TILES - 1), 0)
    wkeys = WIN_TILES * Q_BLOCK
    kw = jnp.concatenate([wk_ref[0, wt0 + j] for j in range(WIN_TILES)], axis=0)
    vw = jnp.concatenate([wv_ref[0, wt0 + j] for j in range(WIN_TILES)], axis=1)
    tw = wt0 * Q_BLOCK + _iota((wkeys, LANE), 0)
    lhs_w = jnp.concatenate([kw, _key_aug(tw, _iota((wkeys, LANE), 1), qblk)], axis=1)
    dist_w = pq_row - tw
    bias_w = jnp.where((dist_w >= 0) & (dist_w < WINDOW), 0.0, NEG_INF)
    bias_w = jnp.concatenate([bias_w] * NSA_REP, axis=1)
    sw = [jnp.dot(lhs_w, rfull[g], preferred_element_type=F32) + bias_w for g in groups]
    mw = [jnp.max(x, axis=0, keepdims=True) for x in sw]
    pw = [jnp.exp(sw[g] - mw[g]) for g in groups]
    dw = [jnp.sum(x, axis=0, keepdims=True) for x in pw]
    o_w = [jnp.dot(vw[g * NSA_HD:(g + 1) * NSA_HD, :], pw[g].astype(BF16), preferred_element_type=F32) / dw[g]
           for g in groups]

    gates = jax.nn.sigmoid(smt_ref[0])
    outs = []
    for h in range(NSA_HEADS):
        g, r = divmod(h, NSA_REP)
        lanes = slice(r * Q_BLOCK, (r + 1) * Q_BLOCK)
        o_s = acc_scr[g][:, lanes] / l_scr[g, 0:1, lanes]
        c0 = SM_G + 3 * h
        o = gates[c0:c0 + 1] * o_c[g][:, lanes] + gates[c0 + 1:c0 + 2] * o_s + gates[c0 + 2:c0 + 3] * o_w[g][:, lanes]
        outs.append(o * lax.rsqrt(jnp.mean(o * o, axis=0, keepdims=True) + EPS) * ngb_ref[...])
    o_ref[...] = jnp.concatenate(outs, axis=0).T


def _nsa_prompt(nq_t, smt, y, sk, sv, wk, wv, ngb, b, t_len):
    nqb = t_len // Q_BLOCK
    nsub = t_len // CMP_STRIDE
    nst = t_len // SLC_TILE
    half = NSA_KV_W // 2
    gw = NSA_REP * Q_BLOCK
    whole = lambda bi, i: (bi, 0, 0, 0)
    return pl.pallas_call(
        functools.partial(_nsa_prompt_kernel, t_len=t_len),
        grid=(b, nqb),
        in_specs=[pl.BlockSpec((1, 512, Q_BLOCK), lambda bi, i: (bi, 0, i)),
                  pl.BlockSpec((1, 32, Q_BLOCK), lambda bi, i: (bi * nqb + i, 0, 0)),
                  pl.BlockSpec((1, nsub, 2 * NSA_KV_W), lambda bi, i: (bi, 0, 0)),
                  pl.BlockSpec((1, nst, SLC_TILE, half), whole),
                  pl.BlockSpec((1, nst, half, SLC_TILE), whole),
                  pl.BlockSpec((1, nqb, Q_BLOCK, half), whole),
                  pl.BlockSpec((1, nqb, half, Q_BLOCK), whole),
                  pl.BlockSpec((NSA_HD, Q_BLOCK), lambda bi, i: (0, 0))],
        out_specs=pl.BlockSpec((Q_BLOCK, 512), lambda bi, i: (bi * nqb + i, 0)),
        out_shape=jax.ShapeDtypeStruct((b * t_len, 512), F32),
        scratch_shapes=[pltpu.VMEM((nsub, half), BF16),
                        pltpu.VMEM((half, nsub), BF16),
                        pltpu.VMEM((NSA_KV_HEADS, LANE, Q_BLOCK), F32),
                        pltpu.VMEM((NSA_KV_HEADS, 8, gw), F32),
                        pltpu.VMEM((NSA_KV_HEADS, 8, gw), F32),
                        pltpu.VMEM((NSA_KV_HEADS, NSA_HD, gw), F32)],
        compiler_params=_params(("parallel", "arbitrary")),
    )(nq_t, smt, y, sk, sv, wk, wv, ngb)


def _nsa_sample_kernel(pt_ref, q_ref, sm_ref, ynew_ref, slcn_ref, winn_ref, winc_ref, ng_ref, *rest,
                       n_pages, past_len, dec_len):
    del pt_ref
    y_pages = rest[0:n_pages]
    s_pages = rest[n_pages:2 * n_pages]
    o_ref = rest[2 * n_pages]
    y_scr, slc_scr, win_scr, m_scr, l_scr, acc_scr = rest[2 * n_pages + 1:]
    ncp = n_pages * SUB_PP
    tk = slc_scr.shape[1]
    wb = winc_ref.shape[2]

    for p in range(n_pages):
        y_scr[p * SUB_PP:(p + 1) * SUB_PP, :] = y_pages[p][0]
    y_scr[ncp:ncp + 8, :] = ynew_ref[0]
    y = y_scr[...]
    ckv = (y[:, 0:NSA_KV_W] + pltpu.roll(y[:, NSA_KV_W:2 * NSA_KV_W], ncp + 7, 0))[0:ncp]

    def new_cols(ref):
        pad = jnp.zeros((PAGE_SIZE - dec_len, NSA_KV_W), F32)
        return jnp.concatenate([ref[...], pad], axis=0).T.astype(BF16)

    for p in range(n_pages):
        slc_scr[:, p * PAGE_SIZE:(p + 1) * PAGE_SIZE] = s_pages[p][0].astype(BF16)
    slc_scr[:, past_len:tk] = new_cols(slcn_ref)
    win_scr[:, 0:wb] = winc_ref[0].astype(BF16)
    win_scr[:, wb:wb + PAGE_SIZE] = new_cols(winn_ref)

    pq = past_len + _iota((dec_len, 1), 0)
    units = _make_units(q_ref[...], True)
    o_c, imps = _cmp_branch(units, pq, ckv, ncp)
    n_blk = (past_len + dec_len - 1) // SEL_BLOCK + 1
    sel = _select_blocks_rank(imps, pq, -(-n_blk // 8) * 8)
    _slc_init(m_scr, l_scr, acc_scr)
    _slc_tile(units, sel, pq, slc_scr[...], 0, tk, True, None, m_scr, l_scr, acc_scr)
    o_w = _win_branch(units, pq, win_scr[...], past_len - wb)
    _nsa_finish(units, o_c, o_w, dec_len, sm_ref[...], ng_ref, o_ref, l_scr, acc_scr)


def _nsa_sample(page_table, nq, sm, ynew, slc_new, win_new, win_cache_t, ng, y_pool, slc_pool_t, layer, n_pool,
                past_len, dec_len):
    db, n_pages = page_table.shape
    wb = win_cache_t.shape[2]
    tk = past_len + PAGE_SIZE
    wk = wb + PAGE_SIZE
    rows = NSA_REP * dec_len
    row = lambda b, pt: (b, 0)

    def page_map(k):
        return lambda b, pt: (layer * n_pool + pt[b, k], 0, 0)

    in_specs = [pl.BlockSpec((dec_len, 512), row),
                pl.BlockSpec((dec_len, LANE), row),
                pl.BlockSpec((1, 8, 2 * NSA_KV_W), lambda b, pt: (b, 0, 0)),
                pl.BlockSpec((dec_len, NSA_KV_W), row),
                pl.BlockSpec((dec_len, NSA_KV_W), row),
                pl.BlockSpec((1, NSA_KV_W, wb), lambda b, pt: (layer * db + b, 0, 0)),
                pl.BlockSpec((1, NSA_HD), lambda b, pt: (0, 0))]
    in_specs += [pl.BlockSpec((1, SUB_PP, 2 * NSA_KV_W), page_map(k)) for k in range(n_pages)]
    in_specs += [pl.BlockSpec((1, NSA_KV_W, PAGE_SIZE), page_map(k)) for k in range(n_pages)]
    grid_spec = pltpu.PrefetchScalarGridSpec(
        num_scalar_prefetch=1,
        grid=(db,),
        in_specs=in_specs,
        out_specs=pl.BlockSpec((dec_len, 512), row),
        scratch_shapes=[pltpu.VMEM((n_pages * SUB_PP + 8, 2 * NSA_KV_W), F32),
                        pltpu.VMEM((NSA_KV_W, tk), BF16),
                        pltpu.VMEM((NSA_KV_W, wk), BF16),
                        pltpu.VMEM((NSA_KV_HEADS, rows, LANE), F32),
                        pltpu.VMEM((NSA_KV_HEADS, rows, LANE), F32),
                        pltpu.VMEM((NSA_KV_HEADS, rows, NSA_HD), F32)])
    return pl.pallas_call(
        functools.partial(_nsa_sample_kernel, n_pages=n_pages, past_len=past_len, dec_len=dec_len),
        grid_spec=grid_spec,
        out_shape=jax.ShapeDtypeStruct((db * dec_len, 512), F32),
        compiler_params=_params(("arbitrary",)),
    )(page_table, nq, sm, ynew, slc_new, win_new, win_cache_t, ng,
      *([y_pool] * n_pages), *([slc_pool_t] * n_pages))


def _reorder_proj(w):
    sizes = (GDN_QKV, GDN_HEADS, GDN_HEADS, GDN_HEADS * GDN_DV, NSA_HEADS * NSA_HD,
             NSA_KV_W, NSA_KV_W, NSA_KV_W, 3 * NSA_HEADS)
    off = [0] + [int(v) for v in np.cumsum(sizes)]
    seg = lambda i: w[:, :, off[i]:off[i + 1]].astype(BF16)
    pad = jnp.zeros(w.shape[:2] + (N_PROJ - off[-1],), BF16)
    return jnp.concatenate([seg(0), seg(3), seg(4), seg(5), seg(6), seg(7), seg(1), seg(2), seg(8), pad], axis=2)


def _cmp_weights(cmp_w):
    def half(w):
        rows = []
        for c in range(2):
            for g in range(NSA_KV_HEADS):
                j = c * NSA_KV_HEADS + g
                rows.append(jnp.pad(w[:, :, c], ((0, 0), (0, 0), (0, 0), (j * NSA_HD, NSA_KV_W - (j + 1) * NSA_HD))))
        return jnp.concatenate(rows, axis=2).astype(BF16)

    return half(cmp_w[:, :CMP_STRIDE]), half(cmp_w[:, CMP_STRIDE:])


def _cmp_pe(cmp_pe):
    nl = cmp_pe.shape[0]

    def half(p):
        return jnp.broadcast_to(p[:, :, :, None, :], (nl, CMP_STRIDE, 2, NSA_KV_HEADS, NSA_HD)).reshape(nl, SUB_W)

    return jnp.stack([half(cmp_pe[:, :CMP_STRIDE]), half(cmp_pe[:, CMP_STRIDE:])], axis=1)


def _small_t(sm, c):
    m = sm.shape[0]
    return jnp.transpose(sm[:, :32].reshape(m // c, c, 32), (0, 2, 1))


def _feature_major(a):
    nd = a.ndim
    perm = tuple(range(nd - 4)) + (nd - 3, nd - 2, nd - 1, nd - 4)
    t = jnp.transpose(a, perm)
    return t.reshape(t.shape[:nd - 4] + (NSA_KV_W, t.shape[-1]))


def _row_major_kv(a_t):
    lead = a_t.shape[:-2]
    n = len(lead)
    t = a_t.reshape(lead + (2, NSA_KV_HEADS, NSA_HD, a_t.shape[-1]))
    return jnp.transpose(t, tuple(range(n)) + (n + 3, n, n + 1, n + 2))


def kernel(x_prompt, x_sample, cache_cmp_kv, cache_slc_kv, page_table, cache_win_kv, state_gdn, state_conv,
           norm_mix, w_in, conv_w, gdn_a_log, gdn_dt_bias, gdn_norm, nsa_cmp_w, nsa_cmp_pe, nsa_norm,
           w_out, norm_ffn, w_ffn_in, w_ffn_out, norm_final):
    bp, tp, _ = x_prompt.shape
    db, ts, _ = x_sample.shape
    n_pool = cache_cmp_kv.shape[1]
    n_pages = page_table.shape[1]
    past_len = n_pages * PAGE_SIZE
    wb = cache_win_kv.shape[2]
    kv_row = (2, NSA_KV_HEADS, NSA_HD)
    wkeep = min(WINDOW, tp)
    assert GDN_CHUNK % ts == 0 and (db * ts) % GDN_CHUNK == 0 and tp % SLC_TILE == 0
    assert tp >= WIN_TILES * Q_BLOCK and n_pool % POOL_PAGES == 0

    w_in_b = _reorder_proj(w_in)
    w_t = jnp.transpose(jnp.concatenate([w_in_b[:, :, C_KV:C_SMALL], w_in_b[:, :, C_NQ:C_KV]], axis=2),
                        (0, 2, 1))
    w_out_b = w_out.astype(BF16)
    w_ffn_in_b = w_ffn_in.astype(BF16)
    w_ffn_out_b = w_ffn_out.astype(BF16)
    cw = jnp.pad(conv_w, ((0, 0), (0, 8 - CONV_W), (0, 0)))
    alog_b = jnp.broadcast_to(jnp.pad(gdn_a_log, ((0, 0), (0, 8 - GDN_HEADS)))[:, :, None], (DEPTH, 8, LANE))
    dtb_b = jnp.broadcast_to(jnp.pad(gdn_dt_bias, ((0, 0), (0, 8 - GDN_HEADS)))[:, :, None], (DEPTH, 8, LANE))
    wlo4, whi4 = _cmp_weights(nsa_cmp_w)
    wlo = wlo4.reshape(DEPTH, SUB_W, NSA_KV_W)
    whi = whi4.reshape(DEPTH, SUB_W, NSA_KV_W)
    pe2 = _cmp_pe(nsa_cmp_pe)

    y_pool = _compress_pool(_feature_major(cache_cmp_kv), pe2.reshape(DEPTH, 2, CMP_STRIDE, NSA_KV_W),
                            wlo4, whi4, POOL_PAGES)
    y_pool = y_pool.reshape(DEPTH * n_pool, SUB_PP, 2 * NSA_KV_W)
    slc_pool_t = _feature_major(cache_slc_kv).reshape(DEPTH * n_pool, NSA_KV_W, PAGE_SIZE)
    win_cache_t = _feature_major(cache_win_kv)
    state_all = state_gdn.reshape(DEPTH * db, GDN_HEADS, GDN_DK, GDN_DV)
    seq_per_chunk = GDN_CHUNK // ts

    xp = x_prompt.reshape(bp * tp, D_MODEL)
    xs = x_sample.reshape(db * ts, D_MODEL)
    zero_hist = jnp.zeros((bp, 8, GDN_QKV), F32)
    zero_state = jnp.zeros((bp, GDN_HEADS, GDN_DK, GDN_DV), F32)
    outs = [[] for _ in range(10)]
    for l in range(DEPTH):
        g_mix = norm_mix[l][None, :]
        g_ffn = norm_ffn[l][None, :]
        gdn_g = gdn_norm[l][None, :]
        nsa_g = nsa_norm[l][None, :]

        (qkv, gate, _, ncmp, _, _, sm, cmp_t, slc_t, win_t, nq_t, sk, sv, wk, wv) = _proj(
            xp, g_mix, w_in_b[l], w_t[l], SLC_TILE, bp)
        smt = _small_t(sm, GDN_CHUNK)
        qkvn = _gdn_prep(qkv.reshape(bp, tp, GDN_QKV), zero_hist, cw[l], 256).reshape(bp * tp, GDN_QKV)
        o_gdn, s_new = _gdn(qkvn, gate, sm, alog_b[l], dtb_b[l], gdn_g, zero_state, 0,
                            bp, tp // (GDN_CHUNK * GDN_NCH), GDN_CHUNK, GDN_NCH)
        y = _compress(ncmp.reshape(1, bp * tp // CMP_STRIDE, SUB_W), pe2[l:l + 1], wlo[l:l + 1],
                      whi[l:l + 1], 512).reshape(bp, tp // CMP_STRIDE, 2 * NSA_KV_W)
        o_nsa = _nsa_prompt(nq_t, smt, y, sk, sv, wk, wv, jnp.broadcast_to(nsa_norm[l][:, None], (NSA_HD, Q_BLOCK)),
                            bp, tp)
        xp = _out_proj(xp, o_gdn, o_nsa, w_out_b[l], 512)
        xp = _ffn(xp, g_ffn, w_ffn_in_b[l], w_ffn_out_b[l], 1024)
        outs[0].append(cmp_t)
        outs[2].append(slc_t)
        outs[4].append(win_t[:, :, tp - wkeep:])
        outs[6].append(s_new)
        outs[8].append(qkv.reshape(bp, tp, GDN_QKV)[:, -(CONV_W - 1):])

        qkv, gate, nq, ncmp, nslc, nwin, sm = _proj(xs, g_mix, w_in_b[l], None, 512)
        hist = jnp.pad(state_conv[l], ((0, 0), (8 - (CONV_W - 1), 0), (0, 0)))
        qkvn = _gdn_prep(qkv.reshape(db, ts, GDN_QKV), hist, cw[l], ts).reshape(db * ts, GDN_QKV)
        o_gdn, s_new = _gdn(qkvn, gate, sm, alog_b[l], dtb_b[l], gdn_g, state_all,
                            l * (db // seq_per_chunk), db // seq_per_chunk, 1, ts, 1)
        new_sub = jnp.pad(ncmp.reshape(db, 1, ts * NSA_KV_W), ((0, 0), (0, 7), (0, SUB_W - ts * NSA_KV_W)))
        ynew = _compress(new_sub.reshape(1, db * 8, SUB_W), pe2[l:l + 1], wlo[l:l + 1], whi[l:l + 1],
                         512).reshape(db, 8, 2 * NSA_KV_W)
        o_nsa = _nsa_sample(page_table, nq, sm, ynew, nslc, nwin, win_cache_t.reshape(DEPTH * db, NSA_KV_W, wb),
                            nsa_g, y_pool, slc_pool_t, l, n_pool, past_len, ts)
        xs = _out_proj(xs, o_gdn, o_nsa, w_out_b[l], 512)
        xs = _ffn(xs, g_ffn, w_ffn_in_b[l], w_ffn_out_b[l], 1024)
        outs[1].append(ncmp.reshape((db, ts) + kv_row))
        outs[3].append(nslc.reshape((db, ts) + kv_row))
        outs[5].append(jnp.transpose(nwin.reshape(db, ts, NSA_KV_W), (0, 2, 1)))
        outs[7].append(s_new)
        outs[9].append(jnp.concatenate([state_conv[l], qkv.reshape(db, ts, GDN_QKV)], axis=1)[:, -(CONV_W - 1):])

    y_prompt = _final_norm(xp, norm_final[None, :], 512).reshape(bp, tp, D_MODEL)
    y_sample = _final_norm(xs, norm_final[None, :], 512).reshape(db, ts, D_MODEL)
    res = [jnp.stack(o) for o in outs]
    res[5] = jnp.concatenate([win_cache_t[:, :, :, ts:], res[5]], axis=3)
    for i in (0, 2, 4, 5):
        res[i] = _row_major_kv(res[i])
    return (y_prompt, y_sample) + tuple(res)
```

```python
import functools

import numpy as np
import jax
import jax.numpy as jnp
from jax import lax
from jax.experimental import pallas as pl
from jax.experimental.pallas import tpu as pltpu

F32 = jnp.float32
BF16 = jnp.bfloat16

D_MODEL = 1024
DEPTH = 4
PAGE_SIZE = 128
GDN_HEADS = 4
GDN_DK = 128
GDN_DV = 128
GDN_QKV = GDN_HEADS * (2 * GDN_DK + GDN_DV)
CONV_W = 4
NSA_HEADS = 8
NSA_KV_HEADS = 2
NSA_HD = 64
NSA_REP = NSA_HEADS // NSA_KV_HEADS
CMP_LEN = 32
CMP_STRIDE = 16
SEL_BLOCK = 64
N_SEL = 16
WINDOW = 512
NSA_KV_W = 2 * NSA_KV_HEADS * NSA_HD
D_FF = (8 * D_MODEL + 3 * 256 - 1) // (3 * 256) * 256
NEG_INF = -1e30
FORCE_BONUS = 1e4
EPS = 1e-6

LANE = 128
GDN_CHUNK = 128
GDN_NCH = 2
INV_BLOCK = 16
Q_BLOCK = 128
SLC_TILE = 512
WIN_TILES = WINDOW // Q_BLOCK + 1
SUB_W = CMP_STRIDE * NSA_KV_W
SUB_PP = PAGE_SIZE // CMP_STRIDE
POOL_SEQS = 2
VMEM_LIMIT = 56 * 1024 * 1024

C_QKV, C_GATE, C_NQ, C_KV, C_SMALL = 0, 1536, 2048, 2560, 3328
N_PROJ = 3456
SM_B, SM_A, SM_G = 0, 4, 8

_NT = (((1,), (1,)), ((), ()))


def _params(sem):
    return pltpu.CompilerParams(dimension_semantics=sem, vmem_limit_bytes=VMEM_LIMIT)


def _dot(a, b):
    return jnp.dot(a.astype(BF16), b.astype(BF16), preferred_element_type=F32)


def _dot_nt(a, b):
    return lax.dot_general(a.astype(BF16), b.astype(BF16), _NT, preferred_element_type=F32)


def _split2(x):
    hi = x.astype(BF16)
    lo = (x - hi.astype(F32)).astype(BF16)
    return hi, lo


def _split3(x):
    hi = x.astype(BF16)
    r = x - hi.astype(F32)
    mid = r.astype(BF16)
    lo = (r - mid.astype(F32)).astype(BF16)
    return hi, mid, lo


def _dotx(a, b):
    ah, al = _split2(a)
    bh, bl = _split2(b)
    d = functools.partial(jnp.dot, preferred_element_type=F32)
    return d(jnp.concatenate([ah, al], axis=1), jnp.concatenate([bh, bh], axis=0)) + d(ah, bl)


def _dot01_l(m01, x):
    m = m01.astype(BF16)
    hi, mid, lo = _split3(x)
    d = functools.partial(jnp.dot, preferred_element_type=F32)
    return d(jnp.concatenate([m, m], axis=1), jnp.concatenate([hi, mid], axis=0)) + d(m, lo)


def _dot01_r(x, m01):
    m = m01.astype(BF16)
    hi, mid, lo = _split3(x)
    d = functools.partial(jnp.dot, preferred_element_type=F32)
    return d(hi, m) + (d(mid, m) + d(lo, m))


def _iota(shape, dim):
    return lax.broadcasted_iota(jnp.int32, shape, dim)


def _idiv(x, d):
    sh = int(d).bit_length() - 1
    assert (1 << sh) == d
    return lax.shift_right_logical(x, jnp.int32(sh))


def _rep(x, rep):
    return x if rep == 1 else jnp.concatenate([x] * rep, axis=0)


def _rms_rows(x, g):
    return x * lax.rsqrt(jnp.mean(x * x, axis=-1, keepdims=True) + EPS) * g


def _proj_kernel(x_ref, g_ref, w_ref, *rest, with_t):
    if with_t:
        wt_ref = rest[0]
        rest = rest[1:]
    qkv_ref, gate_ref, nq_ref, cmp_ref, slc_ref, win_ref, sm_ref = rest[:7]
    h = _rms_rows(x_ref[...], g_ref[...]).astype(BF16)
    d = functools.partial(jnp.dot, preferred_element_type=F32)
    qkv_ref[...] = d(h, w_ref[:, C_QKV:C_GATE])
    gate_ref[...] = d(h, w_ref[:, C_GATE:C_NQ])
    nq_ref[...] = d(h, w_ref[:, C_NQ:C_KV])
    kv = d(h, w_ref[:, C_KV:C_SMALL])
    cmp_ref[...] = kv[:, 0:NSA_KV_W]
    slc_ref[...] = kv[:, NSA_KV_W:2 * NSA_KV_W]
    win_ref[...] = kv[:, 2 * NSA_KV_W:3 * NSA_KV_W]
    sm_ref[...] = d(h, w_ref[:, C_SMALL:N_PROJ])
    if with_t:
        cmpt_ref, slct_ref, wint_ref, nqt_ref, sk_ref, sv_ref, wk_ref, wv_ref = rest[7:]
        kvt = lax.dot_general(wt_ref[...], h, _NT, preferred_element_type=F32)
        cmpt_ref[0] = kvt[0:NSA_KV_W]
        slct = kvt[NSA_KV_W:2 * NSA_KV_W]
        wint = kvt[2 * NSA_KV_W:3 * NSA_KV_W]
        slct_ref[0] = slct
        wint_ref[0] = wint
        nqt_ref[0] = kvt[3 * NSA_KV_W:3 * NSA_KV_W + 512]
        half = NSA_KV_W // 2
        sk_ref[0, 0] = kv[:, NSA_KV_W:NSA_KV_W + half].astype(BF16)
        sv_ref[0, 0] = slct[half:].astype(BF16)
        for j in range(wk_ref.shape[1]):
            rows = slice(j * Q_BLOCK, (j + 1) * Q_BLOCK)
            wk_ref[0, j] = kv[rows, 2 * NSA_KV_W:2 * NSA_KV_W + half].astype(BF16)
            wv_ref[0, j] = wint[half:, rows].astype(BF16)


def _proj(x, g, w, wt, tm, batch=None):
    m = x.shape[0]
    tm = min(tm, m)
    with_t = wt is not None
    widths = (GDN_QKV, 512, 512, NSA_KV_W, NSA_KV_W, NSA_KV_W, LANE)
    in_specs = [pl.BlockSpec((tm, D_MODEL), lambda i: (i, 0)),
                pl.BlockSpec((1, D_MODEL), lambda i: (0, 0)),
                pl.BlockSpec((D_MODEL, N_PROJ), lambda i: (0, 0))]
    out_specs = [pl.BlockSpec((tm, wd), lambda i: (i, 0)) for wd in widths]
    out_shape = [jax.ShapeDtypeStruct((m, wd), F32) for wd in widths]
    args = [x, g, w]
    if with_t:
        t_len = m // batch
        nt = t_len // tm
        assert tm == SLC_TILE and t_len % tm == 0
        in_specs.append(pl.BlockSpec((wt.shape[0], D_MODEL), lambda i: (0, 0)))
        args.append(wt)
        tmap = lambda i: (i // nt, 0, i % nt)
        tile = lambda i: (i // nt, i % nt, 0, 0)
        half = NSA_KV_W // 2
        nwt = tm // Q_BLOCK
        out_specs += [pl.BlockSpec((1, NSA_KV_W, tm), tmap)] * 3 + [pl.BlockSpec((1, 512, tm), tmap)]
        out_shape += [jax.ShapeDtypeStruct((batch, NSA_KV_W, t_len), F32)] * 3
        out_shape += [jax.ShapeDtypeStruct((batch, 512, t_len), F32)]
        out_specs += [pl.BlockSpec((1, 1, tm, half), tile),
                      pl.BlockSpec((1, 1, half, tm), tile),
                      pl.BlockSpec((1, nwt, Q_BLOCK, half), tile),
                      pl.BlockSpec((1, nwt, half, Q_BLOCK), tile)]
        out_shape += [jax.ShapeDtypeStruct((batch, nt, tm, half), BF16),
                      jax.ShapeDtypeStruct((batch, nt, half, tm), BF16),
                      jax.ShapeDtypeStruct((batch, t_len // Q_BLOCK, Q_BLOCK, half), BF16),
                      jax.ShapeDtypeStruct((batch, t_len // Q_BLOCK, half, Q_BLOCK), BF16)]
    return pl.pallas_call(
        functools.partial(_proj_kernel, with_t=with_t),
        grid=(m // tm,),
        in_specs=in_specs,
        out_specs=out_specs,
        out_shape=out_shape,
        compiler_params=_params(("parallel",)),
    )(*args)


def _out_kernel(x_ref, a1_ref, a2_ref, w_ref, o_ref):
    half = w_ref.shape[0] // 2
    o_ref[...] = x_ref[...] + (_dot(a1_ref[...], w_ref[:half, :]) + _dot(a2_ref[...], w_ref[half:, :]))


def _out_proj(x, a1, a2, w, tm):
    m = x.shape[0]
    tm = min(tm, m)
    return pl.pallas_call(
        _out_kernel,
        grid=(m // tm,),
        in_specs=[pl.BlockSpec((tm, D_MODEL), lambda i: (i, 0)),
                  pl.BlockSpec((tm, 512), lambda i: (i, 0)),
                  pl.BlockSpec((tm, 512), lambda i: (i, 0)),
                  pl.BlockSpec((D_MODEL, D_MODEL), lambda i: (0, 0))],
        out_specs=pl.BlockSpec((tm, D_MODEL), lambda i: (i, 0)),
        out_shape=jax.ShapeDtypeStruct((m, D_MODEL), F32),
        compiler_params=_params(("parallel",)),
    )(x, a1, a2, w)


def _ffn_kernel(x_ref, g_ref, wg_ref, wu_ref, wo_ref, o_ref, h_scr, acc_scr):
    f = pl.program_id(1)

    @pl.when(f == 0)
    def _():
        h_scr[...] = _rms_rows(x_ref[...], g_ref[...]).astype(BF16)
        acc_scr[...] = jnp.zeros_like(acc_scr)

    h = h_scr[...]
    gt = jnp.dot(h, wg_ref[...], preferred_element_type=F32)
    up = jnp.dot(h, wu_ref[...], preferred_element_type=F32)
    act = (gt * jax.nn.sigmoid(gt)) * up
    acc_scr[...] += jnp.dot(act.astype(BF16), wo_ref[...], preferred_element_type=F32)

    @pl.when(f == pl.num_programs(1) - 1)
    def _():
        o_ref[...] = x_ref[...] + acc_scr[...]


def _ffn(x, g, w_in, w_out, tm):
    m = x.shape[0]
    tm = min(tm, m)
    nf = 2
    tf = D_FF // nf
    return pl.pallas_call(
        _ffn_kernel,
        grid=(m // tm, nf),
        in_specs=[pl.BlockSpec((tm, D_MODEL), lambda i, f: (i, 0)),
                  pl.BlockSpec((1, D_MODEL), lambda i, f: (0, 0)),
                  pl.BlockSpec((D_MODEL, tf), lambda i, f: (0, f)),
                  pl.BlockSpec((D_MODEL, tf), lambda i, f: (0, nf + f)),
                  pl.BlockSpec((tf, D_MODEL), lambda i, f: (f, 0))],
        out_specs=pl.BlockSpec((tm, D_MODEL), lambda i, f: (i, 0)),
        out_shape=jax.ShapeDtypeStruct((m, D_MODEL), F32),
        scratch_shapes=[pltpu.VMEM((tm, D_MODEL), BF16), pltpu.VMEM((tm, D_MODEL), F32)],
        compiler_params=_params(("parallel", "arbitrary")),
    )(x, g, w_in, w_in, w_out)


def _window_rows_kernel(c_ref, n_ref, o_ref):
    wb = c_ref.shape[-1]
    ts = n_ref.shape[-1]
    for j in range(c_ref.shape[0]):
        o_ref[j] = pltpu.roll(c_ref[j], wb - ts, 1)
        o_ref[j, :, wb - ts:wb] = n_ref[j]


def _window_rows(cache_t, new_t):
    n, f, wb = cache_t.shape
    ts = new_t.shape[-1]
    assert wb >= ts
    nb = 4 if n % 4 == 0 else 1
    return pl.pallas_call(
        _window_rows_kernel,
        grid=(n // nb,),
        in_specs=[pl.BlockSpec((nb, f, wb), lambda i: (i, 0, 0)),
                  pl.BlockSpec((nb, f, ts), lambda i: (i, 0, 0))],
        out_specs=pl.BlockSpec((nb, f, wb), lambda i: (i, 0, 0)),
        out_shape=jax.ShapeDtypeStruct((n, f, wb), F32),
        compiler_params=_params(("parallel",)),
    )(cache_t, new_t)


def _final_norm_kernel(x_ref, g_ref, o_ref):
    o_ref[...] = _rms_rows(x_ref[...], g_ref[...])


def _final_norm(x, g, tm):
    m = x.shape[0]
    tm = min(tm, m)
    return pl.pallas_call(
        _final_norm_kernel,
        grid=(m // tm,),
        in_specs=[pl.BlockSpec((tm, D_MODEL), lambda i: (i, 0)),
                  pl.BlockSpec((1, D_MODEL), lambda i: (0, 0))],
        out_specs=pl.BlockSpec((tm, D_MODEL), lambda i: (i, 0)),
        out_shape=jax.ShapeDtypeStruct((m, D_MODEL), F32),
        compiler_params=_params(("parallel",)),
    )(x, g)


def _prep_kernel(raw_ref, hist_ref, cw_ref, o_ref, ext_scr, *, tc):
    @pl.when(pl.program_id(1) == 0)
    def _():
        ext_scr[0:8, :] = hist_ref[0]

    ext_scr[8:8 + tc, :] = raw_ref[0]
    for s in range(GDN_QKV // LANE):
        o_ref[0, :, s * LANE:(s + 1) * LANE] = _conv_slab(ext_scr, cw_ref, tc, s)
    tail = ext_scr[tc:tc + 8, :]
    ext_scr[0:8, :] = tail


def _gdn_prep(raw, hist, cw, tc):
    b, t, _ = raw.shape
    return pl.pallas_call(
        functools.partial(_prep_kernel, tc=tc),
        grid=(b, t // tc),
        in_specs=[pl.BlockSpec((1, tc, GDN_QKV), lambda i, j: (i, j, 0)),
                  pl.BlockSpec((1, 8, GDN_QKV), lambda i, j: (i, 0, 0)),
                  pl.BlockSpec((8, GDN_QKV), lambda i, j: (0, 0))],
        out_specs=pl.BlockSpec((1, tc, GDN_QKV), lambda i, j: (i, j, 0)),
        out_shape=jax.ShapeDtypeStruct((b, t, GDN_QKV), F32),
        scratch_shapes=[pltpu.VMEM((tc + 8, GDN_QKV), F32)],
        compiler_params=_params(("parallel", "arbitrary")),
    )(raw, hist, cw)


def _tri_inv(mats, ri, ci):
    c = mats[0].shape[0]
    eye = (ri == ci).astype(F32)
    bd = _idiv(ri, INV_BLOCK) == _idiv(ci, INV_BLOCK)
    ad = [jnp.where(bd, a, 0.0) for a in mats]
    ao = [a - d for a, d in zip(mats, ad)]
    pw = ad
    td = [eye - d for d in ad]
    k = 2
    while k < INV_BLOCK:
        pw = [_dotx(p, p) for p in pw]
        td = [_dotx(t, eye + p) for t, p in zip(td, pw)]
        k *= 2
    n = [_dotx(t, o) for t, o in zip(td, ao)]
    tn = [eye - x for x in n]
    pw = n
    k = 2
    while k < c // INV_BLOCK:
        pw = [_dotx(p, p) for p in pw]
        tn = [_dotx(t, eye + p) for t, p in zip(tn, pw)]
        k *= 2
    return [_dotx(a, b) for a, b in zip(tn, td)]


def _softplus(x):
    return jnp.maximum(x, 0.0) + jnp.log1p(jnp.exp(-jnp.abs(x)))


def _conv_slab(ext_ref, cw_ref, rows, s):
    cols = slice(s * LANE, (s + 1) * LANE)
    y = jnp.zeros((rows, LANE), F32)
    for j in range(CONV_W):
        y = y + ext_ref[pl.ds(8 - (CONV_W - 1) + j, rows), cols] * cw_ref[j:j + 1, cols]
    y = y * jax.nn.sigmoid(y)
    if s < 2 * GDN_HEADS:
        y = y * lax.rsqrt(jnp.sum(y * y, axis=-1, keepdims=True) + EPS)
        if s < GDN_HEADS:
            y = y * (GDN_DK ** -0.5)
    return y


def _gdn_kernel(qkv_ref, gate_ref, sm_ref, alog_ref, dtb_ref, ng_ref, s0_ref, *rest, ls, nch, fused):
    c = GDN_CHUNK
    nseq = c // ls
    assert nseq == 1 or nch == 1
    heads = range(GDN_HEADS)
    pairs = [(ch, h) for ch in range(nch) for h in heads]
    if fused:
        hist_ref, cw_ref, o_ref, sout_ref, ext_scr = rest
    else:
        o_ref, sout_ref = rest

    @pl.when(pl.program_id(1) == 0)
    def _():
        sout_ref[...] = s0_ref[...]
        if fused:
            ext_scr[0:8, :] = hist_ref[0]

    if fused:
        rows = nch * c
        ext_scr[8:8 + rows, :] = qkv_ref[...]
        prepped = [_conv_slab(ext_scr, cw_ref, rows, s) for s in range(GDN_QKV // LANE)]
        tail = ext_scr[rows:rows + 8, :]
        ext_scr[0:8, :] = tail

    ri = _iota((c, c), 0)
    ci = _iota((c, c), 1)
    same = _idiv(ri, ls) == _idiv(ci, ls)
    lower = same & (ci <= ri)
    strict = same & (ci < ri)
    lo_m = lower.astype(F32)
    same_m = same.astype(F32)
    sm = [sm_ref[ch * c:(ch + 1) * c, :] for ch in range(nch)]

    def slab(ch, s):
        if fused:
            return prepped[s][ch * c:(ch + 1) * c]
        return qkv_ref[ch * c:(ch + 1) * c, s * LANE:(s + 1) * LANE]

    q = [slab(ch, h) for ch, h in pairs]
    k = [slab(ch, GDN_HEADS + h) for ch, h in pairs]
    v = [slab(ch, 2 * GDN_HEADS + h) for ch, h in pairs]
    neg_a = [-jnp.exp(alog_ref[h:h + 1, :]) for h in heads]
    dtb = [dtb_ref[h:h + 1, :] for h in heads]
    beta = [jax.nn.sigmoid(jnp.broadcast_to(sm[ch][:, SM_B + h:SM_B + h + 1], (c, LANE))) for ch, h in pairs]
    g_col = [neg_a[h] * _softplus(jnp.broadcast_to(sm[ch][:, SM_A + h:SM_A + h + 1], (c, LANE)) + dtb[h])
             for ch, h in pairs]
    n = range(len(pairs))
    dcy = [_dot01_l(lo_m, g_col[i]) for i in n]
    dcy_row = [x.T for x in dcy]
    if nseq == 1:
        dtot = [jnp.broadcast_to(x[c - 1:c, :], (c, LANE)) for x in dcy]
    else:
        dtot = [_dot01_l(same_m, g_col[i]) for i in n]
    dm = [jnp.where(lower, jnp.exp(jnp.where(lower, dcy[i] - dcy_row[i], 0.0)), 0.0) for i in n]
    kb = [k[i] * beta[i] for i in n]
    kk = [_dot_nt(jnp.concatenate([kb[i], q[i]], axis=0), k[i]) for i in n]
    a_mat = [jnp.where(strict, kk[i][0:c] * dm[i], 0.0) for i in n]
    attn = [jnp.where(lower, kk[i][c:2 * c] * dm[i], 0.0) for i in n]
    edc = [jnp.exp(dcy[i]) for i in n]
    t_inv = _tri_inv(a_mat, ri, ci)
    uw = [_dotx(t_inv[i], jnp.concatenate([v[i] * beta[i], kb[i] * edc[i]], axis=1)) for i in n]
    u = [x[:, 0:LANE] for x in uw]
    w = [x[:, LANE:2 * LANE] for x in uw]
    qd = [q[i] * edc[i] for i in n]
    kdt = [(k[i] * jnp.exp(dtot[i] - dcy[i])).T for i in n]
    gl = [jnp.exp(dtot[i]) for i in n]
    o_all = []
    if nseq == 1:
        s_cur = [sout_ref[0, h] for h in heads]
        for ch in range(nch):
            ix = [ch * GDN_HEADS + h for h in heads]
            ws = [_dot(jnp.concatenate([w[i], qd[i]], axis=0), s_cur[h]) for h, i in zip(heads, ix)]
            v_new = [u[i] - ws[h][0:c] for h, i in zip(heads, ix)]
            o_all += [ws[h][c:2 * c] + _dot(attn[i], v_new[h]) for h, i in zip(heads, ix)]
            s_cur = [s_cur[h] * gl[i][0:1, :] + _dot(kdt[i], v_new[h]) for h, i in zip(heads, ix)]
        for h in heads:
            sout_ref[0, h] = s_cur[h]
    else:
        for h in heads:
            vn, oq = [], []
            for s in range(nseq):
                s_old = sout_ref[s, h]
                rows = slice(s * ls, (s + 1) * ls)
                vn.append(u[h][rows] - _dot(w[h][rows], s_old))
                oq.append(_dot(qd[h][rows], s_old))
            v_new = jnp.concatenate(vn, axis=0)
            o_all.append(jnp.concatenate(oq, axis=0) + _dot(attn[h], v_new))
            for s in range(nseq):
                kdt_s = jnp.where(_idiv(ci, ls) == s, kdt[h], 0.0)
                sout_ref[s, h] = sout_ref[s, h] * gl[h][s * ls:s * ls + 1, :] + _dot(kdt_s, v_new)
    for i, (ch, h) in enumerate(pairs):
        o = o_all[i]
        o = o * lax.rsqrt(jnp.mean(o * o, axis=-1, keepdims=True) + EPS) * ng_ref[...]
        gt = gate_ref[ch * c:(ch + 1) * c, h * LANE:(h + 1) * LANE]
        o_ref[ch * c:(ch + 1) * c, h * LANE:(h + 1) * LANE] = o * (gt * jax.nn.sigmoid(gt))


def _gdn(qkv, gate, sm, alog_b, dtb_b, ng, s0, s0_off, nb, nt, ls, nch, prep=None):
    m = qkv.shape[0]
    c = GDN_CHUNK * nch
    nseq = GDN_CHUNK // ls
    row = lambda i, j: (i * nt + j, 0)
    n_state = nb * nseq
    in_specs = [pl.BlockSpec((c, GDN_QKV), row),
                pl.BlockSpec((c, 512), row),
                pl.BlockSpec((c, LANE), row),
                pl.BlockSpec((8, LANE), lambda i, j: (0, 0)),
                pl.BlockSpec((8, LANE), lambda i, j: (0, 0)),
                pl.BlockSpec((1, LANE), lambda i, j: (0, 0)),
                pl.BlockSpec((nseq, GDN_HEADS, GDN_DK, GDN_DV), lambda i, j: (s0_off + i, 0, 0, 0))]
    args = [qkv, gate, sm, alog_b, dtb_b, ng, s0]
    scratch = []
    if prep is not None:
        in_specs += [pl.BlockSpec((1, 8, GDN_QKV), lambda i, j: (i, 0, 0)),
                     pl.BlockSpec((8, GDN_QKV), lambda i, j: (0, 0))]
        args += list(prep)
        scratch = [pltpu.VMEM((c + 8, GDN_QKV), F32)]
    return pl.pallas_call(
        functools.partial(_gdn_kernel, ls=ls, nch=nch, fused=prep is not None),
        grid=(nb, nt),
        in_specs=in_specs,
        out_specs=[pl.BlockSpec((c, 512), row),
                   pl.BlockSpec((nseq, GDN_HEADS, GDN_DK, GDN_DV), lambda i, j: (i, 0, 0, 0))],
        out_shape=[jax.ShapeDtypeStruct((m, 512), F32),
                   jax.ShapeDtypeStruct((n_state, GDN_HEADS, GDN_DK, GDN_DV), F32)],
        scratch_shapes=scratch,
        compiler_params=_params(("parallel", "arbitrary")),
    )(*args)


def _cmp_kernel(x_ref, pe_ref, wlo_ref, whi_ref, y_ref):
    x = x_ref[0]
    y_ref[0, :, 0:NSA_KV_W] = _dot(x + pe_ref[0, 0:1, :], wlo_ref[0])
    y_ref[0, :, NSA_KV_W:2 * NSA_KV_W] = _dot(x + pe_ref[0, 1:2, :], whi_ref[0])


def _compress(x, pe, wlo, whi, tr):
    nl, r, _ = x.shape
    tr = min(tr, r)
    return pl.pallas_call(
        _cmp_kernel,
        grid=(nl, r // tr),
        in_specs=[pl.BlockSpec((1, tr, SUB_W), lambda l, i: (l, i, 0)),
                  pl.BlockSpec((1, 2, SUB_W), lambda l, i: (l, 0, 0)),
                  pl.BlockSpec((1, SUB_W, NSA_KV_W), lambda l, i: (l, 0, 0)),
                  pl.BlockSpec((1, SUB_W, NSA_KV_W), lambda l, i: (l, 0, 0))],
        out_specs=pl.BlockSpec((1, tr, 2 * NSA_KV_W), lambda l, i: (l, i, 0)),
        out_shape=jax.ShapeDtypeStruct((nl, r, 2 * NSA_KV_W), F32),
        compiler_params=_params(("parallel", "parallel")),
    )(x, pe, wlo, whi)


def _cmp_pool_kernel(pt_ref, pe_ref, wlo_ref, whi_ref, *rest, npg):
    del pt_ref
    pages = rest[:npg]
    y_ref, x_scr = rest[npg:]
    for j in range(npg):
        x = pages[j][0].T
        for hf in range(2):
            x_scr[hf, j * PAGE_SIZE:(j + 1) * PAGE_SIZE, :] = x[:, hf * LANE:(hf + 1) * LANE]
    n = npg * SUB_PP
    for hf in range(2):
        cols = slice(hf * LANE, (hf + 1) * LANE)
        ylo = jnp.zeros((n, LANE), F32)
        yhi = jnp.zeros((n, LANE), F32)
        for l in range(CMP_STRIDE):
            xl = x_scr[hf, pl.ds(l, n, stride=CMP_STRIDE), :]
            ylo = ylo + _dot(xl + pe_ref[0, 0, l:l + 1, cols], wlo_ref[0, l, cols, cols])
            yhi = yhi + _dot(xl + pe_ref[0, 1, l:l + 1, cols], whi_ref[0, l, cols, cols])
        y_ref[0, :, hf * LANE:(hf + 1) * LANE] = ylo
        y_ref[0, :, NSA_KV_W + hf * LANE:NSA_KV_W + (hf + 1) * LANE] = yhi


def _compress_pool(page_table, pages_t, pe, wlo, whi, nl, n_pool, seq_per_step):
    db, n_pages = page_table.shape
    assert db % seq_per_step == 0
    npg = seq_per_step * n_pages
    n = npg * SUB_PP

    def page_map(k):
        s, p = divmod(k, n_pages)
        return lambda l, i, pt: (l * n_pool + pt[i * seq_per_step + s, p], 0, 0)

    in_specs = [pl.BlockSpec((1, 2, CMP_STRIDE, NSA_KV_W), lambda l, i, pt: (l, 0, 0, 0)),
                pl.BlockSpec((1, CMP_STRIDE, NSA_KV_W, NSA_KV_W), lambda l, i, pt: (l, 0, 0, 0)),
                pl.BlockSpec((1, CMP_STRIDE, NSA_KV_W, NSA_KV_W), lambda l, i, pt: (l, 0, 0, 0))]
    in_specs += [pl.BlockSpec((1, NSA_KV_W, PAGE_SIZE), page_map(k)) for k in range(npg)]
    grid_spec = pltpu.PrefetchScalarGridSpec(
        num_scalar_prefetch=1,
        grid=(nl, db // seq_per_step),
        in_specs=in_specs,
        out_specs=pl.BlockSpec((1, n, 2 * NSA_KV_W), lambda l, i, pt: (l, i, 0)),
        scratch_shapes=[pltpu.VMEM((2, npg * PAGE_SIZE, LANE), F32)])
    return pl.pallas_call(
        functools.partial(_cmp_pool_kernel, npg=npg),
        grid_spec=grid_spec,
        out_shape=jax.ShapeDtypeStruct((nl, db * n_pages * SUB_PP, 2 * NSA_KV_W), F32),
        compiler_params=_params(("parallel", "parallel")),
    )(page_table, pe, wlo, whi, *([pages_t] * npg))


def _slope(h):
    return 2.0 ** (-8.0 * (h + 1) / NSA_HEADS)


def _make_units(q, stacked):
    qb = q.shape[0]
    units = []
    for g in range(NSA_KV_HEADS):
        hs = range(g * NSA_REP, (g + 1) * NSA_REP)
        slabs = [q[:, h * NSA_HD:(h + 1) * NSA_HD] * (NSA_HD ** -0.5) for h in hs]
        if stacked:
            rows = jnp.concatenate(slabs, axis=0).astype(BF16)
            slope = jnp.concatenate([jnp.full((qb, 1), _slope(h), F32) for h in hs], axis=0)
            units.append([(rows, slope, NSA_REP)])
        else:
            units.append([(s.astype(BF16), _slope(h), 1) for s, h in zip(slabs, hs)])
    return units


def _cmp_branch(units, pq, ckv, ncp):
    qb = pq.shape[0]
    n_ix = _iota((1, ncp), 1)
    c_end = n_ix * CMP_STRIDE + (CMP_LEN - 1)
    c_ctr = (n_ix * CMP_STRIDE).astype(F32) + 0.5 * (CMP_LEN - 1)
    pool = (_idiv(_iota((ncp, LANE), 0), SEL_BLOCK // CMP_STRIDE) == _iota((ncp, LANE), 1)).astype(F32)
    ckv_b = ckv.astype(BF16)
    geo = {}
    for g in range(NSA_KV_HEADS):
        for _, _, rep in units[g]:
            if rep not in geo:
                pqr = _rep(pq, rep)
                geo[rep] = (c_end <= pqr, pqr.astype(F32) - c_ctr)
    flat = [(g, rows, slope, rep) for g in range(NSA_KV_HEADS) for rows, slope, rep in units[g]]
    kc = [ckv_b[:, g * NSA_HD:(g + 1) * NSA_HD] for g in range(NSA_KV_HEADS)]
    vc = [ckv_b[:, (NSA_KV_HEADS + g) * NSA_HD:(NSA_KV_HEADS + g + 1) * NSA_HD] for g in range(NSA_KV_HEADS)]
    s = [_dot_nt(rows, kc[g]) for g, rows, _, _ in flat]
    s = [jnp.where(geo[rep][0], si - slope * geo[rep][1], NEG_INF) for si, (_, _, slope, rep) in zip(s, flat)]
    m = [jnp.max(si, axis=-1, keepdims=True) for si in s]
    e = [jnp.where(geo[rep][0], jnp.exp(si - mi), 0.0) for si, mi, (_, _, _, rep) in zip(s, m, flat)]
    den = [jnp.sum(ei, axis=-1, keepdims=True) for ei in e]
    p = [ei / jnp.where(di > 0.0, di, 1.0) for ei, di in zip(e, den)]
    o = [_dot(pi, vc[g]) for pi, (g, _, _, _) in zip(p, flat)]
    outs, imps = [], []
    for g in range(NSA_KV_HEADS):
        psum = jnp.zeros((qb, ncp), F32)
        og = []
        for pi, oi, (gi, _, _, rep) in zip(p, o, flat):
            if gi == g:
                og.append(oi)
                for r in range(rep):
                    psum = psum + pi[r * qb:(r + 1) * qb]
        outs.append(og)
        imps.append(psum)
    imps = [_dot01_r(ps, pool) for ps in imps]
    return outs, imps


def _select_blocks(imps, pq, ax):
    blk = _iota(imps[0].shape, ax)
    cur = _idiv(pq, SEL_BLOCK)
    forced = (blk == 0) | (blk == cur) | (blk == cur - 1)
    bonus = jnp.where(forced, FORCE_BONUS, 0.0)
    work = [jnp.where(blk <= cur, imp + bonus, NEG_INF) for imp in imps]
    idx = blk.astype(F32)
    sel = [jnp.zeros(w.shape, F32) for w in work]
    for _ in range(N_SEL):
        m = [jnp.max(w, axis=ax, keepdims=True) for w in work]
        first = [jnp.min(jnp.where(w == mi, idx, 2.0 * LANE), axis=ax, keepdims=True) for w, mi in zip(work, m)]
        hit = [idx == f for f in first]
        sel = [jnp.where(h, 1.0, s) for h, s in zip(hit, sel)]
        work = [jnp.where(h, -jnp.inf, w) for h, w in zip(hit, work)]
    return sel


def _select_blocks_rank(imps, pq, nb):
    qb = imps[0].shape[0]
    blk = _iota((qb, LANE), 1)
    cur = _idiv(pq, SEL_BLOCK)
    forced = (blk == 0) | (blk == cur) | (blk == cur - 1)
    bonus = jnp.where(forced, FORCE_BONUS, 0.0)
    score = [jnp.where(blk <= cur, imp + bonus, NEG_INF) for imp in imps]
    fill = jnp.full((LANE - len(imps) * qb, LANE), NEG_INF, F32)
    st = jnp.concatenate(score + [fill], axis=0).T[0:nb]
    jrow = _iota((nb, LANE), 0)
    cnt = jnp.zeros((nb, LANE), F32)
    for j in range(nb):
        row = st[j:j + 1, :]
        cnt = cnt + jnp.where(jrow > j, jnp.where(row >= st, 1.0, 0.0), jnp.where(row > st, 1.0, 0.0))
    sel_t = jnp.where(cnt < N_SEL, 1.0, 0.0)
    sel = jnp.concatenate([sel_t, jnp.zeros((LANE - nb, LANE), F32)], axis=0).T
    return [sel[g * qb:(g + 1) * qb] for g in range(len(imps))]


def _slc_tile(units, sel, pq, kvt, k0, tk, causal, flags, m_scr, l_scr, acc_scr):
    key = k0 + _iota((1, tk), 1)
    dist_i = pq - key
    dist = dist_i.astype(F32)
    e_tile = (_iota((LANE, tk), 0) == _idiv(k0 + _iota((LANE, tk), 1), SEL_BLOCK)).astype(BF16)
    ui = 0
    for g in range(NSA_KV_HEADS):
        def group(g=g, ui=ui):
            on = jnp.dot(sel[g].astype(BF16), e_tile, preferred_element_type=F32) > 0.5
            if causal:
                on = on & (dist_i >= 0)
            bias = jnp.where(on, 0.0, NEG_INF)
            kt = kvt[g * NSA_HD:(g + 1) * NSA_HD, :]
            vt = kvt[(NSA_KV_HEADS + g) * NSA_HD:(NSA_KV_HEADS + g + 1) * NSA_HD, :]
            us = units[g]
            ids = [ui + j for j in range(len(us))]
            s = [jnp.dot(rows, kt, preferred_element_type=F32) for rows, _, _ in us]
            s = [si + (_rep(bias, rep) - slope * _rep(dist, rep)) for si, (_, slope, rep) in zip(s, us)]
            m_prev = [m_scr[u][:, 0:1] for u in ids]
            m_new = [jnp.maximum(mp, jnp.max(si, axis=-1, keepdims=True)) for mp, si in zip(m_prev, s)]
            alpha = [jnp.exp(mp - mn) for mp, mn in zip(m_prev, m_new)]
            p = [jnp.exp(si - mn) for si, mn in zip(s, m_new)]
            l_new = [a * l_scr[u][:, 0:1] + jnp.sum(pi, axis=-1, keepdims=True) for a, u, pi in zip(alpha, ids, p)]
            pv = [lax.dot_general(pi.astype(BF16), vt, _NT, preferred_element_type=F32) for pi in p]
            for u, a, pvi, mn, ln in zip(ids, alpha, pv, m_new, l_new):
                acc_scr[u] = a * acc_scr[u] + pvi
                m_scr[u] = jnp.broadcast_to(mn, m_scr.shape[1:])
                l_scr[u] = jnp.broadcast_to(ln, l_scr.shape[1:])

        if flags is None:
            group()
        else:
            pl.when(flags[g])(group)
        ui += len(units[g])


def _win_branch(units, pq, kvt, w0):
    wk = kvt.shape[1]
    dist_i = pq - (w0 + _iota((1, wk), 1))
    dist = dist_i.astype(F32)
    bias = jnp.where((dist_i >= 0) & (dist_i < WINDOW), 0.0, NEG_INF)
    flat = [(g, rows, slope, rep) for g in range(NSA_KV_HEADS) for rows, slope, rep in units[g]]
    kt = [kvt[g * NSA_HD:(g + 1) * NSA_HD, :] for g in range(NSA_KV_HEADS)]
    vt = [kvt[(NSA_KV_HEADS + g) * NSA_HD:(NSA_KV_HEADS + g + 1) * NSA_HD, :] for g in range(NSA_KV_HEADS)]
    s = [jnp.dot(rows, kt[g], preferred_element_type=F32) for g, rows, _, _ in flat]
    s = [si + (_rep(bias, rep) - slope * _rep(dist, rep)) for si, (_, _, slope, rep) in zip(s, flat)]
    m = [jnp.max(si, axis=-1, keepdims=True) for si in s]
    p = [jnp.exp(si - mi) for si, mi in zip(s, m)]
    den = [jnp.sum(pi, axis=-1, keepdims=True) for pi in p]
    pv = [lax.dot_general(pi.astype(BF16), vt[g], _NT, preferred_element_type=F32) for pi, (g, _, _, _) in zip(p, flat)]
    o = [pvi / di for pvi, di in zip(pv, den)]
    return [[oi for oi, (gi, _, _, _) in zip(o, flat) if gi == g] for g in range(NSA_KV_HEADS)]


def _nsa_finish(units, o_c, o_w, qb, sm, ng_ref, o_ref, l_scr, acc_scr):
    gates = jax.nn.sigmoid(sm)
    per_group = len(units[0])
    for h in range(NSA_HEADS):
        g, r = divmod(h, NSA_REP)
        if per_group == 1:
            rows = slice(r * qb, (r + 1) * qb)
            u, j = g, 0
        else:
            rows = slice(0, qb)
            u, j = h, r
        oc = o_c[g][j][rows]
        ow = o_w[g][j][rows]
        o_s = acc_scr[u][rows] / l_scr[u][rows][:, 0:1]
        c0 = SM_G + 3 * h
        o = gates[:, c0:c0 + 1] * oc + gates[:, c0 + 1:c0 + 2] * o_s + gates[:, c0 + 2:c0 + 3] * ow
        o = o * lax.rsqrt(jnp.mean(o * o, axis=-1, keepdims=True) + EPS) * ng_ref[...]
        o_ref[:, h * NSA_HD:(h + 1) * NSA_HD] = o


def _slc_init(m_scr, l_scr, acc_scr):
    m_scr[...] = jnp.full(m_scr.shape, NEG_INF, F32)
    l_scr[...] = jnp.zeros(l_scr.shape, F32)
    acc_scr[...] = jnp.zeros(acc_scr.shape, F32)


def _key_aug(t_col, lane, qblk):
    rel = (_idiv(t_col, SEL_BLOCK) - qblk).astype(F32)
    off = (t_col & (SEL_BLOCK - 1)).astype(F32)
    return jnp.where(lane == 0, rel, jnp.where(lane == 1, off, 0.0)).astype(BF16)


def _nsa_prompt_kernel(qt_ref, smt_ref, y_ref, sk_ref, sv_ref, wk_ref, wv_ref, ngb_ref, o_ref,
                       kc_scr, vct_scr, selb_scr, m_scr, l_scr, acc_scr, *, t_len):
    i = pl.program_id(1)
    nsub = t_len // CMP_STRIDE
    gw = NSA_REP * Q_BLOCK
    half = NSA_KV_W // 2
    groups = range(NSA_KV_HEADS)

    @pl.when(i == 0)
    def _():
        y = y_ref[0]
        ckv = y[:, 0:NSA_KV_W] + pltpu.roll(y[:, NSA_KV_W:2 * NSA_KV_W], nsub - 1, 0)
        kc_scr[...] = ckv[:, 0:half].astype(BF16)
        vct_scr[...] = ckv[:, half:].T.astype(BF16)

    pos0 = i * Q_BLOCK
    qblk = pos0 // SEL_BLOCK
    pq_row = pos0 + _iota((1, Q_BLOCK), 1)
    pq_g = jnp.concatenate([pq_row] * NSA_REP, axis=1)
    qt = qt_ref[0] * (NSA_HD ** -0.5)
    zeros = jnp.zeros((NSA_HD, gw), F32)
    slope_g, rq, rfull = [], [], []
    aug_row = _iota((LANE, gw), 0)
    for g in groups:
        hs = range(g * NSA_REP, (g + 1) * NSA_REP)
        qg = jnp.concatenate([qt[h * NSA_HD:(h + 1) * NSA_HD, :] for h in hs], axis=1)
        rq.append(jnp.concatenate([qg, zeros] if g == 0 else [zeros, qg], axis=0).astype(BF16))
        sl = jnp.concatenate([jnp.full((1, Q_BLOCK), _slope(h), F32) for h in hs], axis=1)
        slope_g.append(sl)
        raug =jnp.where(aug_row == 0, SEL_BLOCK * sl, jnp.where(aug_row == 1, sl, 0.0)).astype(BF16)
        rfull.append(jnp.concatenate([rq[g], raug], axis=0))

    n_col = _iota((nsub, gw), 0)
    ok = (n_col * CMP_STRIDE + (CMP_LEN - 1)) <= pq_g
    dist = pq_g.astype(F32) - ((n_col * CMP_STRIDE).astype(F32) + 0.5 * (CMP_LEN - 1))
    kc = kc_scr[...]
    s = [jnp.dot(kc, rq[g], preferred_element_type=F32) for g in groups]
    s = [jnp.where(ok, s[g] - slope_g[g] * dist, NEG_INF) for g in groups]
    m = [jnp.max(x, axis=0, keepdims=True) for x in s]
    e = [jnp.where(ok, jnp.exp(s[g] - m[g]), 0.0) for g in groups]
    den = [jnp.sum(x, axis=0, keepdims=True) for x in e]
    p = [e[g] / jnp.where(den[g] > 0.0, den[g], 1.0) for g in groups]
    o_c = [jnp.dot(vct_scr[g * NSA_HD:(g + 1) * NSA_HD, :], p[g].astype(BF16), preferred_element_type=F32)
           for g in groups]
    psum = [sum(p[g][:, r * Q_BLOCK:(r + 1) * Q_BLOCK] for r in range(NSA_REP)) for g in groups]
    pool_t = (_iota((LANE, nsub), 0) == _idiv(_iota((LANE, nsub), 1), SEL_BLOCK // CMP_STRIDE)).astype(F32)
    imp_t = [_dot01_l(pool_t, psum[g]) for g in groups]

    sel = _select_blocks(imp_t, pq_row, 0)
    for g in groups:
        selb_scr[g] = jnp.where(sel[g] > 0.5, 0.0, NEG_INF)
    used = [jnp.max(x, axis=1, keepdims=True) for x in sel]
    tile_of_blk = _idiv(_iota((LANE, 1), 0), SLC_TILE // SEL_BLOCK)

    m_scr[...] = jnp.full(m_scr.shape, NEG_INF, F32)
    l_scr[...] = jnp.zeros(l_scr.shape, F32)
    acc_scr[...] = jnp.zeros(acc_scr.shape, F32)
    n_tiles = (pos0 + Q_BLOCK + SLC_TILE - 1) // SLC_TILE
    blk_per_tile = SLC_TILE // SEL_BLOCK

    def slc_tile(kt, causal, flags):
        k0 = kt * SLC_TILE
        t_col = k0 + _iota((SLC_TILE, LANE), 0)
        lhs = jnp.concatenate([sk_ref[0, kt], _key_aug(t_col, _iota((SLC_TILE, LANE), 1), qblk)], axis=1)
        for g in groups:
            def group(g=g):
                bias = jnp.concatenate(
                    [jnp.broadcast_to(selb_scr[g, pl.ds(kt * blk_per_tile + j, 1), :], (SEL_BLOCK, Q_BLOCK))
                     for j in range(blk_per_tile)], axis=0)
                if causal:
                    bias = jnp.where(t_col <= pq_row, bias, NEG_INF)
                sc = jnp.dot(lhs, rfull[g], preferred_element_type=F32) + jnp.concatenate([bias] * NSA_REP, axis=1)
                m_prev = m_scr[g, 0:1, :]
                m_new = jnp.maximum(m_prev, jnp.max(sc, axis=0, keepdims=True))
                alpha = jnp.exp(m_prev - m_new)
                pr = jnp.exp(sc - m_new)
                l_scr[g, 0:1, :] = alpha * l_scr[g, 0:1, :] + jnp.sum(pr, axis=0, keepdims=True)
                pv = jnp.dot(sv_ref[0, kt, g * NSA_HD:(g + 1) * NSA_HD, :], pr.astype(BF16),
                             preferred_element_type=F32)
                acc_scr[g] = alpha * acc_scr[g] + pv
                m_scr[g, 0:1, :] = m_new

            if flags is None:
                group()
            else:
                pl.when(flags[g])(group)

    def body(kt, carry):
        flags = [jnp.max(jnp.where(tile_of_blk == kt, u, 0.0)) > 0.5 for u in used]
        slc_tile(kt, False, flags)
        return carry

    lax.fori_loop(0, n_tiles - 1, body, 0)
    slc_tile(n_tiles - 1, True, None)

    wt0 = jnp.maximum(i - (WIN_TILES - 1), 0)
    wkeys = WIN_TILES * Q_BLOCK
    kw = jnp.concatenate([wk_ref[0, wt0 + j] for j in range(WIN_TILES)], axis=0)
    vw = jnp.concatenate([wv_ref[0, wt0 + j] for j in range(WIN_TILES)], axis=1)
    tw = wt0 * Q_BLOCK + _iota((wkeys, LANE), 0)
    lhs_w = jnp.concatenate([kw, _key_aug(tw, _iota((wkeys, LANE), 1), qblk)], axis=1)
    dist_w = pq_row - tw
    bias_w = jnp.where((dist_w >= 0) & (dist_w < WINDOW), 0.0, NEG_INF)
    bias_w = jnp.concatenate([bias_w] * NSA_REP, axis=1)
    sw = [jnp.dot(lhs_w, rfull[g], preferred_element_type=F32) + bias_w for g in groups]
    mw = [jnp.max(x, axis=0, keepdims=True) for x in sw]
    pw = [jnp.exp(sw[g] - mw[g]) for g in groups]
    dw = [jnp.sum(x, axis=0, keepdims=True) for x in pw]
    o_w = [jnp.dot(vw[g * NSA_HD:(g + 1) * NSA_HD, :], pw[g].astype(BF16), preferred_element_type=F32) / dw[g]
           for g in groups]

    gates = jax.nn.sigmoid(smt_ref[0])
    outs = []
    for h in range(NSA_HEADS):
        g, r = divmod(h, NSA_REP)
        lanes = slice(r * Q_BLOCK, (r + 1) * Q_BLOCK)
        o_s = acc_scr[g][:, lanes] / l_scr[g, 0:1, lanes]
        c0 = SM_G + 3 * h
        o = gates[c0:c0 + 1] * o_c[g][:, lanes] + gates[c0 + 1:c0 + 2] * o_s + gates[c0 + 2:c0 + 3] * o_w[g][:, lanes]
        outs.append(o * lax.rsqrt(jnp.mean(o * o, axis=0, keepdims=True) + EPS) * ngb_ref[...])
    o_ref[...] = jnp.concatenate(outs, axis=0).T


def _nsa_prompt(nq_t, smt, y, sk, sv, wk, wv, ngb, b, t_len):
    nqb = t_len // Q_BLOCK
    nsub = t_len // CMP_STRIDE
    nst = t_len // SLC_TILE
    half = NSA_KV_W // 2
    gw = NSA_REP * Q_BLOCK
    whole = lambda bi, i: (bi, 0, 0, 0)
    return pl.pallas_call(
        functools.partial(_nsa_prompt_kernel, t_len=t_len),
        grid=(b, nqb),
        in_specs=[pl.BlockSpec((1, 512, Q_BLOCK), lambda bi, i: (bi, 0, i)),
                  pl.BlockSpec((1, 32, Q_BLOCK), lambda bi, i: (bi * nqb + i, 0, 0)),
                  pl.BlockSpec((1, nsub, 2 * NSA_KV_W), lambda bi, i: (bi, 0, 0)),
                  pl.BlockSpec((1, nst, SLC_TILE, half), whole),
                  pl.BlockSpec((1, nst, half, SLC_TILE), whole),
                  pl.BlockSpec((1, nqb, Q_BLOCK, half), whole),
                  pl.BlockSpec((1, nqb, half, Q_BLOCK), whole),
                  pl.BlockSpec((NSA_HD, Q_BLOCK), lambda bi, i: (0, 0))],
        out_specs=pl.BlockSpec((Q_BLOCK, 512), lambda bi, i: (bi * nqb + i, 0)),
        out_shape=jax.ShapeDtypeStruct((b * t_len, 512), F32),
        scratch_shapes=[pltpu.VMEM((nsub, half), BF16),
                        pltpu.VMEM((half, nsub), BF16),
                        pltpu.VMEM((NSA_KV_HEADS, LANE, Q_BLOCK), F32),
                        pltpu.VMEM((NSA_KV_HEADS, 8, gw), F32),
                        pltpu.VMEM((NSA_KV_HEADS, 8, gw), F32),
                        pltpu.VMEM((NSA_KV_HEADS, NSA_HD, gw), F32)],
        compiler_params=_params(("parallel", "arbitrary")),
    )(nq_t, smt, y, sk, sv, wk, wv, ngb)


def _nsa_sample_kernel(pt_ref, q_ref, sm_ref, ypast_ref, ynew_ref, slcn_ref, winn_ref, winc_ref, ng_ref, *rest,
                       n_pages, past_len, dec_len):
    del pt_ref
    s_pages = rest[0:n_pages]
    o_ref = rest[n_pages]
    slc_scr, win_scr, m_scr, l_scr, acc_scr = rest[n_pages + 1:]
    ncp = n_pages * SUB_PP
    tk = slc_scr.shape[1]
    wb = winc_ref.shape[2]

    y = jnp.concatenate([ypast_ref[0], ynew_ref[0]], axis=0)
    ckv = (y[:, 0:NSA_KV_W] + pltpu.roll(y[:, NSA_KV_W:2 * NSA_KV_W], ncp + 7, 0))[0:ncp]

    def new_cols(ref):
        pad = jnp.zeros((PAGE_SIZE - dec_len, NSA_KV_W), F32)
        return jnp.concatenate([ref[...], pad], axis=0).T.astype(BF16)

    for p in range(n_pages):
        slc_scr[:, p * PAGE_SIZE:(p + 1) * PAGE_SIZE] = s_pages[p][0].astype(BF16)
    slc_scr[:, past_len:tk] = new_cols(slcn_ref)
    win_scr[:, 0:wb] = winc_ref[0].astype(BF16)
    win_scr[:, wb:wb + PAGE_SIZE] = new_cols(winn_ref)

    pq = past_len + _iota((dec_len, 1), 0)
    units = _make_units(q_ref[...], True)
    o_c, imps = _cmp_branch(units, pq, ckv, ncp)
    n_blk = (past_len + dec_len - 1) // SEL_BLOCK + 1
    sel = _select_blocks_rank(imps, pq, -(-n_blk // 8) * 8)
    _slc_init(m_scr, l_scr, acc_scr)
    _slc_tile(units, sel, pq, slc_scr[...], 0, tk, True, None, m_scr, l_scr, acc_scr)
    o_w = _win_branch(units, pq, win_scr[...], past_len - wb)
    _nsa_finish(units, o_c, o_w, dec_len, sm_ref[...], ng_ref, o_ref, l_scr, acc_scr)


def _nsa_sample(page_table, nq, sm, ynew, slc_new, win_new, win_cache_t, ng, y_past, slc_pool_t, layer, n_pool,
                past_len, dec_len):
    db, n_pages = page_table.shape
    wb = win_cache_t.shape[2]
    tk = past_len + PAGE_SIZE
    wk = wb + PAGE_SIZE
    rows = NSA_REP * dec_len
    row = lambda b, pt: (b, 0)

    def page_map(k):
        return lambda b, pt: (layer * n_pool + pt[b, k], 0, 0)

    in_specs = [pl.BlockSpec((dec_len, 512), row),
                pl.BlockSpec((dec_len, LANE), row),
                pl.BlockSpec((1, n_pages * SUB_PP, 2 * NSA_KV_W), lambda b, pt: (layer * db + b, 0, 0)),
                pl.BlockSpec((1, 8, 2 * NSA_KV_W), lambda b, pt: (b, 0, 0)),
                pl.BlockSpec((dec_len, NSA_KV_W), row),
                pl.BlockSpec((dec_len, NSA_KV_W), row),
                pl.BlockSpec((1, NSA_KV_W, wb), lambda b, pt: (layer * db + b, 0, 0)),
                pl.BlockSpec((1, NSA_HD), lambda b, pt: (0, 0))]
    in_specs += [pl.BlockSpec((1, NSA_KV_W, PAGE_SIZE), page_map(k)) for k in range(n_pages)]
    grid_spec = pltpu.PrefetchScalarGridSpec(
        num_scalar_prefetch=1,
        grid=(db,),
        in_specs=in_specs,
        out_specs=pl.BlockSpec((dec_len, 512), row),
        scratch_shapes=[pltpu.VMEM((NSA_KV_W, tk), BF16),
                        pltpu.VMEM((NSA_KV_W, wk), BF16),
                        pltpu.VMEM((NSA_KV_HEADS, rows, LANE), F32),
                        pltpu.VMEM((NSA_KV_HEADS, rows, LANE), F32),
                        pltpu.VMEM((NSA_KV_HEADS, rows, NSA_HD), F32)])
    return pl.pallas_call(
        functools.partial(_nsa_sample_kernel, n_pages=n_pages, past_len=past_len, dec_len=dec_len),
        grid_spec=grid_spec,
        out_shape=jax.ShapeDtypeStruct((db * dec_len, 512), F32),
        compiler_params=_params(("arbitrary",)),
    )(page_table, nq, sm, y_past, ynew, slc_new, win_new, win_cache_t, ng, *([slc_pool_t] * n_pages))


def _reorder_proj(w):
    sizes = (GDN_QKV, GDN_HEADS, GDN_HEADS, GDN_HEADS * GDN_DV, NSA_HEADS * NSA_HD,
             NSA_KV_W, NSA_KV_W, NSA_KV_W, 3 * NSA_HEADS)
    off = [0] + [int(v) for v in np.cumsum(sizes)]
    seg = lambda i: w[:, :, off[i]:off[i + 1]].astype(BF16)
    pad = jnp.zeros(w.shape[:2] + (N_PROJ - off[-1],), BF16)
    return jnp.concatenate([seg(0), seg(3), seg(4), seg(5), seg(6), seg(7), seg(1), seg(2), seg(8), pad], axis=2)


def _cmp_weights(cmp_w):
    def half(w):
        rows = []
        for c in range(2):
            for g in range(NSA_KV_HEADS):
                j = c * NSA_KV_HEADS + g
                rows.append(jnp.pad(w[:, :, c], ((0, 0), (0, 0), (0, 0), (j * NSA_HD, NSA_KV_W - (j + 1) * NSA_HD))))
        return jnp.concatenate(rows, axis=2).astype(BF16)

    return half(cmp_w[:, :CMP_STRIDE]), half(cmp_w[:, CMP_STRIDE:])


def _cmp_pe(cmp_pe):
    nl = cmp_pe.shape[0]

    def half(p):
        return jnp.broadcast_to(p[:, :, :, None, :], (nl, CMP_STRIDE, 2, NSA_KV_HEADS, NSA_HD)).reshape(nl, SUB_W)

    return jnp.stack([half(cmp_pe[:, :CMP_STRIDE]), half(cmp_pe[:, CMP_STRIDE:])], axis=1)


def _small_t(sm, c):
    m = sm.shape[0]
    return jnp.transpose(sm[:, :32].reshape(m // c, c, 32), (0, 2, 1))


def _feature_major(a):
    nd = a.ndim
    perm = tuple(range(nd - 4)) + (nd - 3, nd - 2, nd - 1, nd - 4)
    t = jnp.transpose(a, perm)
    return t.reshape(t.shape[:nd - 4] + (NSA_KV_W, t.shape[-1]))


def _row_major_kv(a_t):
    lead = a_t.shape[:-2]
    n = len(lead)
    t = a_t.reshape(lead + (2, NSA_KV_HEADS, NSA_HD, a_t.shape[-1]))
    return jnp.transpose(t, tuple(range(n)) + (n + 3, n, n + 1, n + 2))


def kernel(x_prompt, x_sample, cache_cmp_kv, cache_slc_kv, page_table, cache_win_kv, state_gdn, state_conv,
           norm_mix, w_in, conv_w, gdn_a_log, gdn_dt_bias, gdn_norm, nsa_cmp_w, nsa_cmp_pe, nsa_norm,
           w_out, norm_ffn, w_ffn_in, w_ffn_out, norm_final):
    bp, tp, _ = x_prompt.shape
    db, ts, _ = x_sample.shape
    n_pool = cache_cmp_kv.shape[1]
    n_pages = page_table.shape[1]
    past_len = n_pages * PAGE_SIZE
    wb = cache_win_kv.shape[2]
    kv_row = (2, NSA_KV_HEADS, NSA_HD)
    wkeep = min(WINDOW, tp)
    assert GDN_CHUNK % ts == 0 and (db * ts) % GDN_CHUNK == 0 and tp % SLC_TILE == 0
    assert tp >= WIN_TILES * Q_BLOCK and db % POOL_SEQS == 0

    w_in_b = _reorder_proj(w_in)
    w_t = jnp.transpose(jnp.concatenate([w_in_b[:, :, C_KV:C_SMALL], w_in_b[:, :, C_NQ:C_KV]], axis=2),
                        (0, 2, 1))
    w_out_b = w_out.astype(BF16)
    w_ffn_in_b = w_ffn_in.astype(BF16)
    w_ffn_out_b = w_ffn_out.astype(BF16)
    cw = jnp.pad(conv_w, ((0, 0), (0, 8 - CONV_W), (0, 0)))
    alog_b = jnp.broadcast_to(jnp.pad(gdn_a_log, ((0, 0), (0, 8 - GDN_HEADS)))[:, :, None], (DEPTH, 8, LANE))
    dtb_b = jnp.broadcast_to(jnp.pad(gdn_dt_bias, ((0, 0), (0, 8 - GDN_HEADS)))[:, :, None], (DEPTH, 8, LANE))
    wlo4, whi4 = _cmp_weights(nsa_cmp_w)
    wlo = wlo4.reshape(DEPTH, SUB_W, NSA_KV_W)
    whi = whi4.reshape(DEPTH, SUB_W, NSA_KV_W)
    pe2 = _cmp_pe(nsa_cmp_pe)

    cmp_pool_t = _feature_major(cache_cmp_kv).reshape(DEPTH * n_pool, NSA_KV_W, PAGE_SIZE)
    y_pool = _compress_pool(page_table, cmp_pool_t, pe2.reshape(DEPTH, 2, CMP_STRIDE, NSA_KV_W), wlo4, whi4,
                            DEPTH, n_pool, POOL_SEQS)
    y_pool = y_pool.reshape(DEPTH * db, n_pages * SUB_PP, 2 * NSA_KV_W)
    slc_pool_t = _feature_major(cache_slc_kv).reshape(DEPTH * n_pool, NSA_KV_W, PAGE_SIZE)
    win_cache_t = _feature_major(cache_win_kv)
    state_all = state_gdn.reshape(DEPTH * db, GDN_HEADS, GDN_DK, GDN_DV)
    seq_per_chunk = GDN_CHUNK // ts

    xp = x_prompt.reshape(bp * tp, D_MODEL)
    xs = x_sample.reshape(db * ts, D_MODEL)
    zero_hist = jnp.zeros((bp, 8, GDN_QKV), F32)
    zero_state = jnp.zeros((bp, GDN_HEADS, GDN_DK, GDN_DV), F32)
    outs = [[] for _ in range(10)]
    for l in range(DEPTH):
        g_mix = norm_mix[l][None, :]
        g_ffn = norm_ffn[l][None, :]
        gdn_g = gdn_norm[l][None, :]
        nsa_g = nsa_norm[l][None, :]

        (qkv, gate, _, ncmp, _, _, sm, cmp_t, slc_t, win_t, nq_t, sk, sv, wk, wv) = _proj(
            xp, g_mix, w_in_b[l], w_t[l], SLC_TILE, bp)
        smt = _small_t(sm, GDN_CHUNK)
        o_gdn, s_new = _gdn(qkv, gate, sm, alog_b[l], dtb_b[l], gdn_g, zero_state, 0,
                            bp, tp // (GDN_CHUNK * GDN_NCH), GDN_CHUNK, GDN_NCH, prep=(zero_hist, cw[l]))
        y = _compress(ncmp.reshape(1, bp * tp // CMP_STRIDE, SUB_W), pe2[l:l + 1], wlo[l:l + 1],
                      whi[l:l + 1], 512).reshape(bp, tp // CMP_STRIDE, 2 * NSA_KV_W)
        o_nsa = _nsa_prompt(nq_t, smt, y, sk, sv, wk, wv, jnp.broadcast_to(nsa_norm[l][:, None], (NSA_HD, Q_BLOCK)),
                            bp, tp)
        xp = _out_proj(xp, o_gdn, o_nsa, w_out_b[l], 512)
        xp = _ffn(xp, g_ffn, w_ffn_in_b[l], w_ffn_out_b[l], 1024)
        outs[0].append(cmp_t)
        outs[2].append(slc_t)
        outs[4].append(win_t[:, :, tp - wkeep:])
        outs[6].append(s_new)
        outs[8].append(qkv.reshape(bp, tp, GDN_QKV)[:, -(CONV_W - 1):])

        qkv, gate, nq, ncmp, nslc, nwin, sm = _proj(xs, g_mix, w_in_b[l], None, 512)
        hist = jnp.pad(state_conv[l], ((0, 0), (8 - (CONV_W - 1), 0), (0, 0)))
        qkvn = _gdn_prep(qkv.reshape(db, ts, GDN_QKV), hist, cw[l], ts).reshape(db * ts, GDN_QKV)
        o_gdn, s_new = _gdn(qkvn, gate, sm, alog_b[l], dtb_b[l], gdn_g, state_all,
                            l * (db // seq_per_chunk), db // seq_per_chunk, 1, ts, 1)
        new_sub = jnp.pad(ncmp.reshape(db, 1, ts * NSA_KV_W), ((0, 0), (0, 7), (0, SUB_W - ts * NSA_KV_W)))
        ynew = _compress(new_sub.reshape(1, db * 8, SUB_W), pe2[l:l + 1], wlo[l:l + 1], whi[l:l + 1],
                         512).reshape(db, 8, 2 * NSA_KV_W)
        o_nsa = _nsa_sample(page_table, nq, sm, ynew, nslc, nwin, win_cache_t.reshape(DEPTH * db, NSA_KV_W, wb),
                            nsa_g, y_pool, slc_pool_t, l, n_pool, past_len, ts)
        xs = _out_proj(xs, o_gdn, o_nsa, w_out_b[l], 512)
        xs = _ffn(xs, g_ffn, w_ffn_in_b[l], w_ffn_out_b[l], 1024)
        outs[1].append(ncmp.reshape((db, ts) + kv_row))
        outs[3].append(nslc.reshape((db, ts) + kv_row))
        outs[5].append(jnp.transpose(nwin.reshape(db, ts, NSA_KV_W), (0, 2, 1)))
        outs[7].append(s_new)
        outs[9].append(jnp.concatenate([state_conv[l], qkv.reshape(db, ts, GDN_QKV)], axis=1)[:, -(CONV_W - 1):])

    y_prompt = _final_norm(xp, norm_final[None, :], 512).reshape(bp, tp, D_MODEL)
    y_sample = _final_norm(xs, norm_final[None, :], 512).reshape(db, ts, D_MODEL)
    res = [jnp.stack(o) for o in outs]
    res[5] = _window_rows(win_cache_t.reshape(DEPTH * db, NSA_KV_W, wb),
                          res[5].reshape(DEPTH * db, NSA_KV_W, ts)).reshape(DEPTH, db, NSA_KV_W, wb)
    for i in (0, 2, 4, 5):
        res[i] = _row_major_kv(res[i])
    return (y_prompt, y_sample) + tuple(res)
```

```python
import functools

import numpy as np
import jax
import jax.numpy as jnp
from jax import lax
from jax.experimental import pallas as pl
from jax.experimental.pallas import tpu as pltpu

F32 = jnp.float32
BF16 = jnp.bfloat16

D_MODEL = 1024
DEPTH = 4
PAGE_SIZE = 128
GDN_HEADS = 4
GDN_DK = 128
GDN_DV = 128
GDN_QKV = GDN_HEADS * (2 * GDN_DK + GDN_DV)
CONV_W = 4
NSA_HEADS = 8
NSA_KV_HEADS = 2
NSA_HD = 64
NSA_REP = NSA_HEADS // NSA_KV_HEADS
CMP_LEN = 32
CMP_STRIDE = 16
SEL_BLOCK = 64
N_SEL = 16
WINDOW = 512
NSA_KV_W = 2 * NSA_KV_HEADS * NSA_HD
D_FF = (8 * D_MODEL + 3 * 256 - 1) // (3 * 256) * 256
NEG_INF = -1e30
LOG2E = 1.4426950408889634
FORCE_BONUS = 1e4
EPS = 1e-6

LANE = 128
GDN_CHUNK = 128
GDN_NCH = 2
INV_BLOCK = 16
Q_BLOCK = 128
SLC_TILE = 512
WIN_TILES = WINDOW // Q_BLOCK + 1
SUB_W = CMP_STRIDE * NSA_KV_W
SUB_PP = PAGE_SIZE // CMP_STRIDE
POOL_SEQS = 2
VMEM_LIMIT = 56 * 1024 * 1024

C_QKV, C_GATE, C_NQ, C_KV, C_SMALL = 0, 1536, 2048, 2560, 3328
N_PROJ = 3456
SM_B, SM_A, SM_G = 0, 4, 8

_NT = (((1,), (1,)), ((), ()))


def _params(sem):
    return pltpu.CompilerParams(dimension_semantics=sem, vmem_limit_bytes=VMEM_LIMIT)


def _dot(a, b):
    return jnp.dot(a.astype(BF16), b.astype(BF16), preferred_element_type=F32)


def _dot_nt(a, b):
    return lax.dot_general(a.astype(BF16), b.astype(BF16), _NT, preferred_element_type=F32)


def _split2(x):
    hi = x.astype(BF16)
    lo = (x - hi.astype(F32)).astype(BF16)
    return hi, lo


def _split3(x):
    hi = x.astype(BF16)
    r = x - hi.astype(F32)
    mid = r.astype(BF16)
    lo = (r - mid.astype(F32)).astype(BF16)
    return hi, mid, lo


def _mm_split(a, b):
    ah, al = a
    bh, bl = b
    d = functools.partial(jnp.dot, preferred_element_type=F32)
    return d(jnp.concatenate([ah, al], axis=1), jnp.concatenate([bh, bh], axis=0)) + d(ah, bl)


def _dotx(a, b):
    return _mm_split(_split2(a), _split2(b))


def _dot01_l(m01, x):
    m = m01.astype(BF16)
    hi, mid, lo = _split3(x)
    d = functools.partial(jnp.dot, preferred_element_type=F32)
    return d(jnp.concatenate([m, m], axis=1), jnp.concatenate([hi, mid], axis=0)) + d(m, lo)


def _dot01_r(x, m01):
    m = m01.astype(BF16)
    hi, mid, lo = _split3(x)
    d = functools.partial(jnp.dot, preferred_element_type=F32)
    return d(hi, m) + (d(mid, m) + d(lo, m))


def _iota(shape, dim):
    return lax.broadcasted_iota(jnp.int32, shape, dim)


def _idiv(x, d):
    sh = int(d).bit_length() - 1
    assert (1 << sh) == d
    return lax.shift_right_logical(x, jnp.int32(sh))


def _rep(x, rep):
    return x if rep == 1 else jnp.concatenate([x] * rep, axis=0)


def _rms_rows(x, g):
    return x * lax.rsqrt(jnp.mean(x * x, axis=-1, keepdims=True) + EPS) * g


def _proj_kernel(x_ref, g_ref, w_ref, *rest, with_t):
    if with_t:
        wt_ref = rest[0]
        rest = rest[1:]
    qkv_ref, gate_ref, nq_ref, cmp_ref, slc_ref, win_ref, sm_ref = rest[:7]
    h = _rms_rows(x_ref[...], g_ref[...]).astype(BF16)
    d = functools.partial(jnp.dot, preferred_element_type=F32)
    qkv_ref[...] = d(h, w_ref[:, C_QKV:C_GATE])
    gate_ref[...] = d(h, w_ref[:, C_GATE:C_NQ])
    nq_ref[...] = d(h, w_ref[:, C_NQ:C_KV])
    kv = d(h, w_ref[:, C_KV:C_SMALL])
    cmp_ref[...] = kv[:, 0:NSA_KV_W]
    slc_ref[...] = kv[:, NSA_KV_W:2 * NSA_KV_W]
    win_ref[...] = kv[:, 2 * NSA_KV_W:3 * NSA_KV_W]
    sm_ref[...] = d(h, w_ref[:, C_SMALL:N_PROJ])
    if with_t:
        cmpt_ref, slct_ref, wint_ref, nqt_ref, sk_ref, sv_ref, wk_ref, wv_ref = rest[7:]
        kvt = lax.dot_general(wt_ref[...], h, _NT, preferred_element_type=F32)
        cmpt_ref[0] = kvt[0:NSA_KV_W]
        slct = kvt[NSA_KV_W:2 * NSA_KV_W]
        wint = kvt[2 * NSA_KV_W:3 * NSA_KV_W]
        slct_ref[0] = slct
        wint_ref[0] = wint
        nqt_ref[0] = kvt[3 * NSA_KV_W:3 * NSA_KV_W + 512]
        half = NSA_KV_W // 2
        sk_ref[0, 0] = kv[:, NSA_KV_W:NSA_KV_W + half].astype(BF16)
        sv_ref[0, 0] = slct[half:].astype(BF16)
        for j in range(wk_ref.shape[1]):
            rows = slice(j * Q_BLOCK, (j + 1) * Q_BLOCK)
            wk_ref[0, j] = kv[rows, 2 * NSA_KV_W:2 * NSA_KV_W + half].astype(BF16)
            wv_ref[0, j] = wint[half:, rows].astype(BF16)


def _proj(x, g, w, wt, tm, batch=None):
    m = x.shape[0]
    tm = min(tm, m)
    with_t = wt is not None
    widths = (GDN_QKV, 512, 512, NSA_KV_W, NSA_KV_W, NSA_KV_W, LANE)
    in_specs = [pl.BlockSpec((tm, D_MODEL), lambda i: (i, 0)),
                pl.BlockSpec((1, D_MODEL), lambda i: (0, 0)),
                pl.BlockSpec((D_MODEL, N_PROJ), lambda i: (0, 0))]
    out_specs = [pl.BlockSpec((tm, wd), lambda i: (i, 0)) for wd in widths]
    out_shape = [jax.ShapeDtypeStruct((m, wd), F32) for wd in widths]
    args = [x, g, w]
    if with_t:
        t_len = m // batch
        nt = t_len // tm
        assert tm == SLC_TILE and t_len % tm == 0
        in_specs.append(pl.BlockSpec((wt.shape[0], D_MODEL), lambda i: (0, 0)))
        args.append(wt)
        tmap = lambda i: (i // nt, 0, i % nt)
        tile = lambda i: (i // nt, i % nt, 0, 0)
        half = NSA_KV_W // 2
        nwt = tm // Q_BLOCK
        out_specs += [pl.BlockSpec((1, NSA_KV_W, tm), tmap)] * 3 + [pl.BlockSpec((1, 512, tm), tmap)]
        out_shape += [jax.ShapeDtypeStruct((batch, NSA_KV_W, t_len), F32)] * 3
        out_shape += [jax.ShapeDtypeStruct((batch, 512, t_len), F32)]
        out_specs += [pl.BlockSpec((1, 1, tm, half), tile),
                      pl.BlockSpec((1, 1, half, tm), tile),
                      pl.BlockSpec((1, nwt, Q_BLOCK, half), tile),
                      pl.BlockSpec((1, nwt, half, Q_BLOCK), tile)]
        out_shape += [jax.ShapeDtypeStruct((batch, nt, tm, half), BF16),
                      jax.ShapeDtypeStruct((batch, nt, half, tm), BF16),
                      jax.ShapeDtypeStruct((batch, t_len // Q_BLOCK, Q_BLOCK, half), BF16),
                      jax.ShapeDtypeStruct((batch, t_len // Q_BLOCK, half, Q_BLOCK), BF16)]
    return pl.pallas_call(
        functools.partial(_proj_kernel, with_t=with_t),
        grid=(m // tm,),
        in_specs=in_specs,
        out_specs=out_specs,
        out_shape=out_shape,
        compiler_params=_params(("parallel",)),
    )(*args)


def _out_kernel(x_ref, a1_ref, a2_ref, w_ref, o_ref):
    half = w_ref.shape[0] // 2
    o_ref[...] = x_ref[...] + (_dot(a1_ref[...], w_ref[:half, :]) + _dot(a2_ref[...], w_ref[half:, :]))


def _out_proj(x, a1, a2, w, tm):
    m = x.shape[0]
    tm = min(tm, m)
    return pl.pallas_call(
        _out_kernel,
        grid=(m // tm,),
        in_specs=[pl.BlockSpec((tm, D_MODEL), lambda i: (i, 0)),
                  pl.BlockSpec((tm, 512), lambda i: (i, 0)),
                  pl.BlockSpec((tm, 512), lambda i: (i, 0)),
                  pl.BlockSpec((D_MODEL, D_MODEL), lambda i: (0, 0))],
        out_specs=pl.BlockSpec((tm, D_MODEL), lambda i: (i, 0)),
        out_shape=jax.ShapeDtypeStruct((m, D_MODEL), F32),
        compiler_params=_params(("parallel",)),
    )(x, a1, a2, w)


def _ffn_kernel(x_ref, g_ref, wg_ref, wu_ref, wo_ref, o_ref, h_scr, acc_scr):
    f = pl.program_id(1)

    @pl.when(f == 0)
    def _():
        h_scr[...] = _rms_rows(x_ref[...], g_ref[...]).astype(BF16)
        acc_scr[...] = jnp.zeros_like(acc_scr)

    h = h_scr[...]
    gt = jnp.dot(h, wg_ref[...], preferred_element_type=F32)
    up = jnp.dot(h, wu_ref[...], preferred_element_type=F32)
    act = (gt * jax.nn.sigmoid(gt)) * up
    acc_scr[...] += jnp.dot(act.astype(BF16), wo_ref[...], preferred_element_type=F32)

    @pl.when(f == pl.num_programs(1) - 1)
    def _():
        o_ref[...] = x_ref[...] + acc_scr[...]


def _ffn(x, g, w_in, w_out, tm):
    m = x.shape[0]
    tm = min(tm, m)
    nf = 2
    tf = D_FF // nf
    return pl.pallas_call(
        _ffn_kernel,
        grid=(m // tm, nf),
        in_specs=[pl.BlockSpec((tm, D_MODEL), lambda i, f: (i, 0)),
                  pl.BlockSpec((1, D_MODEL), lambda i, f: (0, 0)),
                  pl.BlockSpec((D_MODEL, tf), lambda i, f: (0, f)),
                  pl.BlockSpec((D_MODEL, tf), lambda i, f: (0, nf + f)),
                  pl.BlockSpec((tf, D_MODEL), lambda i, f: (f, 0))],
        out_specs=pl.BlockSpec((tm, D_MODEL), lambda i, f: (i, 0)),
        out_shape=jax.ShapeDtypeStruct((m, D_MODEL), F32),
        scratch_shapes=[pltpu.VMEM((tm, D_MODEL), BF16), pltpu.VMEM((tm, D_MODEL), F32)],
        compiler_params=_params(("parallel", "arbitrary")),
    )(x, g, w_in, w_in, w_out)


def _window_rows_kernel(c_ref, n_ref, o_ref):
    wb = c_ref.shape[-1]
    ts = n_ref.shape[-1]
    for j in range(c_ref.shape[0]):
        o_ref[j] = pltpu.roll(c_ref[j], wb - ts, 1)
        o_ref[j, :, wb - ts:wb] = n_ref[j]


def _window_rows(cache_t, new_t):
    n, f, wb = cache_t.shape
    ts = new_t.shape[-1]
    assert wb >= ts
    nb = 4 if n % 4 == 0 else 1
    return pl.pallas_call(
        _window_rows_kernel,
        grid=(n // nb,),
        in_specs=[pl.BlockSpec((nb, f, wb), lambda i: (i, 0, 0)),
                  pl.BlockSpec((nb, f, ts), lambda i: (i, 0, 0))],
        out_specs=pl.BlockSpec((nb, f, wb), lambda i: (i, 0, 0)),
        out_shape=jax.ShapeDtypeStruct((n, f, wb), F32),
        compiler_params=_params(("parallel",)),
    )(cache_t, new_t)


def _final_norm_kernel(x_ref, g_ref, o_ref):
    o_ref[...] = _rms_rows(x_ref[...], g_ref[...])


def _final_norm(x, g, tm):
    m = x.shape[0]
    tm = min(tm, m)
    return pl.pallas_call(
        _final_norm_kernel,
        grid=(m // tm,),
        in_specs=[pl.BlockSpec((tm, D_MODEL), lambda i: (i, 0)),
                  pl.BlockSpec((1, D_MODEL), lambda i: (0, 0))],
        out_specs=pl.BlockSpec((tm, D_MODEL), lambda i: (i, 0)),
        out_shape=jax.ShapeDtypeStruct((m, D_MODEL), F32),
        compiler_params=_params(("parallel",)),
    )(x, g)


def _prep_kernel(raw_ref, hist_ref, cw_ref, o_ref, ext_scr, *, tc):
    @pl.when(pl.program_id(1) == 0)
    def _():
        ext_scr[0:8, :] = hist_ref[0]

    ext_scr[8:8 + tc, :] = raw_ref[0]
    for s in range(GDN_QKV // LANE):
        o_ref[0, :, s * LANE:(s + 1) * LANE] = _conv_slab(ext_scr, cw_ref, tc, s)
    tail = ext_scr[tc:tc + 8, :]
    ext_scr[0:8, :] = tail


def _gdn_prep(raw, hist, cw, tc):
    b, t, _ = raw.shape
    return pl.pallas_call(
        functools.partial(_prep_kernel, tc=tc),
        grid=(b, t // tc),
        in_specs=[pl.BlockSpec((1, tc, GDN_QKV), lambda i, j: (i, j, 0)),
                  pl.BlockSpec((1, 8, GDN_QKV), lambda i, j: (i, 0, 0)),
                  pl.BlockSpec((8, GDN_QKV), lambda i, j: (0, 0))],
        out_specs=pl.BlockSpec((1, tc, GDN_QKV), lambda i, j: (i, j, 0)),
        out_shape=jax.ShapeDtypeStruct((b, t, GDN_QKV), F32),
        scratch_shapes=[pltpu.VMEM((tc + 8, GDN_QKV), F32)],
        compiler_params=_params(("parallel", "arbitrary")),
    )(raw, hist, cw)


def _tri_inv(mats, ri, ci):
    c = mats[0].shape[0]
    eye = (ri == ci).astype(F32)
    bd = _idiv(ri, INV_BLOCK) == _idiv(ci, INV_BLOCK)
    ad = [jnp.where(bd, a, 0.0) for a in mats]
    ao = [a - d for a, d in zip(mats, ad)]
    split = lambda xs: [_split2(x) for x in xs]
    mm = lambda xs, ys: [_mm_split(x, y) for x, y in zip(xs, ys)]
    pw_s = split(ad)
    td = [eye - d for d in ad]
    k = 2
    while k < INV_BLOCK:
        pw_s = split(mm(pw_s, pw_s))
        td = [t + x for t, x in zip(td, mm(split(td), pw_s))]
        k *= 2
    td_s = split(td)
    n = mm(td_s, split(ao))
    tn = [eye - x for x in n]
    pw_s = split(n)
    k = 2
    while k < c // INV_BLOCK:
        pw_s = split(mm(pw_s, pw_s))
        tn = [t + x for t, x in zip(tn, mm(split(tn), pw_s))]
        k *= 2
    return mm(split(tn), td_s)


def _softplus(x):
    return jnp.maximum(x, 0.0) + jnp.log1p(jnp.exp(-jnp.abs(x)))


def _conv_slab(ext_ref, cw_ref, rows, s):
    cols = slice(s * LANE, (s + 1) * LANE)
    y = jnp.zeros((rows, LANE), F32)
    for j in range(CONV_W):
        y = y + ext_ref[pl.ds(8 - (CONV_W - 1) + j, rows), cols] * cw_ref[j:j + 1, cols]
    y = y * jax.nn.sigmoid(y)
    if s < 2 * GDN_HEADS:
        y = y * lax.rsqrt(jnp.sum(y * y, axis=-1, keepdims=True) + EPS)
        if s < GDN_HEADS:
            y = y * (GDN_DK ** -0.5)
    return y


def _gdn_kernel(qkv_ref, gate_ref, sm_ref, alog_ref, dtb_ref, ng_ref, s0_ref, *rest, ls, nch, fused):
    c = GDN_CHUNK
    nseq = c // ls
    assert nseq == 1 or nch == 1
    heads = range(GDN_HEADS)
    pairs = [(ch, h) for ch in range(nch) for h in heads]
    if fused:
        hist_ref, cw_ref, o_ref, sout_ref, ext_scr = rest
    else:
        o_ref, sout_ref = rest

    @pl.when(pl.program_id(1) == 0)
    def _():
        sout_ref[...] = s0_ref[...]
        if fused:
            ext_scr[0:8, :] = hist_ref[0]

    if fused:
        rows = nch * c
        ext_scr[8:8 + rows, :] = qkv_ref[...]
        prepped = [_conv_slab(ext_scr, cw_ref, rows, s) for s in range(GDN_QKV // LANE)]
        tail = ext_scr[rows:rows + 8, :]
        ext_scr[0:8, :] = tail

    ri = _iota((c, c), 0)
    ci = _iota((c, c), 1)
    same = _idiv(ri, ls) == _idiv(ci, ls)
    lower = same & (ci <= ri)
    strict = same & (ci < ri)
    lo_m = lower.astype(F32)
    same_m = same.astype(F32)
    sm = [sm_ref[ch * c:(ch + 1) * c, :] for ch in range(nch)]

    def slab(ch, s):
        if fused:
            return prepped[s][ch * c:(ch + 1) * c]
        return qkv_ref[ch * c:(ch + 1) * c, s * LANE:(s + 1) * LANE]

    q = [slab(ch, h) for ch, h in pairs]
    k = [slab(ch, GDN_HEADS + h) for ch, h in pairs]
    v = [slab(ch, 2 * GDN_HEADS + h) for ch, h in pairs]
    neg_a = -jnp.exp(alog_ref[0:1, :])
    beta_s = [jax.nn.sigmoid(x) for x in sm]
    g_s = [neg_a * _softplus(x + dtb_ref[0:1, :]) for x in sm]
    beta = [jnp.broadcast_to(beta_s[ch][:, SM_B + h:SM_B + h + 1], (c, LANE)) for ch, h in pairs]
    g_col = [jnp.broadcast_to(g_s[ch][:, SM_A + h:SM_A + h + 1], (c, LANE)) for ch, h in pairs]
    n = range(len(pairs))
    lo_m = lo_m.astype(BF16)
    same_m = same_m.astype(BF16)
    dcy = [_dot01_l(lo_m, g_col[i]) for i in n]
    dcy_row = [x.T for x in dcy]
    if nseq == 1:
        dtot = [jnp.broadcast_to(x[c - 1:c, :], (c, LANE)) for x in dcy]
    else:
        dtot = [_dot01_l(same_m, g_col[i]) for i in n]
    dm = [jnp.where(lower, jnp.exp(jnp.where(lower, dcy[i] - dcy_row[i], 0.0)), 0.0) for i in n]
    kb = [k[i] * beta[i] for i in n]
    kk = [_dot_nt(jnp.concatenate([kb[i], q[i]], axis=0), k[i]) for i in n]
    a_mat = [jnp.where(strict, kk[i][0:c] * dm[i], 0.0) for i in n]
    attn = [jnp.where(lower, kk[i][c:2 * c] * dm[i], 0.0) for i in n]
    edc = [jnp.exp(dcy[i]) for i in n]
    t_inv = _tri_inv(a_mat, ri, ci)
    uw = [_dotx(t_inv[i], jnp.concatenate([v[i] * beta[i], kb[i] * edc[i]], axis=1)) for i in n]
    u = [x[:, 0:LANE] for x in uw]
    w = [x[:, LANE:2 * LANE] for x in uw]
    qd = [q[i] * edc[i] for i in n]
    kdt = [(k[i] * jnp.exp(dtot[i] - dcy[i])).T for i in n]
    gl = [jnp.exp(dtot[i]) for i in n]
    o_all = []
    if nseq == 1:
        s_cur = [sout_ref[0, h] for h in heads]
        for ch in range(nch):
            ix = [ch * GDN_HEADS + h for h in heads]
            ws = [_dot(jnp.concatenate([w[i], qd[i]], axis=0), s_cur[h]) for h, i in zip(heads, ix)]
            v_new = [u[i] - ws[h][0:c] for h, i in zip(heads, ix)]
            o_all += [ws[h][c:2 * c] + _dot(attn[i], v_new[h]) for h, i in zip(heads, ix)]
            s_cur = [s_cur[h] * gl[i][0:1, :] + _dot(kdt[i], v_new[h]) for h, i in zip(heads, ix)]
        for h in heads:
            sout_ref[0, h] = s_cur[h]
    else:
        for h in heads:
            vn, oq = [], []
            for s in range(nseq):
                s_old = sout_ref[s, h]
                rows = slice(s * ls, (s + 1) * ls)
                vn.append(u[h][rows] - _dot(w[h][rows], s_old))
                oq.append(_dot(qd[h][rows], s_old))
            v_new = jnp.concatenate(vn, axis=0)
            o_all.append(jnp.concatenate(oq, axis=0) + _dot(attn[h], v_new))
            for s in range(nseq):
                kdt_s = jnp.where(_idiv(ci, ls) == s, kdt[h], 0.0)
                sout_ref[s, h] = sout_ref[s, h] * gl[h][s * ls:s * ls + 1, :] + _dot(kdt_s, v_new)
    for i, (ch, h) in enumerate(pairs):
        o = o_all[i]
        o = o * lax.rsqrt(jnp.mean(o * o, axis=-1, keepdims=True) + EPS) * ng_ref[...]
        gt = gate_ref[ch * c:(ch + 1) * c, h * LANE:(h + 1) * LANE]
        o_ref[ch * c:(ch + 1) * c, h * LANE:(h + 1) * LANE] = o * (gt * jax.nn.sigmoid(gt))


def _gdn(qkv, gate, sm, alog_b, dtb_b, ng, s0, s0_off, nb, nt, ls, nch, prep=None):
    m = qkv.shape[0]
    c = GDN_CHUNK * nch
    nseq = GDN_CHUNK // ls
    row = lambda i, j: (i * nt + j, 0)
    n_state = nb * nseq
    in_specs = [pl.BlockSpec((c, GDN_QKV), row),
                pl.BlockSpec((c, 512), row),
                pl.BlockSpec((c, LANE), row),
                pl.BlockSpec((8, LANE), lambda i, j: (0, 0)),
                pl.BlockSpec((8, LANE), lambda i, j: (0, 0)),
                pl.BlockSpec((1, LANE), lambda i, j: (0, 0)),
                pl.BlockSpec((nseq, GDN_HEADS, GDN_DK, GDN_DV), lambda i, j: (s0_off + i, 0, 0, 0))]
    args = [qkv, gate, sm, alog_b, dtb_b, ng, s0]
    scratch = []
    if prep is not None:
        in_specs += [pl.BlockSpec((1, 8, GDN_QKV), lambda i, j: (i, 0, 0)),
                     pl.BlockSpec((8, GDN_QKV), lambda i, j: (0, 0))]
        args += list(prep)
        scratch = [pltpu.VMEM((c + 8, GDN_QKV), F32)]
    return pl.pallas_call(
        functools.partial(_gdn_kernel, ls=ls, nch=nch, fused=prep is not None),
        grid=(nb, nt),
        in_specs=in_specs,
        out_specs=[pl.BlockSpec((c, 512), row),
                   pl.BlockSpec((nseq, GDN_HEADS, GDN_DK, GDN_DV), lambda i, j: (i, 0, 0, 0))],
        out_shape=[jax.ShapeDtypeStruct((m, 512), F32),
                   jax.ShapeDtypeStruct((n_state, GDN_HEADS, GDN_DK, GDN_DV), F32)],
        scratch_shapes=scratch,
        compiler_params=_params(("parallel", "arbitrary")),
    )(*args)


def _cmp_kernel(x_ref, pe_ref, wlo_ref, whi_ref, y_ref):
    x = x_ref[0]
    y_ref[0, :, 0:NSA_KV_W] = _dot(x + pe_ref[0, 0:1, :], wlo_ref[0])
    y_ref[0, :, NSA_KV_W:2 * NSA_KV_W] = _dot(x + pe_ref[0, 1:2, :], whi_ref[0])


def _compress(x, pe, wlo, whi, tr):
    nl, r, _ = x.shape
    tr = min(tr, r)
    return pl.pallas_call(
        _cmp_kernel,
        grid=(nl, r // tr),
        in_specs=[pl.BlockSpec((1, tr, SUB_W), lambda l, i: (l, i, 0)),
                  pl.BlockSpec((1, 2, SUB_W), lambda l, i: (l, 0, 0)),
                  pl.BlockSpec((1, SUB_W, NSA_KV_W), lambda l, i: (l, 0, 0)),
                  pl.BlockSpec((1, SUB_W, NSA_KV_W), lambda l, i: (l, 0, 0))],
        out_specs=pl.BlockSpec((1, tr, 2 * NSA_KV_W), lambda l, i: (l, i, 0)),
        out_shape=jax.ShapeDtypeStruct((nl, r, 2 * NSA_KV_W), F32),
        compiler_params=_params(("parallel", "parallel")),
    )(x, pe, wlo, whi)


def _cmp_pool_kernel(pt_ref, pef_ref, wcat_ref, *rest, npg):
    del pt_ref
    pages = rest[:npg]
    y_ref, x_scr, pb_scr = rest[npg:]

    @pl.when(pl.program_id(1) == 0)
    def _():
        for hf in range(2):
            wf = wcat_ref[0, :, hf].reshape(CMP_STRIDE * LANE, 2 * LANE)
            pb_scr[hf] = _dot(pef_ref[0, hf], wf)

    for j in range(npg):
        x = pages[j][0].T
        for hf in range(2):
            x_scr[hf, j * PAGE_SIZE:(j + 1) * PAGE_SIZE, :] = x[:, hf * LANE:(hf + 1) * LANE]
    n = npg * SUB_PP
    for hf in range(2):
        acc = jnp.zeros((n, 2 * LANE), F32)
        for l in range(0, CMP_STRIDE, 2):
            xl = jnp.concatenate([x_scr[hf, pl.ds(l + j, n, stride=CMP_STRIDE), :] for j in range(2)], axis=1)
            acc = acc + _dot(xl, wcat_ref[0, l:l + 2, hf].reshape(2 * LANE, 2 * LANE))
        y_ref[0, :, hf * LANE:(hf + 1) * LANE] = acc[:, 0:LANE] + pb_scr[hf, 0:1, 0:LANE]
        y_ref[0, :, NSA_KV_W + hf * LANE:NSA_KV_W + (hf + 1) * LANE] = acc[:, LANE:] + pb_scr[hf, 1:2, LANE:]


def _compress_pool(page_table, pages_t, pe, wlo, whi, nl, n_pool, seq_per_step):
    db, n_pages = page_table.shape
    assert db % seq_per_step == 0
    npg = seq_per_step * n_pages
    n = npg * SUB_PP
    halves = [slice(hf * LANE, (hf + 1) * LANE) for hf in range(2)]
    wcat = jnp.stack([jnp.concatenate([wlo[:, :, h, h], whi[:, :, h, h]], axis=-1) for h in halves], axis=2)
    pef = jnp.stack([pe[:, :, :, h].reshape(nl, 2, CMP_STRIDE * LANE) for h in halves], axis=1)
    pef = jnp.pad(pef, ((0, 0), (0, 0), (0, 6), (0, 0)))

    def page_map(k):
        s, p = divmod(k, n_pages)
        return lambda l, i, pt: (l * n_pool + pt[i * seq_per_step + s, p], 0, 0)

    in_specs = [pl.BlockSpec((1, 2, 8, CMP_STRIDE * LANE), lambda l, i, pt: (l, 0, 0, 0)),
                pl.BlockSpec((1, CMP_STRIDE, 2, LANE, 2 * LANE), lambda l, i, pt: (l, 0, 0, 0, 0))]
    in_specs += [pl.BlockSpec((1, NSA_KV_W, PAGE_SIZE), page_map(k)) for k in range(npg)]
    grid_spec = pltpu.PrefetchScalarGridSpec(
        num_scalar_prefetch=1,
        grid=(nl, db // seq_per_step),
        in_specs=in_specs,
        out_specs=pl.BlockSpec((1, n, 2 * NSA_KV_W), lambda l, i, pt: (l, i, 0)),
        scratch_shapes=[pltpu.VMEM((2, npg * PAGE_SIZE, LANE), F32),
                        pltpu.VMEM((2, 8, 2 * LANE), F32)])
    return pl.pallas_call(
        functools.partial(_cmp_pool_kernel, npg=npg),
        grid_spec=grid_spec,
        out_shape=jax.ShapeDtypeStruct((nl, db * n_pages * SUB_PP, 2 * NSA_KV_W), F32),
        compiler_params=_params(("parallel", "arbitrary")),
    )(page_table, pef, wcat, *([pages_t] * npg))


def _slope(h):
    return 2.0 ** (-8.0 * (h + 1) / NSA_HEADS)


def _make_units(q, stacked):
    qb = q.shape[0]
    units = []
    for g in range(NSA_KV_HEADS):
        hs = range(g * NSA_REP, (g + 1) * NSA_REP)
        slabs = [q[:, h * NSA_HD:(h + 1) * NSA_HD] * (NSA_HD ** -0.5) for h in hs]
        if stacked:
            rows = jnp.concatenate(slabs, axis=0).astype(BF16)
            slope = jnp.concatenate([jnp.full((qb, 1), _slope(h), F32) for h in hs], axis=0)
            units.append([(rows, slope, NSA_REP)])
        else:
            units.append([(s.astype(BF16), _slope(h), 1) for s, h in zip(slabs, hs)])
    return units


def _cmp_branch(units, pq, ckv, ncp):
    qb = pq.shape[0]
    n_ix = _iota((1, ncp), 1)
    c_end = n_ix * CMP_STRIDE + (CMP_LEN - 1)
    c_ctr = (n_ix * CMP_STRIDE).astype(F32) + 0.5 * (CMP_LEN - 1)
    pool = (_idiv(_iota((ncp, LANE), 0), SEL_BLOCK // CMP_STRIDE) == _iota((ncp, LANE), 1)).astype(F32)
    ckv_b = ckv.astype(BF16)
    geo = {}
    for g in range(NSA_KV_HEADS):
        for _, _, rep in units[g]:
            if rep not in geo:
                pqr = _rep(pq, rep)
                geo[rep] = (c_end <= pqr, pqr.astype(F32) - c_ctr)
    flat = [(g, rows, slope, rep) for g in range(NSA_KV_HEADS) for rows, slope, rep in units[g]]
    kc = [ckv_b[:, g * NSA_HD:(g + 1) * NSA_HD] for g in range(NSA_KV_HEADS)]
    vc = [ckv_b[:, (NSA_KV_HEADS + g) * NSA_HD:(NSA_KV_HEADS + g + 1) * NSA_HD] for g in range(NSA_KV_HEADS)]
    s = [_dot_nt(rows, kc[g]) for g, rows, _, _ in flat]
    s = [jnp.where(geo[rep][0], si - slope * geo[rep][1], NEG_INF) for si, (_, _, slope, rep) in zip(s, flat)]
    m = [jnp.max(si, axis=-1, keepdims=True) for si in s]
    e = [jnp.where(geo[rep][0], jnp.exp(si - mi), 0.0) for si, mi, (_, _, _, rep) in zip(s, m, flat)]
    den = [jnp.sum(ei, axis=-1, keepdims=True) for ei in e]
    p = [ei / jnp.where(di > 0.0, di, 1.0) for ei, di in zip(e, den)]
    o = [_dot(pi, vc[g]) for pi, (g, _, _, _) in zip(p, flat)]
    outs, imps = [], []
    for g in range(NSA_KV_HEADS):
        psum = jnp.zeros((qb, ncp), F32)
        og = []
        for pi, oi, (gi, _, _, rep) in zip(p, o, flat):
            if gi == g:
                og.append(oi)
                for r in range(rep):
                    psum = psum + pi[r * qb:(r + 1) * qb]
        outs.append(og)
        imps.append(psum)
    imps = [_dot01_r(ps, pool) for ps in imps]
    return outs, imps


def _select_blocks(imps, pq, ax):
    blk = _iota(imps[0].shape, ax)
    cur = _idiv(pq, SEL_BLOCK)
    forced = (blk == 0) | (blk == cur) | (blk == cur - 1)
    bonus = jnp.where(forced, FORCE_BONUS, 0.0)
    work = [jnp.where(blk <= cur, imp + bonus, NEG_INF) for imp in imps]
    idx = blk.astype(F32)
    sel = [jnp.zeros(w.shape, F32) for w in work]
    for _ in range(N_SEL):
        m = [jnp.max(w, axis=ax, keepdims=True) for w in work]
        first = [jnp.min(jnp.where(w == mi, idx, 2.0 * LANE), axis=ax, keepdims=True) for w, mi in zip(work, m)]
        hit = [idx == f for f in first]
        sel = [jnp.where(h, 1.0, s) for h, s in zip(hit, sel)]
        work = [jnp.where(h, -jnp.inf, w) for h, w in zip(hit, work)]
    return sel


def _select_blocks_rank(imps, pq, nb):
    qb = imps[0].shape[0]
    blk = _iota((qb, LANE), 1)
    cur = _idiv(pq, SEL_BLOCK)
    forced = (blk == 0) | (blk == cur) | (blk == cur - 1)
    bonus = jnp.where(forced, FORCE_BONUS, 0.0)
    score = [jnp.where(blk <= cur, imp + bonus, NEG_INF) for imp in imps]
    fill = jnp.full((LANE - len(imps) * qb, LANE), NEG_INF, F32)
    st = jnp.concatenate(score + [fill], axis=0).T[0:nb]
    jrow = _iota((nb, LANE), 0)
    cnt = jnp.zeros((nb, LANE), F32)
    for j in range(nb):
        row = st[j:j + 1, :]
        cnt = cnt + jnp.where(jrow > j, jnp.where(row >= st, 1.0, 0.0), jnp.where(row > st, 1.0, 0.0))
    sel_t = jnp.where(cnt < N_SEL, 1.0, 0.0)
    sel = jnp.concatenate([sel_t, jnp.zeros((LANE - nb, LANE), F32)], axis=0).T
    return [sel[g * qb:(g + 1) * qb] for g in range(len(imps))]


def _slc_tile(units, sel, pq, kvt, k0, tk, causal, flags, m_scr, l_scr, acc_scr):
    key = k0 + _iota((1, tk), 1)
    dist_i = pq - key
    dist = dist_i.astype(F32)
    e_tile = (_iota((LANE, tk), 0) == _idiv(k0 + _iota((LANE, tk), 1), SEL_BLOCK)).astype(BF16)
    ui = 0
    for g in range(NSA_KV_HEADS):
        def group(g=g, ui=ui):
            on = jnp.dot(sel[g].astype(BF16), e_tile, preferred_element_type=F32) > 0.5
            if causal:
                on = on & (dist_i >= 0)
            bias = jnp.where(on, 0.0, NEG_INF)
            kt = kvt[g * NSA_HD:(g + 1) * NSA_HD, :]
            vt = kvt[(NSA_KV_HEADS + g) * NSA_HD:(NSA_KV_HEADS + g + 1) * NSA_HD, :]
            us = units[g]
            ids = [ui + j for j in range(len(us))]
            s = [jnp.dot(rows, kt, preferred_element_type=F32) for rows, _, _ in us]
            s = [si + (_rep(bias, rep) - slope * _rep(dist, rep)) for si, (_, slope, rep) in zip(s, us)]
            m_prev = [m_scr[u][:, 0:1] for u in ids]
            m_new = [jnp.maximum(mp, jnp.max(si, axis=-1, keepdims=True)) for mp, si in zip(m_prev, s)]
            alpha = [jnp.exp(mp - mn) for mp, mn in zip(m_prev, m_new)]
            p = [jnp.exp(si - mn) for si, mn in zip(s, m_new)]
            l_new = [a * l_scr[u][:, 0:1] + jnp.sum(pi, axis=-1, keepdims=True) for a, u, pi in zip(alpha, ids, p)]
            pv = [lax.dot_general(pi.astype(BF16), vt, _NT, preferred_element_type=F32) for pi in p]
            for u, a, pvi, mn, ln in zip(ids, alpha, pv, m_new, l_new):
                acc_scr[u] = a * acc_scr[u] + pvi
                m_scr[u] = jnp.broadcast_to(mn, m_scr.shape[1:])
                l_scr[u] = jnp.broadcast_to(ln, l_scr.shape[1:])

        if flags is None:
            group()
        else:
            pl.when(flags[g])(group)
        ui += len(units[g])


def _win_branch(units, pq, kvt, w0):
    wk = kvt.shape[1]
    dist_i = pq - (w0 + _iota((1, wk), 1))
    dist = dist_i.astype(F32)
    bias = jnp.where((dist_i >= 0) & (dist_i < WINDOW), 0.0, NEG_INF)
    flat = [(g, rows, slope, rep) for g in range(NSA_KV_HEADS) for rows, slope, rep in units[g]]
    kt = [kvt[g * NSA_HD:(g + 1) * NSA_HD, :] for g in range(NSA_KV_HEADS)]
    vt = [kvt[(NSA_KV_HEADS + g) * NSA_HD:(NSA_KV_HEADS + g + 1) * NSA_HD, :] for g in range(NSA_KV_HEADS)]
    s = [jnp.dot(rows, kt[g], preferred_element_type=F32) for g, rows, _, _ in flat]
    s = [si + (_rep(bias, rep) - slope * _rep(dist, rep)) for si, (_, _, slope, rep) in zip(s, flat)]
    m = [jnp.max(si, axis=-1, keepdims=True) for si in s]
    p = [jnp.exp(si - mi) for si, mi in zip(s, m)]
    den = [jnp.sum(pi, axis=-1, keepdims=True) for pi in p]
    pv = [lax.dot_general(pi.astype(BF16), vt[g], _NT, preferred_element_type=F32) for pi, (g, _, _, _) in zip(p, flat)]
    o = [pvi / di for pvi, di in zip(pv, den)]
    return [[oi for oi, (gi, _, _, _) in zip(o, flat) if gi == g] for g in range(NSA_KV_HEADS)]


def _nsa_finish(units, o_c, o_w, qb, sm, ng_ref, o_ref, l_scr, acc_scr):
    gates = jax.nn.sigmoid(sm)
    per_group = len(units[0])
    for h in range(NSA_HEADS):
        g, r = divmod(h, NSA_REP)
        if per_group == 1:
            rows = slice(r * qb, (r + 1) * qb)
            u, j = g, 0
        else:
            rows = slice(0, qb)
            u, j = h, r
        oc = o_c[g][j][rows]
        ow = o_w[g][j][rows]
        o_s = acc_scr[u][rows] / l_scr[u][rows][:, 0:1]
        c0 = SM_G + 3 * h
        o = gates[:, c0:c0 + 1] * oc + gates[:, c0 + 1:c0 + 2] * o_s + gates[:, c0 + 2:c0 + 3] * ow
        o = o * lax.rsqrt(jnp.mean(o * o, axis=-1, keepdims=True) + EPS) * ng_ref[...]
        o_ref[:, h * NSA_HD:(h + 1) * NSA_HD] = o


def _slc_init(m_scr, l_scr, acc_scr):
    m_scr[...] = jnp.full(m_scr.shape, NEG_INF, F32)
    l_scr[...] = jnp.zeros(l_scr.shape, F32)
    acc_scr[...] = jnp.zeros(acc_scr.shape, F32)


def _key_aug(t_col, lane, qblk):
    rel = (_idiv(t_col, SEL_BLOCK) - qblk).astype(F32)
    off = (t_col & (SEL_BLOCK - 1)).astype(F32)
    return jnp.where(lane < 4, jnp.where((lane & 1) == 0, rel, off), 0.0).astype(BF16)


def _nsa_prompt_kernel(qt_ref, smt_ref, y_ref, sk_ref, sv_ref, wk_ref, wv_ref, ngb_ref, o_ref,
                       kc_scr, vct_scr, selb_scr, m_scr, l_scr, acc_scr, *, t_len):
    i = pl.program_id(1)
    nsub = t_len // CMP_STRIDE
    gw = NSA_REP * Q_BLOCK
    half = NSA_KV_W // 2
    groups = range(NSA_KV_HEADS)

    @pl.when(i == 0)
    def _():
        y = y_ref[0]
        ckv = y[:, 0:NSA_KV_W] + pltpu.roll(y[:, NSA_KV_W:2 * NSA_KV_W], nsub - 1, 0)
        kc_scr[...] = ckv[:, 0:half].astype(BF16)
        vct_scr[...] = ckv[:, half:].T.astype(BF16)

    pos0 = i * Q_BLOCK
    qblk = pos0 // SEL_BLOCK
    pq_row = pos0 + _iota((1, Q_BLOCK), 1)
    pq_g = jnp.concatenate([pq_row] * NSA_REP, axis=1)
    qt = qt_ref[0] * (NSA_HD ** -0.5 * LOG2E)
    zeros = jnp.zeros((NSA_HD, gw), F32)
    slope_g, rq, rfull = [], [], []
    aug_row = _iota((LANE, gw), 0)
    for g in groups:
        hs = range(g * NSA_REP, (g + 1) * NSA_REP)
        qg = jnp.concatenate([qt[h * NSA_HD:(h + 1) * NSA_HD, :] for h in hs], axis=1)
        rq.append(jnp.concatenate([qg, zeros] if g == 0 else [zeros, qg], axis=0).astype(BF16))
        sl = jnp.concatenate([jnp.full((1, Q_BLOCK), _slope(h) * LOG2E, F32) for h in hs], axis=1)
        slope_g.append(sl)
        sl_hi = sl.astype(BF16).astype(F32)
        sl_p = jnp.where(aug_row < 2, sl_hi, sl - sl_hi)
        raug = jnp.where(aug_row < 4, jnp.where((aug_row & 1) == 0, SEL_BLOCK * sl_p, sl_p), 0.0).astype(BF16)
        rfull.append(jnp.concatenate([rq[g], raug], axis=0))

    n_col = _iota((nsub, gw), 0)
    ok = (n_col * CMP_STRIDE + (CMP_LEN - 1)) <= pq_g
    dist = pq_g.astype(F32) - ((n_col * CMP_STRIDE).astype(F32) + 0.5 * (CMP_LEN - 1))
    kc = kc_scr[...]
    s = [jnp.dot(kc, rq[g], preferred_element_type=F32) for g in groups]
    s = [jnp.where(ok, s[g] - slope_g[g] * dist, NEG_INF) for g in groups]
    m = [jnp.max(x, axis=0, keepdims=True) for x in s]
    e = [jnp.where(ok, jnp.exp2(s[g] - m[g]), 0.0) for g in groups]
    den = [jnp.sum(x, axis=0, keepdims=True) for x in e]
    p = [e[g] / jnp.where(den[g] > 0.0, den[g], 1.0) for g in groups]
    o_c = [jnp.dot(vct_scr[g * NSA_HD:(g + 1) * NSA_HD, :], p[g].astype(BF16), preferred_element_type=F32)
           for g in groups]
    psum = [sum(p[g][:, r * Q_BLOCK:(r + 1) * Q_BLOCK] for r in range(NSA_REP)) for g in groups]
    pool_t = (_iota((LANE, nsub), 0) == _idiv(_iota((LANE, nsub), 1), SEL_BLOCK // CMP_STRIDE)).astype(F32)
    imp_t = [_dot01_l(pool_t, psum[g]) for g in groups]

    sel = _select_blocks(imp_t, pq_row, 0)
    for g in groups:
        selb_scr[g] = jnp.where(sel[g] > 0.5, 0.0, NEG_INF)
    used = [jnp.max(x, axis=1, keepdims=True) for x in sel]
    tile_of_blk = _idiv(_iota((LANE, 1), 0), SLC_TILE // SEL_BLOCK)

    m_scr[...] = jnp.full(m_scr.shape, NEG_INF, F32)
    l_scr[...] = jnp.zeros(l_scr.shape, F32)
    acc_scr[...] = jnp.zeros(acc_scr.shape, F32)
    n_tiles = (pos0 + Q_BLOCK + SLC_TILE - 1) // SLC_TILE
    blk_per_tile = SLC_TILE // SEL_BLOCK

    def slc_tile(kt, causal):
        k0 = kt * SLC_TILE
        t_col = k0 + _iota((SLC_TILE, LANE), 0)
        lhs = jnp.concatenate([sk_ref[0, kt], _key_aug(t_col, _iota((SLC_TILE, LANE), 1), qblk)], axis=1)
        bias = [jnp.concatenate(
            [jnp.broadcast_to(selb_scr[g, pl.ds(kt * blk_per_tile + j, 1), :], (SEL_BLOCK, Q_BLOCK))
             for j in range(blk_per_tile)], axis=0) for g in groups]
        if causal:
            bias = [jnp.where(t_col <= pq_row, b, NEG_INF) for b in bias]
        sc = [jnp.dot(lhs, rfull[g], preferred_element_type=F32) + jnp.concatenate([bias[g]] * NSA_REP, axis=1)
              for g in groups]
        m_prev = [m_scr[g, 0:1, :] for g in groups]
        m_new = [jnp.maximum(m_prev[g], jnp.max(sc[g], axis=0, keepdims=True)) for g in groups]
        alpha = [jnp.exp2(m_prev[g] - m_new[g]) for g in groups]
        pr = [jnp.exp2(sc[g] - m_new[g]) for g in groups]
        l_new = [alpha[g] * l_scr[g, 0:1, :] + jnp.sum(pr[g], axis=0, keepdims=True) for g in groups]
        pv = [jnp.dot(sv_ref[0, kt, g * NSA_HD:(g + 1) * NSA_HD, :], pr[g].astype(BF16),
                      preferred_element_type=F32) for g in groups]
        for g in groups:
            acc_scr[g] = alpha[g] * acc_scr[g] + pv[g]
            l_scr[g, 0:1, :] = l_new[g]
            m_scr[g, 0:1, :] = m_new[g]

    used_any = jnp.maximum(used[0], used[1])

    def body(kt, carry):
        @pl.when(jnp.max(jnp.where(tile_of_blk == kt, used_any, 0.0)) > 0.5)
        def _():
            slc_tile(kt, False)
        return carry

    lax.fori_loop(0, n_tiles - 1, body, 0)
    slc_tile(n_tiles - 1, True)

    wt0 = jnp.maximum(i - (WIN_TILES - 1), 0)
    wkeys = WIN_TILES * Q_BLOCK
    kw = jnp.concatenate([wk_ref[0, wt0 + j] for j in range(WIN_TILES)], axis=0)
    vw = jnp.concatenate([wv_ref[0, wt0 + j] for j in range(WIN_TILES)], axis=1)
    tw = wt0 * Q_BLOCK + _iota((wkeys, LANE), 0)
    lhs_w = jnp.concatenate([kw, _key_aug(tw, _iota((wkeys, LANE), 1), qblk)], axis=1)
    dist_w = pq_row - tw
    bias_w = jnp.where((dist_w >= 0) & (dist_w < WINDOW), 0.0, NEG_INF)
    bias_w = jnp.concatenate([bias_w] * NSA_REP, axis=1)
    sw = [jnp.dot(lhs_w, rfull[g], preferred_element_type=F32) + bias_w for g in groups]
    mw = [jnp.max(x, axis=0, keepdims=True) for x in sw]
    pw = [jnp.exp2(sw[g] - mw[g]) for g in groups]
    dw = [jnp.sum(x, axis=0, keepdims=True) for x in pw]
    o_w = [jnp.dot(vw[g * NSA_HD:(g + 1) * NSA_HD, :], pw[g].astype(BF16), preferred_element_type=F32) / dw[g]
           for g in groups]

    gates = jax.nn.sigmoid(smt_ref[0])
    outs = []
    for h in range(NSA_HEADS):
        g, r = divmod(h, NSA_REP)
        lanes = slice(r * Q_BLOCK, (r + 1) * Q_BLOCK)
        o_s = acc_scr[g][:, lanes] / l_scr[g, 0:1, lanes]
        c0 = SM_G + 3 * h
        o = gates[c0:c0 + 1] * o_c[g][:, lanes] + gates[c0 + 1:c0 + 2] * o_s + gates[c0 + 2:c0 + 3] * o_w[g][:, lanes]
        outs.append(o * lax.rsqrt(jnp.mean(o * o, axis=0, keepdims=True) + EPS) * ngb_ref[...])
    o_ref[...] = jnp.concatenate(outs, axis=0).T


def _nsa_prompt(nq_t, smt, y, sk, sv, wk, wv, ngb, b, t_len):
    nqb = t_len // Q_BLOCK
    nsub = t_len // CMP_STRIDE
    nst = t_len // SLC_TILE
    half = NSA_KV_W // 2
    gw = NSA_REP * Q_BLOCK
    whole = lambda bi, i: (bi, 0, 0, 0)
    return pl.pallas_call(
        functools.partial(_nsa_prompt_kernel, t_len=t_len),
        grid=(b, nqb),
        in_specs=[pl.BlockSpec((1, 512, Q_BLOCK), lambda bi, i: (bi, 0, i)),
                  pl.BlockSpec((1, 32, Q_BLOCK), lambda bi, i: (bi * nqb + i, 0, 0)),
                  pl.BlockSpec((1, nsub, 2 * NSA_KV_W), lambda bi, i: (bi, 0, 0)),
                  pl.BlockSpec((1, nst, SLC_TILE, half), whole),
                  pl.BlockSpec((1, nst, half, SLC_TILE), whole),
                  pl.BlockSpec((1, nqb, Q_BLOCK, half), whole),
                  pl.BlockSpec((1, nqb, half, Q_BLOCK), whole),
                  pl.BlockSpec((NSA_HD, Q_BLOCK), lambda bi, i: (0, 0))],
        out_specs=pl.BlockSpec((Q_BLOCK, 512), lambda bi, i: (bi * nqb + i, 0)),
        out_shape=jax.ShapeDtypeStruct((b * t_len, 512), F32),
        scratch_shapes=[pltpu.VMEM((nsub, half), BF16),
                        pltpu.VMEM((half, nsub), BF16),
                        pltpu.VMEM((NSA_KV_HEADS, LANE, Q_BLOCK), F32),
                        pltpu.VMEM((NSA_KV_HEADS, 8, gw), F32),
                        pltpu.VMEM((NSA_KV_HEADS, 8, gw), F32),
                        pltpu.VMEM((NSA_KV_HEADS, NSA_HD, gw), F32)],
        compiler_params=_params(("parallel", "arbitrary")),
    )(nq_t, smt, y, sk, sv, wk, wv, ngb)


def _nsa_sample_kernel(pt_ref, q_ref, sm_ref, ypast_ref, ynew_ref, slcn_ref, winn_ref, winc_ref, ng_ref, *rest,
                       n_pages, past_len, dec_len):
    del pt_ref
    s_pages = rest[0:n_pages]
    o_ref = rest[n_pages]
    slc_scr, win_scr, m_scr, l_scr, acc_scr = rest[n_pages + 1:]
    ncp = n_pages * SUB_PP
    tk = slc_scr.shape[1]
    wb = winc_ref.shape[2]

    y = jnp.concatenate([ypast_ref[0], ynew_ref[0]], axis=0)
    ckv = (y[:, 0:NSA_KV_W] + pltpu.roll(y[:, NSA_KV_W:2 * NSA_KV_W], ncp + 7, 0))[0:ncp]

    def new_cols(ref):
        pad = jnp.zeros((PAGE_SIZE - dec_len, NSA_KV_W), F32)
        return jnp.concatenate([ref[...], pad], axis=0).T.astype(BF16)

    for p in range(n_pages):
        slc_scr[:, p * PAGE_SIZE:(p + 1) * PAGE_SIZE] = s_pages[p][0].astype(BF16)
    slc_scr[:, past_len:tk] = new_cols(slcn_ref)
    win_scr[:, 0:wb] = winc_ref[0].astype(BF16)
    win_scr[:, wb:wb + PAGE_SIZE] = new_cols(winn_ref)

    pq = past_len + _iota((dec_len, 1), 0)
    units = _make_units(q_ref[...], True)
    o_c, imps = _cmp_branch(units, pq, ckv, ncp)
    n_blk = (past_len + dec_len - 1) // SEL_BLOCK + 1
    sel = _select_blocks_rank(imps, pq, -(-n_blk // 8) * 8)
    _slc_init(m_scr, l_scr, acc_scr)
    _slc_tile(units, sel, pq, slc_scr[...], 0, tk, True, None, m_scr, l_scr, acc_scr)
    o_w = _win_branch(units, pq, win_scr[...], past_len - wb)
    _nsa_finish(units, o_c, o_w, dec_len, sm_ref[...], ng_ref, o_ref, l_scr, acc_scr)


def _nsa_sample(page_table, nq, sm, ynew, slc_new, win_new, win_cache_t, ng, y_past, slc_pool_t, layer, n_pool,
                past_len, dec_len):
    db, n_pages = page_table.shape
    wb = win_cache_t.shape[2]
    tk = past_len + PAGE_SIZE
    wk = wb + PAGE_SIZE
    rows = NSA_REP * dec_len
    row = lambda b, pt: (b, 0)

    def page_map(k):
        return lambda b, pt: (layer * n_pool + pt[b, k], 0, 0)

    in_specs = [pl.BlockSpec((dec_len, 512), row),
                pl.BlockSpec((dec_len, LANE), row),
                pl.BlockSpec((1, n_pages * SUB_PP, 2 * NSA_KV_W), lambda b, pt: (layer * db + b, 0, 0)),
                pl.BlockSpec((1, 8, 2 * NSA_KV_W), lambda b, pt: (b, 0, 0)),
                pl.BlockSpec((dec_len, NSA_KV_W), row),
                pl.BlockSpec((dec_len, NSA_KV_W), row),
                pl.BlockSpec((1, NSA_KV_W, wb), lambda b, pt: (layer * db + b, 0, 0)),
                pl.BlockSpec((1, NSA_HD), lambda b, pt: (0, 0))]
    in_specs += [pl.BlockSpec((1, NSA_KV_W, PAGE_SIZE), page_map(k)) for k in range(n_pages)]
    grid_spec = pltpu.PrefetchScalarGridSpec(
        num_scalar_prefetch=1,
        grid=(db,),
        in_specs=in_specs,
        out_specs=pl.BlockSpec((dec_len, 512), row),
        scratch_shapes=[pltpu.VMEM((NSA_KV_W, tk), BF16),
                        pltpu.VMEM((NSA_KV_W, wk), BF16),
                        pltpu.VMEM((NSA_KV_HEADS, rows, LANE), F32),
                        pltpu.VMEM((NSA_KV_HEADS, rows, LANE), F32),
                        pltpu.VMEM((NSA_KV_HEADS, rows, NSA_HD), F32)])
    return pl.pallas_call(
        functools.partial(_nsa_sample_kernel, n_pages=n_pages, past_len=past_len, dec_len=dec_len),
        grid_spec=grid_spec,
        out_shape=jax.ShapeDtypeStruct((db * dec_len, 512), F32),
        compiler_params=_params(("arbitrary",)),
    )(page_table, nq, sm, y_past, ynew, slc_new, win_new, win_cache_t, ng, *([slc_pool_t] * n_pages))


def _reorder_proj(w):
    sizes = (GDN_QKV, GDN_HEADS, GDN_HEADS, GDN_HEADS * GDN_DV, NSA_HEADS * NSA_HD,
             NSA_KV_W, NSA_KV_W, NSA_KV_W, 3 * NSA_HEADS)
    off = [0] + [int(v) for v in np.cumsum(sizes)]
    seg = lambda i: w[:, :, off[i]:off[i + 1]].astype(BF16)
    pad = jnp.zeros(w.shape[:2] + (N_PROJ - off[-1],), BF16)
    return jnp.concatenate([seg(0), seg(3), seg(4), seg(5), seg(6), seg(7), seg(1), seg(2), seg(8), pad], axis=2)


def _cmp_weights(cmp_w):
    def half(w):
        rows = []
        for c in range(2):
            for g in range(NSA_KV_HEADS):
                j = c * NSA_KV_HEADS + g
                rows.append(jnp.pad(w[:, :, c], ((0, 0), (0, 0), (0, 0), (j * NSA_HD, NSA_KV_W - (j + 1) * NSA_HD))))
        return jnp.concatenate(rows, axis=2).astype(BF16)

    return half(cmp_w[:, :CMP_STRIDE]), half(cmp_w[:, CMP_STRIDE:])


def _cmp_pe(cmp_pe):
    nl = cmp_pe.shape[0]

    def half(p):
        return jnp.broadcast_to(p[:, :, :, None, :], (nl, CMP_STRIDE, 2, NSA_KV_HEADS, NSA_HD)).reshape(nl, SUB_W)

    return jnp.stack([half(cmp_pe[:, :CMP_STRIDE]), half(cmp_pe[:, CMP_STRIDE:])], axis=1)


def _small_t(sm, c):
    m = sm.shape[0]
    return jnp.transpose(sm[:, :32].reshape(m // c, c, 32), (0, 2, 1))


def _feature_major(a):
    nd = a.ndim
    perm = tuple(range(nd - 4)) + (nd - 3, nd - 2, nd - 1, nd - 4)
    t = jnp.transpose(a, perm)
    return t.reshape(t.shape[:nd - 4] + (NSA_KV_W, t.shape[-1]))


def _row_major_kv(a_t):
    lead = a_t.shape[:-2]
    n = len(lead)
    t = a_t.reshape(lead + (2, NSA_KV_HEADS, NSA_HD, a_t.shape[-1]))
    return jnp.transpose(t, tuple(range(n)) + (n + 3, n, n + 1, n + 2))


def kernel(x_prompt, x_sample, cache_cmp_kv, cache_slc_kv, page_table, cache_win_kv, state_gdn, state_conv,
           norm_mix, w_in, conv_w, gdn_a_log, gdn_dt_bias, gdn_norm, nsa_cmp_w, nsa_cmp_pe, nsa_norm,
           w_out, norm_ffn, w_ffn_in, w_ffn_out, norm_final):
    bp, tp, _ = x_prompt.shape
    db, ts, _ = x_sample.shape
    n_pool = cache_cmp_kv.shape[1]
    n_pages = page_table.shape[1]
    past_len = n_pages * PAGE_SIZE
    wb = cache_win_kv.shape[2]
    kv_row = (2, NSA_KV_HEADS, NSA_HD)
    wkeep = min(WINDOW, tp)
    assert GDN_CHUNK % ts == 0 and (db * ts) % GDN_CHUNK == 0 and tp % SLC_TILE == 0
    assert tp >= WIN_TILES * Q_BLOCK and db % POOL_SEQS == 0

    w_in_b = _reorder_proj(w_in)
    w_t = jnp.transpose(jnp.concatenate([w_in_b[:, :, C_KV:C_SMALL], w_in_b[:, :, C_NQ:C_KV]], axis=2),
                        (0, 2, 1))
    w_out_b = w_out.astype(BF16)
    w_ffn_in_b = w_ffn_in.astype(BF16)
    w_ffn_out_b = w_ffn_out.astype(BF16)
    cw = jnp.pad(conv_w, ((0, 0), (0, 8 - CONV_W), (0, 0)))
    lane_pad = ((0, 0), (SM_A, LANE - SM_A - GDN_HEADS))
    alog_b = jnp.pad(jnp.pad(gdn_a_log, lane_pad)[:, None, :], ((0, 0), (0, 7), (0, 0)))
    dtb_b = jnp.pad(jnp.pad(gdn_dt_bias, lane_pad)[:, None, :], ((0, 0), (0, 7), (0, 0)))
    wlo4, whi4 = _cmp_weights(nsa_cmp_w)
    wlo = wlo4.reshape(DEPTH, SUB_W, NSA_KV_W)
    whi = whi4.reshape(DEPTH, SUB_W, NSA_KV_W)
    pe2 = _cmp_pe(nsa_cmp_pe)

    cmp_pool_t = _feature_major(cache_cmp_kv).reshape(DEPTH * n_pool, NSA_KV_W, PAGE_SIZE)
    y_pool = _compress_pool(page_table, cmp_pool_t, pe2.reshape(DEPTH, 2, CMP_STRIDE, NSA_KV_W), wlo4, whi4,
                            DEPTH, n_pool, POOL_SEQS)
    y_pool = y_pool.reshape(DEPTH * db, n_pages * SUB_PP, 2 * NSA_KV_W)
    slc_pool_t = _feature_major(cache_slc_kv).reshape(DEPTH * n_pool, NSA_KV_W, PAGE_SIZE)
    win_cache_t = _feature_major(cache_win_kv)
    state_all = state_gdn.reshape(DEPTH * db, GDN_HEADS, GDN_DK, GDN_DV)
    seq_per_chunk = GDN_CHUNK // ts

    xp = x_prompt.reshape(bp * tp, D_MODEL)
    xs = x_sample.reshape(db * ts, D_MODEL)
    zero_hist = jnp.zeros((bp, 8, GDN_QKV), F32)
    zero_state = jnp.zeros((bp, GDN_HEADS, GDN_DK, GDN_DV), F32)
    outs = [[] for _ in range(10)]
    for l in range(DEPTH):
        g_mix = norm_mix[l][None, :]
        g_ffn = norm_ffn[l][None, :]
        gdn_g = gdn_norm[l][None, :]
        nsa_g = nsa_norm[l][None, :]

        (qkv, gate, _, ncmp, _, _, sm, cmp_t, slc_t, win_t, nq_t, sk, sv, wk, wv) = _proj(
            xp, g_mix, w_in_b[l], w_t[l], SLC_TILE, bp)
        smt = _small_t(sm, GDN_CHUNK)
        o_gdn, s_new = _gdn(qkv, gate, sm, alog_b[l], dtb_b[l], gdn_g, zero_state, 0,
                            bp, tp // (GDN_CHUNK * GDN_NCH), GDN_CHUNK, GDN_NCH, prep=(zero_hist, cw[l]))
        y = _compress(ncmp.reshape(1, bp * tp // CMP_STRIDE, SUB_W), pe2[l:l + 1], wlo[l:l + 1],
                      whi[l:l + 1], 512).reshape(bp, tp // CMP_STRIDE, 2 * NSA_KV_W)
        o_nsa = _nsa_prompt(nq_t, smt, y, sk, sv, wk, wv, jnp.broadcast_to(nsa_norm[l][:, None], (NSA_HD, Q_BLOCK)),
                            bp, tp)
        xp = _out_proj(xp, o_gdn, o_nsa, w_out_b[l], 512)
        xp = _ffn(xp, g_ffn, w_ffn_in_b[l], w_ffn_out_b[l], 1024)
        outs[0].append(cmp_t)
        outs[2].append(slc_t)
        outs[4].append(win_t[:, :, tp - wkeep:])
        outs[6].append(s_new)
        outs[8].append(qkv.reshape(bp, tp, GDN_QKV)[:, -(CONV_W - 1):])

        qkv, gate, nq, ncmp, nslc, nwin, sm = _proj(xs, g_mix, w_in_b[l], None, 512)
        hist = jnp.pad(state_conv[l], ((0, 0), (8 - (CONV_W - 1), 0), (0, 0)))
        qkvn = _gdn_prep(qkv.reshape(db, ts, GDN_QKV), hist, cw[l], ts).reshape(db * ts, GDN_QKV)
        o_gdn, s_new = _gdn(qkvn, gate, sm, alog_b[l], dtb_b[l], gdn_g, state_all,
                            l * (db // seq_per_chunk), db // seq_per_chunk, 1, ts, 1)
        new_sub = jnp.pad(ncmp.reshape(db, 1, ts * NSA_KV_W), ((0, 0), (0, 7), (0, SUB_W - ts * NSA_KV_W)))
        ynew = _compress(new_sub.reshape(1, db * 8, SUB_W), pe2[l:l + 1], wlo[l:l + 1], whi[l:l + 1],
                         512).reshape(db, 8, 2 * NSA_KV_W)
        o_nsa = _nsa_sample(page_table, nq, sm, ynew, nslc, nwin, win_cache_t.reshape(DEPTH * db, NSA_KV_W, wb),
                            nsa_g, y_pool, slc_pool_t, l, n_pool, past_len, ts)
        xs = _out_proj(xs, o_gdn, o_nsa, w_out_b[l], 512)
        xs = _ffn(xs, g_ffn, w_ffn_in_b[l], w_ffn_out_b[l], 1024)
        outs[1].append(ncmp.reshape((db, ts) + kv_row))
        outs[3].append(nslc.reshape((db, ts) + kv_row))
        outs[5].append(jnp.transpose(nwin.reshape(db, ts, NSA_KV_W), (0, 2, 1)))
        outs[7].append(s_new)
        outs[9].append(jnp.concatenate([state_conv[l], qkv.reshape(db, ts, GDN_QKV)], axis=1)[:, -(CONV_W - 1):])

    y_prompt = _final_norm(xp, norm_final[None, :], 512).reshape(bp, tp, D_MODEL)
    y_sample = _final_norm(xs, norm_final[None, :], 512).reshape(db, ts, D_MODEL)
    res = [jnp.stack(o) for o in outs]
    res[5] = _window_rows(win_cache_t.reshape(DEPTH * db, NSA_KV_W, wb),
                          res[5].reshape(DEPTH * db, NSA_KV_W, ts)).reshape(DEPTH, db, NSA_KV_W, wb)
    for i in (0, 2, 4, 5):
        res[i] = _row_major_kv(res[i])
    return (y_prompt, y_sample) + tuple(res)
```

```python
import functools

import numpy as np
import jax
import jax.numpy as jnp
from jax import lax
from jax.experimental import pallas as pl
from jax.experimental.pallas import tpu as pltpu

F32 = jnp.float32
BF16 = jnp.bfloat16

D_MODEL = 1024
DEPTH = 4
PAGE_SIZE = 128
GDN_HEADS = 4
GDN_DK = 128
GDN_DV = 128
GDN_QKV = GDN_HEADS * (2 * GDN_DK + GDN_DV)
CONV_W = 4
NSA_HEADS = 8
NSA_KV_HEADS = 2
NSA_HD = 64
NSA_REP = NSA_HEADS // NSA_KV_HEADS
CMP_LEN = 32
CMP_STRIDE = 16
SEL_BLOCK = 64
N_SEL = 16
WINDOW = 512
NSA_KV_W = 2 * NSA_KV_HEADS * NSA_HD
D_FF = (8 * D_MODEL + 3 * 256 - 1) // (3 * 256) * 256
NEG_INF = -1e30
LOG2E = 1.4426950408889634
FORCE_BONUS = 1e4
EPS = 1e-6

LANE = 128
GDN_CHUNK = 128
GDN_NCH = 2
INV_BLOCK = 16
Q_BLOCK = 128
SLC_TILE = 512
WIN_TILES = WINDOW // Q_BLOCK + 1
SUB_W = CMP_STRIDE * NSA_KV_W
SUB_PP = PAGE_SIZE // CMP_STRIDE
POOL_SEQS = 2
SAMPLE_SEQS = 4
VMEM_LIMIT = 56 * 1024 * 1024

C_QKV, C_GATE, C_NQ, C_KV, C_SMALL = 0, 1536, 2048, 2560, 3328
N_PROJ = 3456
SM_B, SM_A, SM_G = 0, 4, 8

_NT = (((1,), (1,)), ((), ()))


def _params(sem):
    return pltpu.CompilerParams(dimension_semantics=sem, vmem_limit_bytes=VMEM_LIMIT)


def _dot(a, b):
    return jnp.dot(a.astype(BF16), b.astype(BF16), preferred_element_type=F32)


def _dot_nt(a, b):
    return lax.dot_general(a.astype(BF16), b.astype(BF16), _NT, preferred_element_type=F32)


def _split2(x):
    hi = x.astype(BF16)
    lo = (x - hi.astype(F32)).astype(BF16)
    return hi, lo


def _split3(x):
    hi = x.astype(BF16)
    r = x - hi.astype(F32)
    mid = r.astype(BF16)
    lo = (r - mid.astype(F32)).astype(BF16)
    return hi, mid, lo


def _mm_split(a, b):
    ah, al = a
    bh, bl = b
    d = functools.partial(jnp.dot, preferred_element_type=F32)
    return d(jnp.concatenate([ah, al], axis=1), jnp.concatenate([bh, bh], axis=0)) + d(ah, bl)


def _dotx(a, b):
    return _mm_split(_split2(a), _split2(b))


def _dot01_l(m01, x):
    m = m01.astype(BF16)
    hi, mid, lo = _split3(x)
    d = functools.partial(jnp.dot, preferred_element_type=F32)
    return d(jnp.concatenate([m, m], axis=1), jnp.concatenate([hi, mid], axis=0)) + d(m, lo)


def _dot01_r(x, m01):
    m = m01.astype(BF16)
    hi, mid, lo = _split3(x)
    d = functools.partial(jnp.dot, preferred_element_type=F32)
    return d(hi, m) + (d(mid, m) + d(lo, m))


def _iota(shape, dim):
    return lax.broadcasted_iota(jnp.int32, shape, dim)


def _idiv(x, d):
    sh = int(d).bit_length() - 1
    assert (1 << sh) == d
    return lax.shift_right_logical(x, jnp.int32(sh))


def _rep(x, rep):
    return x if rep == 1 else jnp.concatenate([x] * rep, axis=0)


def _rms_rows(x, g):
    return x * lax.rsqrt(jnp.mean(x * x, axis=-1, keepdims=True) + EPS) * g


def _proj_kernel(x_ref, g_ref, w_ref, *rest, with_t):
    if with_t:
        wt_ref = rest[0]
        rest = rest[1:]
    qkv_ref, gate_ref, nq_ref, cmp_ref, slc_ref, win_ref, sm_ref = rest[:7]
    h = _rms_rows(x_ref[...], g_ref[...]).astype(BF16)
    d = functools.partial(jnp.dot, preferred_element_type=F32)
    qkv_ref[...] = d(h, w_ref[:, C_QKV:C_GATE])
    gate_ref[...] = d(h, w_ref[:, C_GATE:C_NQ])
    nq_ref[...] = d(h, w_ref[:, C_NQ:C_KV])
    kv = d(h, w_ref[:, C_KV:C_SMALL])
    cmp_ref[...] = kv[:, 0:NSA_KV_W]
    slc_ref[...] = kv[:, NSA_KV_W:2 * NSA_KV_W]
    win_ref[...] = kv[:, 2 * NSA_KV_W:3 * NSA_KV_W]
    sm_ref[...] = d(h, w_ref[:, C_SMALL:N_PROJ])
    if with_t:
        cmpt_ref, slct_ref, wint_ref, nqt_ref, sk_ref, sv_ref, wk_ref, wv_ref = rest[7:]
        kvt = lax.dot_general(wt_ref[...], h, _NT, preferred_element_type=F32)
        cmpt_ref[0] = kvt[0:NSA_KV_W]
        slct = kvt[NSA_KV_W:2 * NSA_KV_W]
        wint = kvt[2 * NSA_KV_W:3 * NSA_KV_W]
        slct_ref[0] = slct
        wint_ref[0] = wint
        nqt_ref[0] = kvt[3 * NSA_KV_W:3 * NSA_KV_W + 512]
        half = NSA_KV_W // 2
        sk_ref[0, 0] = kv[:, NSA_KV_W:NSA_KV_W + half].astype(BF16)
        sv_ref[0, 0] = slct[half:].astype(BF16)
        for j in range(wk_ref.shape[1]):
            rows = slice(j * Q_BLOCK, (j + 1) * Q_BLOCK)
            wk_ref[0, j] = kv[rows, 2 * NSA_KV_W:2 * NSA_KV_W + half].astype(BF16)
            wv_ref[0, j] = wint[half:, rows].astype(BF16)


def _proj(x, g, w, wt, tm, batch=None):
    m = x.shape[0]
    tm = min(tm, m)
    with_t = wt is not None
    widths = (GDN_QKV, 512, 512, NSA_KV_W, NSA_KV_W, NSA_KV_W, LANE)
    in_specs = [pl.BlockSpec((tm, D_MODEL), lambda i: (i, 0)),
                pl.BlockSpec((1, D_MODEL), lambda i: (0, 0)),
                pl.BlockSpec((D_MODEL, N_PROJ), lambda i: (0, 0))]
    out_specs = [pl.BlockSpec((tm, wd), lambda i: (i, 0)) for wd in widths]
    out_shape = [jax.ShapeDtypeStruct((m, wd), F32) for wd in widths]
    args = [x, g, w]
    if with_t:
        t_len = m // batch
        nt = t_len // tm
        assert tm == SLC_TILE and t_len % tm == 0
        in_specs.append(pl.BlockSpec((wt.shape[0], D_MODEL), lambda i: (0, 0)))
        args.append(wt)
        tmap = lambda i: (i // nt, 0, i % nt)
        tile = lambda i: (i // nt, i % nt, 0, 0)
        half = NSA_KV_W // 2
        nwt = tm // Q_BLOCK
        out_specs += [pl.BlockSpec((1, NSA_KV_W, tm), tmap)] * 3 + [pl.BlockSpec((1, 512, tm), tmap)]
        out_shape += [jax.ShapeDtypeStruct((batch, NSA_KV_W, t_len), F32)] * 3
        out_shape += [jax.ShapeDtypeStruct((batch, 512, t_len), F32)]
        out_specs += [pl.BlockSpec((1, 1, tm, half), tile),
                      pl.BlockSpec((1, 1, half, tm), tile),
                      pl.BlockSpec((1, nwt, Q_BLOCK, half), tile),
                      pl.BlockSpec((1, nwt, half, Q_BLOCK), tile)]
        out_shape += [jax.ShapeDtypeStruct((batch, nt, tm, half), BF16),
                      jax.ShapeDtypeStruct((batch, nt, half, tm), BF16),
                      jax.ShapeDtypeStruct((batch, t_len // Q_BLOCK, Q_BLOCK, half), BF16),
                      jax.ShapeDtypeStruct((batch, t_len // Q_BLOCK, half, Q_BLOCK), BF16)]
    return pl.pallas_call(
        functools.partial(_proj_kernel, with_t=with_t),
        grid=(m // tm,),
        in_specs=in_specs,
        out_specs=out_specs,
        out_shape=out_shape,
        compiler_params=_params(("parallel",)),
    )(*args)


def _out_kernel(x_ref, a1_ref, a2_ref, w_ref, o_ref):
    half = w_ref.shape[0] // 2
    o_ref[...] = x_ref[...] + (_dot(a1_ref[...], w_ref[:half, :]) + _dot(a2_ref[...], w_ref[half:, :]))


def _out_proj(x, a1, a2, w, tm):
    m = x.shape[0]
    tm = min(tm, m)
    return pl.pallas_call(
        _out_kernel,
        grid=(m // tm,),
        in_specs=[pl.BlockSpec((tm, D_MODEL), lambda i: (i, 0)),
                  pl.BlockSpec((tm, 512), lambda i: (i, 0)),
                  pl.BlockSpec((tm, 512), lambda i: (i, 0)),
                  pl.BlockSpec((D_MODEL, D_MODEL), lambda i: (0, 0))],
        out_specs=pl.BlockSpec((tm, D_MODEL), lambda i: (i, 0)),
        out_shape=jax.ShapeDtypeStruct((m, D_MODEL), F32),
        compiler_params=_params(("parallel",)),
    )(x, a1, a2, w)


def _ffn_kernel(x_ref, g_ref, wg_ref, wu_ref, wo_ref, o_ref, h_scr, acc_scr):
    f = pl.program_id(1)

    @pl.when(f == 0)
    def _():
        h_scr[...] = _rms_rows(x_ref[...], g_ref[...]).astype(BF16)
        acc_scr[...] = jnp.zeros_like(acc_scr)

    h = h_scr[...]
    gt = jnp.dot(h, wg_ref[...], preferred_element_type=F32)
    up = jnp.dot(h, wu_ref[...], preferred_element_type=F32)
    act = (gt * jax.nn.sigmoid(gt)) * up
    acc_scr[...] += jnp.dot(act.astype(BF16), wo_ref[...], preferred_element_type=F32)

    @pl.when(f == pl.num_programs(1) - 1)
    def _():
        o_ref[...] = x_ref[...] + acc_scr[...]


def _ffn(x, g, w_in, w_out, tm):
    m = x.shape[0]
    tm = min(tm, m)
    nf = 2
    tf = D_FF // nf
    return pl.pallas_call(
        _ffn_kernel,
        grid=(m // tm, nf),
        in_specs=[pl.BlockSpec((tm, D_MODEL), lambda i, f: (i, 0)),
                  pl.BlockSpec((1, D_MODEL), lambda i, f: (0, 0)),
                  pl.BlockSpec((D_MODEL, tf), lambda i, f: (0, f)),
                  pl.BlockSpec((D_MODEL, tf), lambda i, f: (0, nf + f)),
                  pl.BlockSpec((tf, D_MODEL), lambda i, f: (f, 0))],
        out_specs=pl.BlockSpec((tm, D_MODEL), lambda i, f: (i, 0)),
        out_shape=jax.ShapeDtypeStruct((m, D_MODEL), F32),
        scratch_shapes=[pltpu.VMEM((tm, D_MODEL), BF16), pltpu.VMEM((tm, D_MODEL), F32)],
        compiler_params=_params(("parallel", "arbitrary")),
    )(x, g, w_in, w_in, w_out)


def _window_rows_kernel(c_ref, n_ref, o_ref):
    wb = c_ref.shape[-1]
    ts = n_ref.shape[-1]
    for j in range(c_ref.shape[0]):
        o_ref[j] = pltpu.roll(c_ref[j], wb - ts, 1)
        o_ref[j, :, wb - ts:wb] = n_ref[j]


def _window_rows(cache_t, new_t):
    n, f, wb = cache_t.shape
    ts = new_t.shape[-1]
    assert wb >= ts
    nb = 4 if n % 4 == 0 else 1
    return pl.pallas_call(
        _window_rows_kernel,
        grid=(n // nb,),
        in_specs=[pl.BlockSpec((nb, f, wb), lambda i: (i, 0, 0)),
                  pl.BlockSpec((nb, f, ts), lambda i: (i, 0, 0))],
        out_specs=pl.BlockSpec((nb, f, wb), lambda i: (i, 0, 0)),
        out_shape=jax.ShapeDtypeStruct((n, f, wb), F32),
        compiler_params=_params(("parallel",)),
    )(cache_t, new_t)


def _final_norm_kernel(x_ref, g_ref, o_ref):
    o_ref[...] = _rms_rows(x_ref[...], g_ref[...])


def _final_norm(x, g, tm):
    m = x.shape[0]
    tm = min(tm, m)
    return pl.pallas_call(
        _final_norm_kernel,
        grid=(m // tm,),
        in_specs=[pl.BlockSpec((tm, D_MODEL), lambda i: (i, 0)),
                  pl.BlockSpec((1, D_MODEL), lambda i: (0, 0))],
        out_specs=pl.BlockSpec((tm, D_MODEL), lambda i: (i, 0)),
        out_shape=jax.ShapeDtypeStruct((m, D_MODEL), F32),
        compiler_params=_params(("parallel",)),
    )(x, g)


def _prep_kernel(raw_ref, hist_ref, cw_ref, o_ref, ext_scr, *, tc):
    @pl.when(pl.program_id(1) == 0)
    def _():
        ext_scr[0:8, :] = hist_ref[0]

    ext_scr[8:8 + tc, :] = raw_ref[0]
    for s in range(GDN_QKV // LANE):
        o_ref[0, :, s * LANE:(s + 1) * LANE] = _conv_slab(ext_scr, cw_ref, tc, s)
    tail = ext_scr[tc:tc + 8, :]
    ext_scr[0:8, :] = tail


def _gdn_prep(raw, hist, cw, tc):
    b, t, _ = raw.shape
    return pl.pallas_call(
        functools.partial(_prep_kernel, tc=tc),
        grid=(b, t // tc),
        in_specs=[pl.BlockSpec((1, tc, GDN_QKV), lambda i, j: (i, j, 0)),
                  pl.BlockSpec((1, 8, GDN_QKV), lambda i, j: (i, 0, 0)),
                  pl.BlockSpec((8, GDN_QKV), lambda i, j: (0, 0))],
        out_specs=pl.BlockSpec((1, tc, GDN_QKV), lambda i, j: (i, j, 0)),
        out_shape=jax.ShapeDtypeStruct((b, t, GDN_QKV), F32),
        scratch_shapes=[pltpu.VMEM((tc + 8, GDN_QKV), F32)],
        compiler_params=_params(("parallel", "arbitrary")),
    )(raw, hist, cw)


def _tri_inv(mats, ri, ci):
    c = mats[0].shape[0]
    eye = (ri == ci).astype(F32)
    bd = _idiv(ri, INV_BLOCK) == _idiv(ci, INV_BLOCK)
    ad = [jnp.where(bd, a, 0.0) for a in mats]
    ao = [a - d for a, d in zip(mats, ad)]
    split = lambda xs: [_split2(x) for x in xs]
    mm = lambda xs, ys: [_mm_split(x, y) for x, y in zip(xs, ys)]
    pw_s = split(ad)
    td = [eye - d for d in ad]
    k = 2
    while k < INV_BLOCK:
        pw_s = split(mm(pw_s, pw_s))
        td = [t + x for t, x in zip(td, mm(split(td), pw_s))]
        k *= 2
    td_s = split(td)
    n = mm(td_s, split(ao))
    tn = [eye - x for x in n]
    pw_s = split(n)
    k = 2
    while k < c // INV_BLOCK:
        pw_s = split(mm(pw_s, pw_s))
        tn = [t + x for t, x in zip(tn, mm(split(tn), pw_s))]
        k *= 2
    return mm(split(tn), td_s)


def _softplus(x):
    return jnp.maximum(x, 0.0) + jnp.log1p(jnp.exp(-jnp.abs(x)))


def _conv_slab(ext_ref, cw_ref, rows, s):
    cols = slice(s * LANE, (s + 1) * LANE)
    y = jnp.zeros((rows, LANE), F32)
    for j in range(CONV_W):
        y = y + ext_ref[pl.ds(8 - (CONV_W - 1) + j, rows), cols] * cw_ref[j:j + 1, cols]
    y = y * jax.nn.sigmoid(y)
    if s < 2 * GDN_HEADS:
        y = y * lax.rsqrt(jnp.sum(y * y, axis=-1, keepdims=True) + EPS)
        if s < GDN_HEADS:
            y = y * (GDN_DK ** -0.5)
    return y


def _gdn_kernel(qkv_ref, gate_ref, sm_ref, alog_ref, dtb_ref, ng_ref, s0_ref, *rest, ls, nch, fused):
    c = GDN_CHUNK
    nseq = c // ls
    assert nseq == 1 or nch == 1
    heads = range(GDN_HEADS)
    pairs = [(ch, h) for ch in range(nch) for h in heads]
    if fused:
        hist_ref, cw_ref, o_ref, sout_ref, ext_scr = rest
    else:
        o_ref, sout_ref = rest

    @pl.when(pl.program_id(1) == 0)
    def _():
        sout_ref[...] = s0_ref[...]
        if fused:
            ext_scr[0:8, :] = hist_ref[0]

    if fused:
        rows = nch * c
        ext_scr[8:8 + rows, :] = qkv_ref[...]
        prepped = [_conv_slab(ext_scr, cw_ref, rows, s) for s in range(GDN_QKV // LANE)]
        tail = ext_scr[rows:rows + 8, :]
        ext_scr[0:8, :] = tail

    ri = _iota((c, c), 0)
    ci = _iota((c, c), 1)
    same = _idiv(ri, ls) == _idiv(ci, ls)
    lower = same & (ci <= ri)
    strict = same & (ci < ri)
    lo_m = lower.astype(F32)
    same_m = same.astype(F32)
    sm = [sm_ref[ch * c:(ch + 1) * c, :] for ch in range(nch)]

    def slab(ch, s):
        if fused:
            return prepped[s][ch * c:(ch + 1) * c]
        return qkv_ref[ch * c:(ch + 1) * c, s * LANE:(s + 1) * LANE]

    q = [slab(ch, h) for ch, h in pairs]
    k = [slab(ch, GDN_HEADS + h) for ch, h in pairs]
    v = [slab(ch, 2 * GDN_HEADS + h) for ch, h in pairs]
    neg_a = -jnp.exp(alog_ref[0:1, :])
    beta_s = [jax.nn.sigmoid(x) for x in sm]
    g_s = [neg_a * _softplus(x + dtb_ref[0:1, :]) for x in sm]
    beta = [jnp.broadcast_to(beta_s[ch][:, SM_B + h:SM_B + h + 1], (c, LANE)) for ch, h in pairs]
    g_col = [jnp.broadcast_to(g_s[ch][:, SM_A + h:SM_A + h + 1], (c, LANE)) for ch, h in pairs]
    n = range(len(pairs))
    lo_m = lo_m.astype(BF16)
    same_m = same_m.astype(BF16)
    dcy = [_dot01_l(lo_m, g_col[i]) for i in n]
    dcy_row = [x.T for x in dcy]
    if nseq == 1:
        dtot = [jnp.broadcast_to(x[c - 1:c, :], (c, LANE)) for x in dcy]
    else:
        dtot = [_dot01_l(same_m, g_col[i]) for i in n]
    dm = [jnp.where(lower, jnp.exp(jnp.where(lower, dcy[i] - dcy_row[i], 0.0)), 0.0) for i in n]
    kb = [k[i] * beta[i] for i in n]
    kk = [_dot_nt(jnp.concatenate([kb[i], q[i]], axis=0), k[i]) for i in n]
    a_mat = [jnp.where(strict, kk[i][0:c] * dm[i], 0.0) for i in n]
    attn = [jnp.where(lower, kk[i][c:2 * c] * dm[i], 0.0) for i in n]
    edc = [jnp.exp(dcy[i]) for i in n]
    t_inv = _tri_inv(a_mat, ri, ci)
    uw = [_dotx(t_inv[i], jnp.concatenate([v[i] * beta[i], kb[i] * edc[i]], axis=1)) for i in n]
    u = [x[:, 0:LANE] for x in uw]
    w = [x[:, LANE:2 * LANE] for x in uw]
    qd = [q[i] * edc[i] for i in n]
    kdt = [(k[i] * jnp.exp(dtot[i] - dcy[i])).T for i in n]
    gl = [jnp.exp(dtot[i]) for i in n]
    o_all = []
    if nseq == 1:
        s_cur = [sout_ref[0, h] for h in heads]
        for ch in range(nch):
            ix = [ch * GDN_HEADS + h for h in heads]
            ws = [_dot(jnp.concatenate([w[i], qd[i]], axis=0), s_cur[h]) for h, i in zip(heads, ix)]
            v_new = [u[i] - ws[h][0:c] for h, i in zip(heads, ix)]
            o_all += [ws[h][c:2 * c] + _dot(attn[i], v_new[h]) for h, i in zip(heads, ix)]
            s_cur = [s_cur[h] * gl[i][0:1, :] + _dot(kdt[i], v_new[h]) for h, i in zip(heads, ix)]
        for h in heads:
            sout_ref[0, h] = s_cur[h]
    else:
        for h in heads:
            vn, oq = [], []
            for s in range(nseq):
                s_old = sout_ref[s, h]
                rows = slice(s * ls, (s + 1) * ls)
                vn.append(u[h][rows] - _dot(w[h][rows], s_old))
                oq.append(_dot(qd[h][rows], s_old))
            v_new = jnp.concatenate(vn, axis=0)
            o_all.append(jnp.concatenate(oq, axis=0) + _dot(attn[h], v_new))
            for s in range(nseq):
                kdt_s = jnp.where(_idiv(ci, ls) == s, kdt[h], 0.0)
                sout_ref[s, h] = sout_ref[s, h] * gl[h][s * ls:s * ls + 1, :] + _dot(kdt_s, v_new)
    for i, (ch, h) in enumerate(pairs):
        o = o_all[i]
        o = o * lax.rsqrt(jnp.mean(o * o, axis=-1, keepdims=True) + EPS) * ng_ref[...]
        gt = gate_ref[ch * c:(ch + 1) * c, h * LANE:(h + 1) * LANE]
        o_ref[ch * c:(ch + 1) * c, h * LANE:(h + 1) * LANE] = o * (gt * jax.nn.sigmoid(gt))


def _gdn(qkv, gate, sm, alog_b, dtb_b, ng, s0, s0_off, nb, nt, ls, nch, prep=None):
    m = qkv.shape[0]
    c = GDN_CHUNK * nch
    nseq = GDN_CHUNK // ls
    row = lambda i, j: (i * nt + j, 0)
    n_state = nb * nseq
    in_specs = [pl.BlockSpec((c, GDN_QKV), row),
                pl.BlockSpec((c, 512), row),
                pl.BlockSpec((c, LANE), row),
                pl.BlockSpec((8, LANE), lambda i, j: (0, 0)),
                pl.BlockSpec((8, LANE), lambda i, j: (0, 0)),
                pl.BlockSpec((1, LANE), lambda i, j: (0, 0)),
                pl.BlockSpec((nseq, GDN_HEADS, GDN_DK, GDN_DV), lambda i, j: (s0_off + i, 0, 0, 0))]
    args = [qkv, gate, sm, alog_b, dtb_b, ng, s0]
    scratch = []
    if prep is not None:
        in_specs += [pl.BlockSpec((1, 8, GDN_QKV), lambda i, j: (i, 0, 0)),
                     pl.BlockSpec((8, GDN_QKV), lambda i, j: (0, 0))]
        args += list(prep)
        scratch = [pltpu.VMEM((c + 8, GDN_QKV), F32)]
    return pl.pallas_call(
        functools.partial(_gdn_kernel, ls=ls, nch=nch, fused=prep is not None),
        grid=(nb, nt),
        in_specs=in_specs,
        out_specs=[pl.BlockSpec((c, 512), row),
                   pl.BlockSpec((nseq, GDN_HEADS, GDN_DK, GDN_DV), lambda i, j: (i, 0, 0, 0))],
        out_shape=[jax.ShapeDtypeStruct((m, 512), F32),
                   jax.ShapeDtypeStruct((n_state, GDN_HEADS, GDN_DK, GDN_DV), F32)],
        scratch_shapes=scratch,
        compiler_params=_params(("parallel", "arbitrary")),
    )(*args)


def _cmp_kernel(x_ref, pe_ref, wlo_ref, whi_ref, y_ref):
    x = x_ref[0]
    y_ref[0, :, 0:NSA_KV_W] = _dot(x + pe_ref[0, 0:1, :], wlo_ref[0])
    y_ref[0, :, NSA_KV_W:2 * NSA_KV_W] = _dot(x + pe_ref[0, 1:2, :], whi_ref[0])


def _compress(x, pe, wlo, whi, tr):
    nl, r, _ = x.shape
    tr = min(tr, r)
    return pl.pallas_call(
        _cmp_kernel,
        grid=(nl, r // tr),
        in_specs=[pl.BlockSpec((1, tr, SUB_W), lambda l, i: (l, i, 0)),
                  pl.BlockSpec((1, 2, SUB_W), lambda l, i: (l, 0, 0)),
                  pl.BlockSpec((1, SUB_W, NSA_KV_W), lambda l, i: (l, 0, 0)),
                  pl.BlockSpec((1, SUB_W, NSA_KV_W), lambda l, i: (l, 0, 0))],
        out_specs=pl.BlockSpec((1, tr, 2 * NSA_KV_W), lambda l, i: (l, i, 0)),
        out_shape=jax.ShapeDtypeStruct((nl, r, 2 * NSA_KV_W), F32),
        compiler_params=_params(("parallel", "parallel")),
    )(x, pe, wlo, whi)


def _cmp_pool_kernel(pt_ref, pef_ref, wcat_ref, *rest, npg):
    del pt_ref
    pages = rest[:npg]
    y_ref, x_scr, pb_scr = rest[npg:]

    @pl.when(pl.program_id(1) == 0)
    def _():
        for hf in range(2):
            wf = wcat_ref[0, :, hf].reshape(CMP_STRIDE * LANE, 2 * LANE)
            pb_scr[hf] = _dot(pef_ref[0, hf], wf)

    for j in range(npg):
        x = pages[j][0].T
        for hf in range(2):
            x_scr[hf, j * PAGE_SIZE:(j + 1) * PAGE_SIZE, :] = x[:, hf * LANE:(hf + 1) * LANE]
    n = npg * SUB_PP
    for hf in range(2):
        acc = jnp.zeros((n, 2 * LANE), F32)
        for l in range(0, CMP_STRIDE, 2):
            xl = jnp.concatenate([x_scr[hf, pl.ds(l + j, n, stride=CMP_STRIDE), :] for j in range(2)], axis=1)
            acc = acc + _dot(xl, wcat_ref[0, l:l + 2, hf].reshape(2 * LANE, 2 * LANE))
        y_ref[0, :, hf * LANE:(hf + 1) * LANE] = acc[:, 0:LANE] + pb_scr[hf, 0:1, 0:LANE]
        y_ref[0, :, NSA_KV_W + hf * LANE:NSA_KV_W + (hf + 1) * LANE] = acc[:, LANE:] + pb_scr[hf, 1:2, LANE:]


def _compress_pool(page_table, pages_t, pe, wlo, whi, nl, n_pool, seq_per_step):
    db, n_pages = page_table.shape
    assert db % seq_per_step == 0
    npg = seq_per_step * n_pages
    n = npg * SUB_PP
    halves = [slice(hf * LANE, (hf + 1) * LANE) for hf in range(2)]
    wcat = jnp.stack([jnp.concatenate([wlo[:, :, h, h], whi[:, :, h, h]], axis=-1) for h in halves], axis=2)
    pef = jnp.stack([pe[:, :, :, h].reshape(nl, 2, CMP_STRIDE * LANE) for h in halves], axis=1)
    pef = jnp.pad(pef, ((0, 0), (0, 0), (0, 6), (0, 0)))

    def page_map(k):
        s, p = divmod(k, n_pages)
        return lambda l, i, pt: (l * n_pool + pt[i * seq_per_step + s, p], 0, 0)

    in_specs = [pl.BlockSpec((1, 2, 8, CMP_STRIDE * LANE), lambda l, i, pt: (l, 0, 0, 0)),
                pl.BlockSpec((1, CMP_STRIDE, 2, LANE, 2 * LANE), lambda l, i, pt: (l, 0, 0, 0, 0))]
    in_specs += [pl.BlockSpec((1, NSA_KV_W, PAGE_SIZE), page_map(k)) for k in range(npg)]
    grid_spec = pltpu.PrefetchScalarGridSpec(
        num_scalar_prefetch=1,
        grid=(nl, db // seq_per_step),
        in_specs=in_specs,
        out_specs=pl.BlockSpec((1, n, 2 * NSA_KV_W), lambda l, i, pt: (l, i, 0)),
        scratch_shapes=[pltpu.VMEM((2, npg * PAGE_SIZE, LANE), F32),
                        pltpu.VMEM((2, 8, 2 * LANE), F32)])
    return pl.pallas_call(
        functools.partial(_cmp_pool_kernel, npg=npg),
        grid_spec=grid_spec,
        out_shape=jax.ShapeDtypeStruct((nl, db * n_pages * SUB_PP, 2 * NSA_KV_W), F32),
        compiler_params=_params(("parallel", "arbitrary")),
    )(page_table, pef, wcat, *([pages_t] * npg))


def _slope(h):
    return 2.0 ** (-8.0 * (h + 1) / NSA_HEADS)


def _make_units(q, stacked):
    qb = q.shape[0]
    units = []
    for g in range(NSA_KV_HEADS):
        hs = range(g * NSA_REP, (g + 1) * NSA_REP)
        slabs = [q[:, h * NSA_HD:(h + 1) * NSA_HD] * (NSA_HD ** -0.5) for h in hs]
        if stacked:
            rows = jnp.concatenate(slabs, axis=0).astype(BF16)
            slope = jnp.concatenate([jnp.full((qb, 1), _slope(h), F32) for h in hs], axis=0)
            units.append([(rows, slope, NSA_REP)])
        else:
            units.append([(s.astype(BF16), _slope(h), 1) for s, h in zip(slabs, hs)])
    return units


def _cmp_branch(units, pq, ckv, ncp):
    qb = pq.shape[0]
    n_ix = _iota((1, ncp), 1)
    c_end = n_ix * CMP_STRIDE + (CMP_LEN - 1)
    c_ctr = (n_ix * CMP_STRIDE).astype(F32) + 0.5 * (CMP_LEN - 1)
    pool = (_idiv(_iota((ncp, LANE), 0), SEL_BLOCK // CMP_STRIDE) == _iota((ncp, LANE), 1)).astype(F32)
    ckv_b = ckv.astype(BF16)
    geo = {}
    for g in range(NSA_KV_HEADS):
        for _, _, rep in units[g]:
            if rep not in geo:
                pqr = _rep(pq, rep)
                geo[rep] = (c_end <= pqr, pqr.astype(F32) - c_ctr)
    flat = [(g, rows, slope, rep) for g in range(NSA_KV_HEADS) for rows, slope, rep in units[g]]
    kc = [ckv_b[:, g * NSA_HD:(g + 1) * NSA_HD] for g in range(NSA_KV_HEADS)]
    vc = [ckv_b[:, (NSA_KV_HEADS + g) * NSA_HD:(NSA_KV_HEADS + g + 1) * NSA_HD] for g in range(NSA_KV_HEADS)]
    s = [_dot_nt(rows, kc[g]) for g, rows, _, _ in flat]
    s = [jnp.where(geo[rep][0], si - slope * geo[rep][1], NEG_INF) for si, (_, _, slope, rep) in zip(s, flat)]
    m = [jnp.max(si, axis=-1, keepdims=True) for si in s]
    e = [jnp.where(geo[rep][0], jnp.exp(si - mi), 0.0) for si, mi, (_, _, _, rep) in zip(s, m, flat)]
    den = [jnp.sum(ei, axis=-1, keepdims=True) for ei in e]
    p = [ei / jnp.where(di > 0.0, di, 1.0) for ei, di in zip(e, den)]
    o = [_dot(pi, vc[g]) for pi, (g, _, _, _) in zip(p, flat)]
    outs, imps = [], []
    for g in range(NSA_KV_HEADS):
        psum = jnp.zeros((qb, ncp), F32)
        og = []
        for pi, oi, (gi, _, _, rep) in zip(p, o, flat):
            if gi == g:
                og.append(oi)
                for r in range(rep):
                    psum = psum + pi[r * qb:(r + 1) * qb]
        outs.append(og)
        imps.append(psum)
    imps = [_dot01_r(ps, pool) for ps in imps]
    return outs, imps


def _select_blocks(imps, pq, ax):
    blk = _iota(imps[0].shape, ax)
    cur = _idiv(pq, SEL_BLOCK)
    forced = (blk == 0) | (blk == cur) | (blk == cur - 1)
    bonus = jnp.where(forced, FORCE_BONUS, 0.0)
    work = [jnp.where(blk <= cur, imp + bonus, NEG_INF) for imp in imps]
    idx = blk.astype(F32)
    sel = [jnp.zeros(w.shape, F32) for w in work]
    for _ in range(N_SEL):
        m = [jnp.max(w, axis=ax, keepdims=True) for w in work]
        first = [jnp.min(jnp.where(w == mi, idx, 2.0 * LANE), axis=ax, keepdims=True) for w, mi in zip(work, m)]
        hit = [idx == f for f in first]
        sel = [jnp.where(h, 1.0, s) for h, s in zip(hit, sel)]
        work = [jnp.where(h, -jnp.inf, w) for h, w in zip(hit, work)]
    return sel


def _select_blocks_rank(imps, pq, nb):
    qb = imps[0].shape[0]
    blk = _iota((qb, LANE), 1)
    cur = _idiv(pq, SEL_BLOCK)
    forced = (blk == 0) | (blk == cur) | (blk == cur - 1)
    bonus = jnp.where(forced, FORCE_BONUS, 0.0)
    score = [jnp.where(blk <= cur, imp + bonus, NEG_INF) for imp in imps]
    fill = jnp.full((LANE - len(imps) * qb, LANE), NEG_INF, F32)
    st = jnp.concatenate(score + [fill], axis=0).T[0:nb]
    jrow = _iota((nb, LANE), 0)
    cnt = jnp.zeros((nb, LANE), F32)
    for j in range(nb):
        row = st[j:j + 1, :]
        cnt = cnt + jnp.where(jrow > j, jnp.where(row >= st, 1.0, 0.0), jnp.where(row > st, 1.0, 0.0))
    sel_t = jnp.where(cnt < N_SEL, 1.0, 0.0)
    sel = jnp.concatenate([sel_t, jnp.zeros((LANE - nb, LANE), F32)], axis=0).T
    return [sel[g * qb:(g + 1) * qb] for g in range(len(imps))]


def _slc_tile(units, sel, pq, kvt, k0, tk, causal, flags, m_scr, l_scr, acc_scr):
    key = k0 + _iota((1, tk), 1)
    dist_i = pq - key
    dist = dist_i.astype(F32)
    e_tile = (_iota((LANE, tk), 0) == _idiv(k0 + _iota((LANE, tk), 1), SEL_BLOCK)).astype(BF16)
    ui = 0
    for g in range(NSA_KV_HEADS):
        def group(g=g, ui=ui):
            on = jnp.dot(sel[g].astype(BF16), e_tile, preferred_element_type=F32) > 0.5
            if causal:
                on = on & (dist_i >= 0)
            bias = jnp.where(on, 0.0, NEG_INF)
            kt = kvt[g * NSA_HD:(g + 1) * NSA_HD, :]
            vt = kvt[(NSA_KV_HEADS + g) * NSA_HD:(NSA_KV_HEADS + g + 1) * NSA_HD, :]
            us = units[g]
            ids = [ui + j for j in range(len(us))]
            s = [jnp.dot(rows, kt, preferred_element_type=F32) for rows, _, _ in us]
            s = [si + (_rep(bias, rep) - slope * _rep(dist, rep)) for si, (_, slope, rep) in zip(s, us)]
            m_prev = [m_scr[u][:, 0:1] for u in ids]
            m_new = [jnp.maximum(mp, jnp.max(si, axis=-1, keepdims=True)) for mp, si in zip(m_prev, s)]
            alpha = [jnp.exp(mp - mn) for mp, mn in zip(m_prev, m_new)]
            p = [jnp.exp(si - mn) for si, mn in zip(s, m_new)]
            l_new = [a * l_scr[u][:, 0:1] + jnp.sum(pi, axis=-1, keepdims=True) for a, u, pi in zip(alpha, ids, p)]
            pv = [lax.dot_general(pi.astype(BF16), vt, _NT, preferred_element_type=F32) for pi in p]
            for u, a, pvi, mn, ln in zip(ids, alpha, pv, m_new, l_new):
                acc_scr[u] = a * acc_scr[u] + pvi
                m_scr[u] = jnp.broadcast_to(mn, m_scr.shape[1:])
                l_scr[u] = jnp.broadcast_to(ln, l_scr.shape[1:])

        if flags is None:
            group()
        else:
            pl.when(flags[g])(group)
        ui += len(units[g])


def _win_branch(units, pq, kvt, w0):
    wk = kvt.shape[1]
    dist_i = pq - (w0 + _iota((1, wk), 1))
    dist = dist_i.astype(F32)
    bias = jnp.where((dist_i >= 0) & (dist_i < WINDOW), 0.0, NEG_INF)
    flat = [(g, rows, slope, rep) for g in range(NSA_KV_HEADS) for rows, slope, rep in units[g]]
    kt = [kvt[g * NSA_HD:(g + 1) * NSA_HD, :] for g in range(NSA_KV_HEADS)]
    vt = [kvt[(NSA_KV_HEADS + g) * NSA_HD:(NSA_KV_HEADS + g + 1) * NSA_HD, :] for g in range(NSA_KV_HEADS)]
    s = [jnp.dot(rows, kt[g], preferred_element_type=F32) for g, rows, _, _ in flat]
    s = [si + (_rep(bias, rep) - slope * _rep(dist, rep)) for si, (_, _, slope, rep) in zip(s, flat)]
    m = [jnp.max(si, axis=-1, keepdims=True) for si in s]
    p = [jnp.exp(si - mi) for si, mi in zip(s, m)]
    den = [jnp.sum(pi, axis=-1, keepdims=True) for pi in p]
    pv = [lax.dot_general(pi.astype(BF16), vt[g], _NT, preferred_element_type=F32) for pi, (g, _, _, _) in zip(p, flat)]
    o = [pvi / di for pvi, di in zip(pv, den)]
    return [[oi for oi, (gi, _, _, _) in zip(o, flat) if gi == g] for g in range(NSA_KV_HEADS)]


def _nsa_finish(units, o_c, o_w, qb, sm, ng_ref, o_ref, l_scr, acc_scr):
    gates = jax.nn.sigmoid(sm)
    per_group = len(units[0])
    for h in range(NSA_HEADS):
        g, r = divmod(h, NSA_REP)
        if per_group == 1:
            rows = slice(r * qb, (r + 1) * qb)
            u, j = g, 0
        else:
            rows = slice(0, qb)
            u, j = h, r
        oc = o_c[g][j][rows]
        ow = o_w[g][j][rows]
        o_s = acc_scr[u][rows] / l_scr[u][rows][:, 0:1]
        c0 = SM_G + 3 * h
        o = gates[:, c0:c0 + 1] * oc + gates[:, c0 + 1:c0 + 2] * o_s + gates[:, c0 + 2:c0 + 3] * ow
        o = o * lax.rsqrt(jnp.mean(o * o, axis=-1, keepdims=True) + EPS) * ng_ref[...]
        o_ref[:, h * NSA_HD:(h + 1) * NSA_HD] = o


def _slc_init(m_scr, l_scr, acc_scr):
    m_scr[...] = jnp.full(m_scr.shape, NEG_INF, F32)
    l_scr[...] = jnp.zeros(l_scr.shape, F32)
    acc_scr[...] = jnp.zeros(acc_scr.shape, F32)


def _key_aug(t_col, lane, qblk):
    rel = (_idiv(t_col, SEL_BLOCK) - qblk).astype(F32)
    off = (t_col & (SEL_BLOCK - 1)).astype(F32)
    return jnp.where(lane < 4, jnp.where((lane & 1) == 0, rel, off), 0.0).astype(BF16)


def _nsa_prompt_kernel(qt_ref, smt_ref, y_ref, sk_ref, sv_ref, wk_ref, wv_ref, ngb_ref, o_ref,
                       kc_scr, vct_scr, selb_scr, m_scr, l_scr, acc_scr, *, t_len):
    i = pl.program_id(1)
    nsub = t_len // CMP_STRIDE
    gw = NSA_REP * Q_BLOCK
    half = NSA_KV_W // 2
    groups = range(NSA_KV_HEADS)

    @pl.when(i == 0)
    def _():
        y = y_ref[0]
        ckv = y[:, 0:NSA_KV_W] + pltpu.roll(y[:, NSA_KV_W:2 * NSA_KV_W], nsub - 1, 0)
        kc_scr[...] = ckv[:, 0:half].astype(BF16)
        vct_scr[...] = ckv[:, half:].T.astype(BF16)

    pos0 = i * Q_BLOCK
    qblk = pos0 // SEL_BLOCK
    pq_row = pos0 + _iota((1, Q_BLOCK), 1)
    pq_g = jnp.concatenate([pq_row] * NSA_REP, axis=1)
    qt = qt_ref[0] * (NSA_HD ** -0.5 * LOG2E)
    zeros = jnp.zeros((NSA_HD, gw), F32)
    slope_g, rq, rfull = [], [], []
    aug_row = _iota((LANE, gw), 0)
    for g in groups:
        hs = range(g * NSA_REP, (g + 1) * NSA_REP)
        qg = jnp.concatenate([qt[h * NSA_HD:(h + 1) * NSA_HD, :] for h in hs], axis=1)
        rq.append(jnp.concatenate([qg, zeros] if g == 0 else [zeros, qg], axis=0).astype(BF16))
        sl = jnp.concatenate([jnp.full((1, Q_BLOCK), _slope(h) * LOG2E, F32) for h in hs], axis=1)
        slope_g.append(sl)
        sl_hi = sl.astype(BF16).astype(F32)
        sl_p = jnp.where(aug_row < 2, sl_hi, sl - sl_hi)
        raug = jnp.where(aug_row < 4, jnp.where((aug_row & 1) == 0, SEL_BLOCK * sl_p, sl_p), 0.0).astype(BF16)
        rfull.append(jnp.concatenate([rq[g], raug], axis=0))

    n_col = _iota((nsub, gw), 0)
    ok = (n_col * CMP_STRIDE + (CMP_LEN - 1)) <= pq_g
    dist = pq_g.astype(F32) - ((n_col * CMP_STRIDE).astype(F32) + 0.5 * (CMP_LEN - 1))
    kc = kc_scr[...]
    s = [jnp.dot(kc, rq[g], preferred_element_type=F32) for g in groups]
    s = [jnp.where(ok, s[g] - slope_g[g] * dist, NEG_INF) for g in groups]
    m = [jnp.max(x, axis=0, keepdims=True) for x in s]
    e = [jnp.where(ok, jnp.exp2(s[g] - m[g]), 0.0) for g in groups]
    den = [jnp.sum(x, axis=0, keepdims=True) for x in e]
    p = [e[g] / jnp.where(den[g] > 0.0, den[g], 1.0) for g in groups]
    o_c = [jnp.dot(vct_scr[g * NSA_HD:(g + 1) * NSA_HD, :], p[g].astype(BF16), preferred_element_type=F32)
           for g in groups]
    psum = [sum(p[g][:, r * Q_BLOCK:(r + 1) * Q_BLOCK] for r in range(NSA_REP)) for g in groups]
    pool_t = (_iota((LANE, nsub), 0) == _idiv(_iota((LANE, nsub), 1), SEL_BLOCK // CMP_STRIDE)).astype(F32)
    imp_t = [_dot01_l(pool_t, psum[g]) for g in groups]

    sel = _select_blocks(imp_t, pq_row, 0)
    for g in groups:
        selb_scr[g] = jnp.where(sel[g] > 0.5, 0.0, NEG_INF)
    used = [jnp.max(x, axis=1, keepdims=True) for x in sel]
    tile_of_blk = _idiv(_iota((LANE, 1), 0), SLC_TILE // SEL_BLOCK)

    m_scr[...] = jnp.full(m_scr.shape, NEG_INF, F32)
    l_scr[...] = jnp.zeros(l_scr.shape, F32)
    acc_scr[...] = jnp.zeros(acc_scr.shape, F32)
    n_tiles = (pos0 + Q_BLOCK + SLC_TILE - 1) // SLC_TILE
    blk_per_tile = SLC_TILE // SEL_BLOCK

    def slc_tile(kt, causal):
        k0 = kt * SLC_TILE
        t_col = k0 + _iota((SLC_TILE, LANE), 0)
        lhs = jnp.concatenate([sk_ref[0, kt], _key_aug(t_col, _iota((SLC_TILE, LANE), 1), qblk)], axis=1)
        bias = [jnp.concatenate(
            [jnp.broadcast_to(selb_scr[g, pl.ds(kt * blk_per_tile + j, 1), :], (SEL_BLOCK, Q_BLOCK))
             for j in range(blk_per_tile)], axis=0) for g in groups]
        if causal:
            bias = [jnp.where(t_col <= pq_row, b, NEG_INF) for b in bias]
        sc = [jnp.dot(lhs, rfull[g], preferred_element_type=F32) + jnp.concatenate([bias[g]] * NSA_REP, axis=1)
              for g in groups]
        m_prev = [m_scr[g, 0:1, :] for g in groups]
        m_new = [jnp.maximum(m_prev[g], jnp.max(sc[g], axis=0, keepdims=True)) for g in groups]
        alpha = [jnp.exp2(m_prev[g] - m_new[g]) for g in groups]
        pr = [jnp.exp2(sc[g] - m_new[g]) for g in groups]
        l_new = [alpha[g] * l_scr[g, 0:1, :] + jnp.sum(pr[g], axis=0, keepdims=True) for g in groups]
        pv = [jnp.dot(sv_ref[0, kt, g * NSA_HD:(g + 1) * NSA_HD, :], pr[g].astype(BF16),
                      preferred_element_type=F32) for g in groups]
        for g in groups:
            acc_scr[g] = alpha[g] * acc_scr[g] + pv[g]
            l_scr[g, 0:1, :] = l_new[g]
            m_scr[g, 0:1, :] = m_new[g]

    used_any = jnp.maximum(used[0], used[1])

    def body(kt, carry):
        @pl.when(jnp.max(jnp.where(tile_of_blk == kt, used_any, 0.0)) > 0.5)
        def _():
            slc_tile(kt, False)
        return carry

    lax.fori_loop(0, n_tiles - 1, body, 0)
    slc_tile(n_tiles - 1, True)

    wt0 = jnp.maximum(i - (WIN_TILES - 1), 0)
    wkeys = WIN_TILES * Q_BLOCK
    kw = jnp.concatenate([wk_ref[0, wt0 + j] for j in range(WIN_TILES)], axis=0)
    vw = jnp.concatenate([wv_ref[0, wt0 + j] for j in range(WIN_TILES)], axis=1)
    tw = wt0 * Q_BLOCK + _iota((wkeys, LANE), 0)
    lhs_w = jnp.concatenate([kw, _key_aug(tw, _iota((wkeys, LANE), 1), qblk)], axis=1)
    dist_w = pq_row - tw
    bias_w = jnp.where((dist_w >= 0) & (dist_w < WINDOW), 0.0, NEG_INF)
    bias_w = jnp.concatenate([bias_w] * NSA_REP, axis=1)
    sw = [jnp.dot(lhs_w, rfull[g], preferred_element_type=F32) + bias_w for g in groups]
    mw = [jnp.max(x, axis=0, keepdims=True) for x in sw]
    pw = [jnp.exp2(sw[g] - mw[g]) for g in groups]
    dw = [jnp.sum(x, axis=0, keepdims=True) for x in pw]
    o_w = [jnp.dot(vw[g * NSA_HD:(g + 1) * NSA_HD, :], pw[g].astype(BF16), preferred_element_type=F32) / dw[g]
           for g in groups]

    gates = jax.nn.sigmoid(smt_ref[0])
    outs = []
    for h in range(NSA_HEADS):
        g, r = divmod(h, NSA_REP)
        lanes = slice(r * Q_BLOCK, (r + 1) * Q_BLOCK)
        o_s = acc_scr[g][:, lanes] / l_scr[g, 0:1, lanes]
        c0 = SM_G + 3 * h
        o = gates[c0:c0 + 1] * o_c[g][:, lanes] + gates[c0 + 1:c0 + 2] * o_s + gates[c0 + 2:c0 + 3] * o_w[g][:, lanes]
        outs.append(o * lax.rsqrt(jnp.mean(o * o, axis=0, keepdims=True) + EPS) * ngb_ref[...])
    o_ref[...] = jnp.concatenate(outs, axis=0).T


def _nsa_prompt(nq_t, smt, y, sk, sv, wk, wv, ngb, b, t_len):
    nqb = t_len // Q_BLOCK
    nsub = t_len // CMP_STRIDE
    nst = t_len // SLC_TILE
    half = NSA_KV_W // 2
    gw = NSA_REP * Q_BLOCK
    whole = lambda bi, i: (bi, 0, 0, 0)
    return pl.pallas_call(
        functools.partial(_nsa_prompt_kernel, t_len=t_len),
        grid=(b, nqb),
        in_specs=[pl.BlockSpec((1, 512, Q_BLOCK), lambda bi, i: (bi, 0, i)),
                  pl.BlockSpec((1, 32, Q_BLOCK), lambda bi, i: (bi * nqb + i, 0, 0)),
                  pl.BlockSpec((1, nsub, 2 * NSA_KV_W), lambda bi, i: (bi, 0, 0)),
                  pl.BlockSpec((1, nst, SLC_TILE, half), whole),
                  pl.BlockSpec((1, nst, half, SLC_TILE), whole),
                  pl.BlockSpec((1, nqb, Q_BLOCK, half), whole),
                  pl.BlockSpec((1, nqb, half, Q_BLOCK), whole),
                  pl.BlockSpec((NSA_HD, Q_BLOCK), lambda bi, i: (0, 0))],
        out_specs=pl.BlockSpec((Q_BLOCK, 512), lambda bi, i: (bi * nqb + i, 0)),
        out_shape=jax.ShapeDtypeStruct((b * t_len, 512), F32),
        scratch_shapes=[pltpu.VMEM((nsub, half), BF16),
                        pltpu.VMEM((half, nsub), BF16),
                        pltpu.VMEM((NSA_KV_HEADS, LANE, Q_BLOCK), F32),
                        pltpu.VMEM((NSA_KV_HEADS, 8, gw), F32),
                        pltpu.VMEM((NSA_KV_HEADS, 8, gw), F32),
                        pltpu.VMEM((NSA_KV_HEADS, NSA_HD, gw), F32)],
        compiler_params=_params(("parallel", "arbitrary")),
    )(nq_t, smt, y, sk, sv, wk, wv, ngb)


def _softmax_rows(s):
    m = jnp.max(s, axis=-1, keepdims=True)
    p = jnp.exp(s - m)
    return p, jnp.sum(p, axis=-1, keepdims=True)


def _nsa_sample_kernel(pt_ref, q_ref, sm_ref, ypast_ref, ynew_ref, slcn_ref, winn_ref, winc_ref, ng_ref, *rest,
                       n_pages, past_len, dec_len, nseq):
    del pt_ref
    s_pages = rest[0:nseq * n_pages]
    o_ref = rest[nseq * n_pages]
    slc_scr, win_scr = rest[nseq * n_pages + 1:]
    ncp = n_pages * SUB_PP
    tk = slc_scr.shape[2]
    wk = win_scr.shape[2]
    wb = winc_ref.shape[2]
    qb = dec_len
    rows = NSA_REP * qb
    seqs = range(nseq)
    chains = [(s, g) for s in seqs for g in range(NSA_KV_HEADS)]
    tile4 = lambda x: jnp.concatenate([x] * NSA_REP, axis=0)

    def new_cols(ref, s):
        pad = jnp.zeros((PAGE_SIZE - qb, NSA_KV_W), F32)
        return jnp.concatenate([ref[s * qb:(s + 1) * qb, :], pad], axis=0).T.astype(BF16)

    ckv = []
    for s in seqs:
        y = jnp.concatenate([ypast_ref[s], ynew_ref[s]], axis=0)
        ckv.append((y[:, 0:NSA_KV_W] + pltpu.roll(y[:, NSA_KV_W:2 * NSA_KV_W], ncp + 7, 0))[0:ncp].astype(BF16))
        for p in range(n_pages):
            slc_scr[s, :, p * PAGE_SIZE:(p + 1) * PAGE_SIZE] = s_pages[s * n_pages + p][0].astype(BF16)
        slc_scr[s, :, past_len:tk] = new_cols(slcn_ref, s)
        win_scr[s, :, 0:wb] = winc_ref[s].astype(BF16)
        win_scr[s, :, wb:wk] = new_cols(winn_ref, s)

    def feat(ref, s, g, v):
        r0 = (v * NSA_KV_HEADS + g) * NSA_HD
        return ref[s, r0:r0 + NSA_HD, :]

    pq = past_len + _iota((qb, 1), 0)
    pq4 = tile4(pq)
    q = q_ref[...]
    qrows, slope = {}, {}
    for g in range(NSA_KV_HEADS):
        hs = range(g * NSA_REP, (g + 1) * NSA_REP)
        slope[g] = jnp.concatenate([jnp.full((qb, 1), _slope(h), F32) for h in hs], axis=0)
        for s in seqs:
            qs = q[s * qb:(s + 1) * qb]
            qrows[(s, g)] = jnp.concatenate([qs[:, h * NSA_HD:(h + 1) * NSA_HD] * (NSA_HD ** -0.5) for h in hs],
                                            axis=0).astype(BF16)

    n_ix = _iota((1, ncp), 1)
    ok = (n_ix * CMP_STRIDE + (CMP_LEN - 1)) <= pq4
    dist_c = pq4.astype(F32) - ((n_ix * CMP_STRIDE).astype(F32) + 0.5 * (CMP_LEN - 1))
    pool = (_idiv(_iota((ncp, LANE), 0), SEL_BLOCK // CMP_STRIDE) == _iota((ncp, LANE), 1)).astype(F32)
    sc = [jnp.where(ok, _dot_nt(qrows[c], ckv[c[0]][:, c[1] * NSA_HD:(c[1] + 1) * NSA_HD]) - slope[c[1]] * dist_c,
                    NEG_INF) for c in chains]
    mc = [jnp.max(x, axis=-1, keepdims=True) for x in sc]
    ec = [jnp.where(ok, jnp.exp(x - m), 0.0) for x, m in zip(sc, mc)]
    dc = [jnp.sum(x, axis=-1, keepdims=True) for x in ec]
    pc = [x / jnp.where(d > 0.0, d, 1.0) for x, d in zip(ec, dc)]
    o_c = [_dot(p, ckv[c[0]][:, (NSA_KV_HEADS + c[1]) * NSA_HD:(NSA_KV_HEADS + c[1] + 1) * NSA_HD])
           for p, c in zip(pc, chains)]
    imps = [_dot01_r(sum(p[r * qb:(r + 1) * qb] for r in range(NSA_REP)), pool) for p in pc]

    n_blk = (past_len + qb - 1) // SEL_BLOCK + 1
    sel = _select_blocks_rank(imps, pq, -(-n_blk // 8) * 8)

    dist_i = pq - _iota((1, tk), 1)
    dist_s = tile4(dist_i.astype(F32))
    e_tile = (_iota((LANE, tk), 0) == _idiv(_iota((LANE, tk), 1), SEL_BLOCK)).astype(BF16)
    on = [(jnp.dot(x.astype(BF16), e_tile, preferred_element_type=F32) > 0.5) & (dist_i >= 0) for x in sel]
    bias = [tile4(jnp.where(x, 0.0, NEG_INF)) for x in on]
    ss = [jnp.dot(qrows[c], feat(slc_scr, c[0], c[1], 0), preferred_element_type=F32)
          + (b - slope[c[1]] * dist_s) for c, b in zip(chains, bias)]
    ps = [_softmax_rows(x) for x in ss]
    o_s = [lax.dot_general(p.astype(BF16), feat(slc_scr, c[0], c[1], 1), _NT, preferred_element_type=F32) / l
           for (p, l), c in zip(ps, chains)]

    dist_wi = pq - (past_len - wb + _iota((1, wk), 1))
    bias_w = tile4(jnp.where((dist_wi >= 0) & (dist_wi < WINDOW), 0.0, NEG_INF))
    dist_w = tile4(dist_wi.astype(F32))
    sw = [jnp.dot(qrows[c], feat(win_scr, c[0], c[1], 0), preferred_element_type=F32)
          + (bias_w - slope[c[1]] * dist_w) for c in chains]
    pw = [_softmax_rows(x) for x in sw]
    o_w = [lax.dot_general(p.astype(BF16), feat(win_scr, c[0], c[1], 1), _NT, preferred_element_type=F32) / l
           for (p, l), c in zip(pw, chains)]

    gates = jax.nn.sigmoid(sm_ref[...])
    for s in seqs:
        gs = gates[s * qb:(s + 1) * qb]
        for h in range(NSA_HEADS):
            g, r = divmod(h, NSA_REP)
            i = s * NSA_KV_HEADS + g
            rr = slice(r * qb, (r + 1) * qb)
            c0 = SM_G + 3 * h
            o = gs[:, c0:c0 + 1] * o_c[i][rr] + gs[:, c0 + 1:c0 + 2] * o_s[i][rr] + gs[:, c0 + 2:c0 + 3] * o_w[i][rr]
            o = o * lax.rsqrt(jnp.mean(o * o, axis=-1, keepdims=True) + EPS) * ng_ref[...]
            o_ref[s * qb:(s + 1) * qb, h * NSA_HD:(h + 1) * NSA_HD] = o


def _nsa_sample(page_table, nq, sm, ynew, slc_new, win_new, win_cache_t, ng, y_past, slc_pool_t, layer, n_pool,
                past_len, dec_len):
    db, n_pages = page_table.shape
    wb = win_cache_t.shape[2]
    tk = past_len + PAGE_SIZE
    wk = wb + PAGE_SIZE
    ns = SAMPLE_SEQS
    assert db % ns == 0
    row = lambda b, pt: (b, 0)
    seq0 = layer * (db // ns)

    def page_map(k):
        s, p = divmod(k, n_pages)
        return lambda b, pt: (layer * n_pool + pt[b * ns + s, p], 0, 0)

    in_specs = [pl.BlockSpec((ns * dec_len, 512), row),
                pl.BlockSpec((ns * dec_len, LANE), row),
                pl.BlockSpec((ns, n_pages * SUB_PP, 2 * NSA_KV_W), lambda b, pt: (seq0 + b, 0, 0)),
                pl.BlockSpec((ns, 8, 2 * NSA_KV_W), lambda b, pt: (b, 0, 0)),
                pl.BlockSpec((ns * dec_len, NSA_KV_W), row),
                pl.BlockSpec((ns * dec_len, NSA_KV_W), row),
                pl.BlockSpec((ns, NSA_KV_W, wb), lambda b, pt: (seq0 + b, 0, 0)),
                pl.BlockSpec((1, NSA_HD), lambda b, pt: (0, 0))]
    in_specs += [pl.BlockSpec((1, NSA_KV_W, PAGE_SIZE), page_map(k)) for k in range(ns * n_pages)]
    grid_spec = pltpu.PrefetchScalarGridSpec(
        num_scalar_prefetch=1,
        grid=(db // ns,),
        in_specs=in_specs,
        out_specs=pl.BlockSpec((ns * dec_len, 512), row),
        scratch_shapes=[pltpu.VMEM((ns, NSA_KV_W, tk), BF16),
                        pltpu.VMEM((ns, NSA_KV_W, wk), BF16)])
    return pl.pallas_call(
        functools.partial(_nsa_sample_kernel, n_pages=n_pages, past_len=past_len, dec_len=dec_len, nseq=ns),
        grid_spec=grid_spec,
        out_shape=jax.ShapeDtypeStruct((db * dec_len, 512), F32),
        compiler_params=_params(("arbitrary",)),
    )(page_table, nq, sm, y_past, ynew, slc_new, win_new, win_cache_t, ng, *([slc_pool_t] * (ns * n_pages)))


def _reorder_proj(w):
    sizes = (GDN_QKV, GDN_HEADS, GDN_HEADS, GDN_HEADS * GDN_DV, NSA_HEADS * NSA_HD,
             NSA_KV_W, NSA_KV_W, NSA_KV_W, 3 * NSA_HEADS)
    off = [0] + [int(v) for v in np.cumsum(sizes)]
    seg = lambda i: w[:, :, off[i]:off[i + 1]].astype(BF16)
    pad = jnp.zeros(w.shape[:2] + (N_PROJ - off[-1],), BF16)
    return jnp.concatenate([seg(0), seg(3), seg(4), seg(5), seg(6), seg(7), seg(1), seg(2), seg(8), pad], axis=2)


def _cmp_weights(cmp_w):
    def half(w):
        rows = []
        for c in range(2):
            for g in range(NSA_KV_HEADS):
                j = c * NSA_KV_HEADS + g
                rows.append(jnp.pad(w[:, :, c], ((0, 0), (0, 0), (0, 0), (j * NSA_HD, NSA_KV_W - (j + 1) * NSA_HD))))
        return jnp.concatenate(rows, axis=2).astype(BF16)

    return half(cmp_w[:, :CMP_STRIDE]), half(cmp_w[:, CMP_STRIDE:])


def _cmp_pe(cmp_pe):
    nl = cmp_pe.shape[0]

    def half(p):
        return jnp.broadcast_to(p[:, :, :, None, :], (nl, CMP_STRIDE, 2, NSA_KV_HEADS, NSA_HD)).reshape(nl, SUB_W)

    return jnp.stack([half(cmp_pe[:, :CMP_STRIDE]), half(cmp_pe[:, CMP_STRIDE:])], axis=1)


def _small_t(sm, c):
    m = sm.shape[0]
    return jnp.transpose(sm[:, :32].reshape(m // c, c, 32), (0, 2, 1))


def _feature_major(a):
    nd = a.ndim
    perm = tuple(range(nd - 4)) + (nd - 3, nd - 2, nd - 1, nd - 4)
    t = jnp.transpose(a, perm)
    return t.reshape(t.shape[:nd - 4] + (NSA_KV_W, t.shape[-1]))


def _row_major_kv(a_t):
    lead = a_t.shape[:-2]
    n = len(lead)
    t = a_t.reshape(lead + (2, NSA_KV_HEADS, NSA_HD, a_t.shape[-1]))
    return jnp.transpose(t, tuple(range(n)) + (n + 3, n, n + 1, n + 2))


def kernel(x_prompt, x_sample, cache_cmp_kv, cache_slc_kv, page_table, cache_win_kv, state_gdn, state_conv,
           norm_mix, w_in, conv_w, gdn_a_log, gdn_dt_bias, gdn_norm, nsa_cmp_w, nsa_cmp_pe, nsa_norm,
           w_out, norm_ffn, w_ffn_in, w_ffn_out, norm_final):
    bp, tp, _ = x_prompt.shape
    db, ts, _ = x_sample.shape
    n_pool = cache_cmp_kv.shape[1]
    n_pages = page_table.shape[1]
    past_len = n_pages * PAGE_SIZE
    wb = cache_win_kv.shape[2]
    kv_row = (2, NSA_KV_HEADS, NSA_HD)
    wkeep = min(WINDOW, tp)
    assert GDN_CHUNK % ts == 0 and (db * ts) % GDN_CHUNK == 0 and tp % SLC_TILE == 0
    assert tp >= WIN_TILES * Q_BLOCK and db % POOL_SEQS == 0

    w_in_b = _reorder_proj(w_in)
    w_t = jnp.transpose(jnp.concatenate([w_in_b[:, :, C_KV:C_SMALL], w_in_b[:, :, C_NQ:C_KV]], axis=2),
                        (0, 2, 1))
    w_out_b = w_out.astype(BF16)
    w_ffn_in_b = w_ffn_in.astype(BF16)
    w_ffn_out_b = w_ffn_out.astype(BF16)
    cw = jnp.pad(conv_w, ((0, 0), (0, 8 - CONV_W), (0, 0)))
    lane_pad = ((0, 0), (SM_A, LANE - SM_A - GDN_HEADS))
    alog_b = jnp.pad(jnp.pad(gdn_a_log, lane_pad)[:, None, :], ((0, 0), (0, 7), (0, 0)))
    dtb_b = jnp.pad(jnp.pad(gdn_dt_bias, lane_pad)[:, None, :], ((0, 0), (0, 7), (0, 0)))
    wlo4, whi4 = _cmp_weights(nsa_cmp_w)
    wlo = wlo4.reshape(DEPTH, SUB_W, NSA_KV_W)
    whi = whi4.reshape(DEPTH, SUB_W, NSA_KV_W)
    pe2 = _cmp_pe(nsa_cmp_pe)

    cmp_pool_t = _feature_major(cache_cmp_kv).reshape(DEPTH * n_pool, NSA_KV_W, PAGE_SIZE)
    y_pool = _compress_pool(page_table, cmp_pool_t, pe2.reshape(DEPTH, 2, CMP_STRIDE, NSA_KV_W), wlo4, whi4,
                            DEPTH, n_pool, POOL_SEQS)
    y_pool = y_pool.reshape(DEPTH * db, n_pages * SUB_PP, 2 * NSA_KV_W)
    slc_pool_t = _feature_major(cache_slc_kv).reshape(DEPTH * n_pool, NSA_KV_W, PAGE_SIZE)
    win_cache_t = _feature_major(cache_win_kv)
    state_all = state_gdn.reshape(DEPTH * db, GDN_HEADS, GDN_DK, GDN_DV)
    seq_per_chunk = GDN_CHUNK // ts

    xp = x_prompt.reshape(bp * tp, D_MODEL)
    xs = x_sample.reshape(db * ts, D_MODEL)
    zero_hist = jnp.zeros((bp, 8, GDN_QKV), F32)
    zero_state = jnp.zeros((bp, GDN_HEADS, GDN_DK, GDN_DV), F32)
    outs = [[] for _ in range(10)]
    for l in range(DEPTH):
        g_mix = norm_mix[l][None, :]
        g_ffn = norm_ffn[l][None, :]
        gdn_g = gdn_norm[l][None, :]
        nsa_g = nsa_norm[l][None, :]

        (qkv, gate, _, ncmp, _, _, sm, cmp_t, slc_t, win_t, nq_t, sk, sv, wk, wv) = _proj(
            xp, g_mix, w_in_b[l], w_t[l], SLC_TILE, bp)
        smt = _small_t(sm, GDN_CHUNK)
        o_gdn, s_new = _gdn(qkv, gate, sm, alog_b[l], dtb_b[l], gdn_g, zero_state, 0,
                            bp, tp // (GDN_CHUNK * GDN_NCH), GDN_CHUNK, GDN_NCH, prep=(zero_hist, cw[l]))
        y = _compress(ncmp.reshape(1, bp * tp // CMP_STRIDE, SUB_W), pe2[l:l + 1], wlo[l:l + 1],
                      whi[l:l + 1], 512).reshape(bp, tp // CMP_STRIDE, 2 * NSA_KV_W)
        o_nsa = _nsa_prompt(nq_t, smt, y, sk, sv, wk, wv, jnp.broadcast_to(nsa_norm[l][:, None], (NSA_HD, Q_BLOCK)),
                            bp, tp)
        xp = _out_proj(xp, o_gdn, o_nsa, w_out_b[l], 512)
        xp = _ffn(xp, g_ffn, w_ffn_in_b[l], w_ffn_out_b[l], 1024)
        outs[0].append(cmp_t)
        outs[2].append(slc_t)
        outs[4].append(win_t[:, :, tp - wkeep:])
        outs[6].append(s_new)
        outs[8].append(qkv.reshape(bp, tp, GDN_QKV)[:, -(CONV_W - 1):])

        qkv, gate, nq, ncmp, nslc, nwin, sm = _proj(xs, g_mix, w_in_b[l], None, 512)
        hist = jnp.pad(state_conv[l], ((0, 0), (8 - (CONV_W - 1), 0), (0, 0)))
        qkvn = _gdn_prep(qkv.reshape(db, ts, GDN_QKV), hist, cw[l], ts).reshape(db * ts, GDN_QKV)
        o_gdn, s_new = _gdn(qkvn, gate, sm, alog_b[l], dtb_b[l], gdn_g, state_all,
                            l * (db // seq_per_chunk), db // seq_per_chunk, 1, ts, 1)
        new_sub = jnp.pad(ncmp.reshape(db, 1, ts * NSA_KV_W), ((0, 0), (0, 7), (0, SUB_W - ts * NSA_KV_W)))
        ynew = _compress(new_sub.reshape(1, db * 8, SUB_W), pe2[l:l + 1], wlo[l:l + 1], whi[l:l + 1],
                         512).reshape(db, 8, 2 * NSA_KV_W)
        o_nsa = _nsa_sample(page_table, nq, sm, ynew, nslc, nwin, win_cache_t.reshape(DEPTH * db, NSA_KV_W, wb),
                            nsa_g, y_pool, slc_pool_t, l, n_pool, past_len, ts)
        xs = _out_proj(xs, o_gdn, o_nsa, w_out_b[l], 512)
        xs = _ffn(xs, g_ffn, w_ffn_in_b[l], w_ffn_out_b[l], 1024)
        outs[1].append(ncmp.reshape((db, ts) + kv_row))
        outs[3].append(nslc.reshape((db, ts) + kv_row))
        outs[5].append(jnp.transpose(nwin.reshape(db, ts, NSA_KV_W), (0, 2, 1)))
        outs[7].append(s_new)
        outs[9].append(jnp.concatenate([state_conv[l], qkv.reshape(db, ts, GDN_QKV)], axis=1)[:, -(CONV_W - 1):])

    y_prompt = _final_norm(xp, norm_final[None, :], 512).reshape(bp, tp, D_MODEL)
    y_sample = _final_norm(xs, norm_final[None, :], 512).reshape(db, ts, D_MODEL)
    res = [jnp.stack(o) for o in outs]
    res[5] = _window_rows(win_cache_t.reshape(DEPTH * db, NSA_KV_W, wb),
                          res[5].reshape(DEPTH * db, NSA_KV_W, ts)).reshape(DEPTH, db, NSA_KV_W, wb)
    for i in (0, 2, 4, 5):
        res[i] = _row_major_kv(res[i])
    return (y_prompt, y_sample) + tuple(res)
```

```python
import functools

import numpy as np
import jax
import jax.numpy as jnp
from jax import lax
from jax.experimental import pallas as pl
from jax.experimental.pallas import tpu as pltpu

F32 = jnp.float32
BF16 = jnp.bfloat16

D_MODEL = 1024
DEPTH = 4
PAGE_SIZE = 128
GDN_HEADS = 4
GDN_DK = 128
GDN_DV = 128
GDN_QKV = GDN_HEADS * (2 * GDN_DK + GDN_DV)
CONV_W = 4
NSA_HEADS = 8
NSA_KV_HEADS = 2
NSA_HD = 64
NSA_REP = NSA_HEADS // NSA_KV_HEADS
CMP_LEN = 32
CMP_STRIDE = 16
SEL_BLOCK = 64
N_SEL = 16
WINDOW = 512
NSA_KV_W = 2 * NSA_KV_HEADS * NSA_HD
D_FF = (8 * D_MODEL + 3 * 256 - 1) // (3 * 256) * 256
NEG_INF = -1e30
LOG2E = 1.4426950408889634
FORCE_BONUS = 1e4
EPS = 1e-6

LANE = 128
GDN_CHUNK = 128
GDN_NCH = 2
INV_BLOCK = 16
Q_BLOCK = 128
SLC_TILE = 512
WIN_TILES = WINDOW // Q_BLOCK + 1
SUB_W = CMP_STRIDE * NSA_KV_W
SUB_PP = PAGE_SIZE // CMP_STRIDE
POOL_SEQS = 2
SAMPLE_SEQS = 4
VMEM_LIMIT = 56 * 1024 * 1024

C_QKV, C_GATE, C_NQ, C_KV, C_SMALL = 0, 1536, 2048, 2560, 3328
N_PROJ = 3456
SM_B, SM_A, SM_G = 0, 4, 8

_NT = (((1,), (1,)), ((), ()))


def _params(sem):
    return pltpu.CompilerParams(dimension_semantics=sem, vmem_limit_bytes=VMEM_LIMIT)


def _dot(a, b):
    return jnp.dot(a.astype(BF16), b.astype(BF16), preferred_element_type=F32)


def _dot_nt(a, b):
    return lax.dot_general(a.astype(BF16), b.astype(BF16), _NT, preferred_element_type=F32)


def _split2(x):
    hi = x.astype(BF16)
    lo = (x - hi.astype(F32)).astype(BF16)
    return hi, lo


def _split3(x):
    hi = x.astype(BF16)
    r = x - hi.astype(F32)
    mid = r.astype(BF16)
    lo = (r - mid.astype(F32)).astype(BF16)
    return hi, mid, lo


def _mm_split(a, b):
    ah, al = a
    bh, bl = b
    d = functools.partial(jnp.dot, preferred_element_type=F32)
    return d(jnp.concatenate([ah, al], axis=1), jnp.concatenate([bh, bh], axis=0)) + d(ah, bl)


def _dotx(a, b):
    return _mm_split(_split2(a), _split2(b))


def _dot01_l(m01, x):
    m = m01.astype(BF16)
    hi, mid, lo = _split3(x)
    d = functools.partial(jnp.dot, preferred_element_type=F32)
    return d(jnp.concatenate([m, m], axis=1), jnp.concatenate([hi, mid], axis=0)) + d(m, lo)


def _dot01_r(x, m01):
    m = m01.astype(BF16)
    hi, mid, lo = _split3(x)
    d = functools.partial(jnp.dot, preferred_element_type=F32)
    return d(hi, m) + (d(mid, m) + d(lo, m))


def _iota(shape, dim):
    return lax.broadcasted_iota(jnp.int32, shape, dim)


def _idiv(x, d):
    sh = int(d).bit_length() - 1
    assert (1 << sh) == d
    return lax.shift_right_logical(x, jnp.int32(sh))


def _rms_rows(x, g):
    return x * lax.rsqrt(jnp.mean(x * x, axis=-1, keepdims=True) + EPS) * g


def _proj_kernel(x_ref, g_ref, w_ref, *rest, with_t):
    if with_t:
        wt_ref = rest[0]
        rest = rest[1:]
    qkv_ref, gate_ref, nq_ref, cmp_ref, slc_ref, win_ref, sm_ref = rest[:7]
    h = _rms_rows(x_ref[...], g_ref[...]).astype(BF16)
    d = functools.partial(jnp.dot, preferred_element_type=F32)
    qkv_ref[...] = d(h, w_ref[:, C_QKV:C_GATE])
    gate_ref[...] = d(h, w_ref[:, C_GATE:C_NQ])
    nq_ref[...] = d(h, w_ref[:, C_NQ:C_KV])
    kv = d(h, w_ref[:, C_KV:C_SMALL])
    cmp_ref[...] = kv[:, 0:NSA_KV_W]
    slc_ref[...] = kv[:, NSA_KV_W:2 * NSA_KV_W]
    win_ref[...] = kv[:, 2 * NSA_KV_W:3 * NSA_KV_W]
    sm_ref[...] = d(h, w_ref[:, C_SMALL:N_PROJ])
    if with_t:
        cmpt_ref, slct_ref, wint_ref, nqt_ref, sk_ref, sv_ref, wk_ref, wv_ref = rest[7:]
        kvt = lax.dot_general(wt_ref[...], h, _NT, preferred_element_type=F32)
        cmpt_ref[0] = kvt[0:NSA_KV_W]
        slct = kvt[NSA_KV_W:2 * NSA_KV_W]
        wint = kvt[2 * NSA_KV_W:3 * NSA_KV_W]
        slct_ref[0] = slct
        wint_ref[0] = wint
        nqt_ref[0] = kvt[3 * NSA_KV_W:3 * NSA_KV_W + 512]
        half = NSA_KV_W // 2
        sk_ref[0, 0] = kv[:, NSA_KV_W:NSA_KV_W + half].astype(BF16)
        sv_ref[0, 0] = slct[half:].astype(BF16)
        for j in range(wk_ref.shape[1]):
            rows = slice(j * Q_BLOCK, (j + 1) * Q_BLOCK)
            wk_ref[0, j] = kv[rows, 2 * NSA_KV_W:2 * NSA_KV_W + half].astype(BF16)
            wv_ref[0, j] = wint[half:, rows].astype(BF16)


def _proj(x, g, w, wt, tm, batch=None):
    m = x.shape[0]
    tm = min(tm, m)
    with_t = wt is not None
    widths = (GDN_QKV, 512, 512, NSA_KV_W, NSA_KV_W, NSA_KV_W, LANE)
    in_specs = [pl.BlockSpec((tm, D_MODEL), lambda i: (i, 0)),
                pl.BlockSpec((1, D_MODEL), lambda i: (0, 0)),
                pl.BlockSpec((D_MODEL, N_PROJ), lambda i: (0, 0))]
    out_specs = [pl.BlockSpec((tm, wd), lambda i: (i, 0)) for wd in widths]
    out_shape = [jax.ShapeDtypeStruct((m, wd), F32) for wd in widths]
    args = [x, g, w]
    if with_t:
        t_len = m // batch
        nt = t_len // tm
        assert tm == SLC_TILE and t_len % tm == 0
        in_specs.append(pl.BlockSpec((wt.shape[0], D_MODEL), lambda i: (0, 0)))
        args.append(wt)
        tmap = lambda i: (i // nt, 0, i % nt)
        tile = lambda i: (i // nt, i % nt, 0, 0)
        half = NSA_KV_W // 2
        nwt = tm // Q_BLOCK
        out_specs += [pl.BlockSpec((1, NSA_KV_W, tm), tmap)] * 3 + [pl.BlockSpec((1, 512, tm), tmap)]
        out_shape += [jax.ShapeDtypeStruct((batch, NSA_KV_W, t_len), F32)] * 3
        out_shape += [jax.ShapeDtypeStruct((batch, 512, t_len), F32)]
        out_specs += [pl.BlockSpec((1, 1, tm, half), tile),
                      pl.BlockSpec((1, 1, half, tm), tile),
                      pl.BlockSpec((1, nwt, Q_BLOCK, half), tile),
                      pl.BlockSpec((1, nwt, half, Q_BLOCK), tile)]
        out_shape += [jax.ShapeDtypeStruct((batch, nt, tm, half), BF16),
                      jax.ShapeDtypeStruct((batch, nt, half, tm), BF16),
                      jax.ShapeDtypeStruct((batch, t_len // Q_BLOCK, Q_BLOCK, half), BF16),
                      jax.ShapeDtypeStruct((batch, t_len // Q_BLOCK, half, Q_BLOCK), BF16)]
    return pl.pallas_call(
        functools.partial(_proj_kernel, with_t=with_t),
        grid=(m // tm,),
        in_specs=in_specs,
        out_specs=out_specs,
        out_shape=out_shape,
        compiler_params=_params(("parallel",)),
    )(*args)


def _out_kernel(x_ref, a1_ref, a2_ref, w_ref, o_ref):
    half = w_ref.shape[0] // 2
    o_ref[...] = x_ref[...] + (_dot(a1_ref[...], w_ref[:half, :]) + _dot(a2_ref[...], w_ref[half:, :]))


def _out_proj(x, a1, a2, w, tm):
    m = x.shape[0]
    tm = min(tm, m)
    return pl.pallas_call(
        _out_kernel,
        grid=(m // tm,),
        in_specs=[pl.BlockSpec((tm, D_MODEL), lambda i: (i, 0)),
                  pl.BlockSpec((tm, 512), lambda i: (i, 0)),
                  pl.BlockSpec((tm, 512), lambda i: (i, 0)),
                  pl.BlockSpec((D_MODEL, D_MODEL), lambda i: (0, 0))],
        out_specs=pl.BlockSpec((tm, D_MODEL), lambda i: (i, 0)),
        out_shape=jax.ShapeDtypeStruct((m, D_MODEL), F32),
        compiler_params=_params(("parallel",)),
    )(x, a1, a2, w)


def _ffn_kernel(x_ref, g_ref, wg_ref, wu_ref, wo_ref, o_ref, h_scr, acc_scr):
    f = pl.program_id(1)

    @pl.when(f == 0)
    def _():
        h_scr[...] = _rms_rows(x_ref[...], g_ref[...]).astype(BF16)
        acc_scr[...] = jnp.zeros_like(acc_scr)

    h = h_scr[...]
    gt = jnp.dot(h, wg_ref[...], preferred_element_type=F32)
    up = jnp.dot(h, wu_ref[...], preferred_element_type=F32)
    act = (gt * jax.nn.sigmoid(gt)) * up
    acc_scr[...] += jnp.dot(act.astype(BF16), wo_ref[...], preferred_element_type=F32)

    @pl.when(f == pl.num_programs(1) - 1)
    def _():
        o_ref[...] = x_ref[...] + acc_scr[...]


def _ffn(x, g, w_in, w_out, tm):
    m = x.shape[0]
    tm = min(tm, m)
    nf = 2
    tf = D_FF // nf
    return pl.pallas_call(
        _ffn_kernel,
        grid=(m // tm, nf),
        in_specs=[pl.BlockSpec((tm, D_MODEL), lambda i, f: (i, 0)),
                  pl.BlockSpec((1, D_MODEL), lambda i, f: (0, 0)),
                  pl.BlockSpec((D_MODEL, tf), lambda i, f: (0, f)),
                  pl.BlockSpec((D_MODEL, tf), lambda i, f: (0, nf + f)),
                  pl.BlockSpec((tf, D_MODEL), lambda i, f: (f, 0))],
        out_specs=pl.BlockSpec((tm, D_MODEL), lambda i, f: (i, 0)),
        out_shape=jax.ShapeDtypeStruct((m, D_MODEL), F32),
        scratch_shapes=[pltpu.VMEM((tm, D_MODEL), BF16), pltpu.VMEM((tm, D_MODEL), F32)],
        compiler_params=_params(("parallel", "arbitrary")),
    )(x, g, w_in, w_in, w_out)


def _window_rows_kernel(c_ref, n_ref, o_ref):
    wb = c_ref.shape[-1]
    ts = n_ref.shape[-1]
    for j in range(c_ref.shape[0]):
        o_ref[j] = pltpu.roll(c_ref[j], wb - ts, 1)
        o_ref[j, :, wb - ts:wb] = n_ref[j]


def _window_rows(cache_t, new_t):
    n, f, wb = cache_t.shape
    ts = new_t.shape[-1]
    assert wb >= ts
    nb = 4 if n % 4 == 0 else 1
    return pl.pallas_call(
        _window_rows_kernel,
        grid=(n // nb,),
        in_specs=[pl.BlockSpec((nb, f, wb), lambda i: (i, 0, 0)),
                  pl.BlockSpec((nb, f, ts), lambda i: (i, 0, 0))],
        out_specs=pl.BlockSpec((nb, f, wb), lambda i: (i, 0, 0)),
        out_shape=jax.ShapeDtypeStruct((n, f, wb), F32),
        compiler_params=_params(("parallel",)),
    )(cache_t, new_t)


def _final_norm_kernel(x_ref, g_ref, o_ref):
    o_ref[...] = _rms_rows(x_ref[...], g_ref[...])


def _final_norm(x, g, tm):
    m = x.shape[0]
    tm = min(tm, m)
    return pl.pallas_call(
        _final_norm_kernel,
        grid=(m // tm,),
        in_specs=[pl.BlockSpec((tm, D_MODEL), lambda i: (i, 0)),
                  pl.BlockSpec((1, D_MODEL), lambda i: (0, 0))],
        out_specs=pl.BlockSpec((tm, D_MODEL), lambda i: (i, 0)),
        out_shape=jax.ShapeDtypeStruct((m, D_MODEL), F32),
        compiler_params=_params(("parallel",)),
    )(x, g)


def _prep_kernel(raw_ref, hist_ref, cw_ref, o_ref, ext_scr, *, tc):
    @pl.when(pl.program_id(1) == 0)
    def _():
        ext_scr[0:8, :] = hist_ref[0]

    ext_scr[8:8 + tc, :] = raw_ref[0]
    for s in range(GDN_QKV // LANE):
        o_ref[0, :, s * LANE:(s + 1) * LANE] = _conv_slab(ext_scr, cw_ref, tc, s)
    tail = ext_scr[tc:tc + 8, :]
    ext_scr[0:8, :] = tail


def _gdn_prep(raw, hist, cw, tc):
    b, t, _ = raw.shape
    return pl.pallas_call(
        functools.partial(_prep_kernel, tc=tc),
        grid=(b, t // tc),
        in_specs=[pl.BlockSpec((1, tc, GDN_QKV), lambda i, j: (i, j, 0)),
                  pl.BlockSpec((1, 8, GDN_QKV), lambda i, j: (i, 0, 0)),
                  pl.BlockSpec((8, GDN_QKV), lambda i, j: (0, 0))],
        out_specs=pl.BlockSpec((1, tc, GDN_QKV), lambda i, j: (i, j, 0)),
        out_shape=jax.ShapeDtypeStruct((b, t, GDN_QKV), F32),
        scratch_shapes=[pltpu.VMEM((tc + 8, GDN_QKV), F32)],
        compiler_params=_params(("parallel", "arbitrary")),
    )(raw, hist, cw)


def _tri_inv(mats, ri, ci):
    c = mats[0].shape[0]
    eye = (ri == ci).astype(F32)
    bd = _idiv(ri, INV_BLOCK) == _idiv(ci, INV_BLOCK)
    ad = [jnp.where(bd, a, 0.0) for a in mats]
    ao = [a - d for a, d in zip(mats, ad)]
    split = lambda xs: [_split2(x) for x in xs]
    mm = lambda xs, ys: [_mm_split(x, y) for x, y in zip(xs, ys)]
    pw_s = split(ad)
    td = [eye - d for d in ad]
    k = 2
    while k < INV_BLOCK:
        pw_s = split(mm(pw_s, pw_s))
        td = [t + x for t, x in zip(td, mm(split(td), pw_s))]
        k *= 2
    td_s = split(td)
    n = mm(td_s, split(ao))
    tn = [eye - x for x in n]
    pw_s = split(n)
    k = 2
    while k < c // INV_BLOCK:
        pw_s = split(mm(pw_s, pw_s))
        tn = [t + x for t, x in zip(tn, mm(split(tn), pw_s))]
        k *= 2
    return mm(split(tn), td_s)


def _softplus(x):
    return jnp.maximum(x, 0.0) + jnp.log1p(jnp.exp(-jnp.abs(x)))


def _conv_slab(ext_ref, cw_ref, rows, s):
    cols = slice(s * LANE, (s + 1) * LANE)
    y = jnp.zeros((rows, LANE), F32)
    for j in range(CONV_W):
        y = y + ext_ref[pl.ds(8 - (CONV_W - 1) + j, rows), cols] * cw_ref[j:j + 1, cols]
    y = y * jax.nn.sigmoid(y)
    if s < 2 * GDN_HEADS:
        y = y * lax.rsqrt(jnp.sum(y * y, axis=-1, keepdims=True) + EPS)
        if s < GDN_HEADS:
            y = y * (GDN_DK ** -0.5)
    return y


def _gdn_kernel(qkv_ref, gate_ref, sm_ref, alog_ref, dtb_ref, ng_ref, mask_ref, s0_ref, *rest, ls, nch, fused):
    c = GDN_CHUNK
    nseq = c // ls
    assert nseq == 1 or nch == 1
    heads = range(GDN_HEADS)
    pairs = [(ch, h) for ch in range(nch) for h in heads]
    if fused:
        hist_ref, cw_ref, o_ref, sout_ref, ext_scr = rest
    else:
        o_ref, sout_ref = rest

    @pl.when(pl.program_id(1) == 0)
    def _():
        sout_ref[...] = s0_ref[...]
        if fused:
            ext_scr[0:8, :] = hist_ref[0]

    if fused:
        rows = nch * c
        ext_scr[8:8 + rows, :] = qkv_ref[...]
        prepped = [_conv_slab(ext_scr, cw_ref, rows, s) for s in range(GDN_QKV // LANE)]
        tail = ext_scr[rows:rows + 8, :]
        ext_scr[0:8, :] = tail

    ri = _iota((c, c), 0)
    ci = _iota((c, c), 1)
    same = _idiv(ri, ls) == _idiv(ci, ls)
    lower = same & (ci <= ri)
    strict = same & (ci < ri)
    lo_m = mask_ref[0]
    same_m = mask_ref[1]
    sm = [sm_ref[ch * c:(ch + 1) * c, :] for ch in range(nch)]

    def slab(ch, s):
        if fused:
            return prepped[s][ch * c:(ch + 1) * c]
        return qkv_ref[ch * c:(ch + 1) * c, s * LANE:(s + 1) * LANE]

    q = [slab(ch, h) for ch, h in pairs]
    k = [slab(ch, GDN_HEADS + h) for ch, h in pairs]
    v = [slab(ch, 2 * GDN_HEADS + h) for ch, h in pairs]
    neg_a = -jnp.exp(alog_ref[0:1, :])
    beta_s = [jax.nn.sigmoid(x) for x in sm]
    g_s = [neg_a * _softplus(x + dtb_ref[0:1, :]) for x in sm]
    beta = [jnp.broadcast_to(beta_s[ch][:, SM_B + h:SM_B + h + 1], (c, LANE)) for ch, h in pairs]
    g_col = [jnp.broadcast_to(g_s[ch][:, SM_A + h:SM_A + h + 1], (c, LANE)) for ch, h in pairs]
    n = range(len(pairs))
    dcy =[_dot01_l(lo_m, g_col[i]) for i in n]
    dcy_row = [x.T for x in dcy]
    if nseq == 1:
        dtot = [jnp.broadcast_to(x[c - 1:c, :], (c, LANE)) for x in dcy]
    else:
        dtot = [_dot01_l(same_m, g_col[i]) for i in n]
    dm = [jnp.where(lower, jnp.exp(jnp.where(lower, dcy[i] - dcy_row[i], 0.0)), 0.0) for i in n]
    kb = [k[i] * beta[i] for i in n]
    kk = [_dot_nt(jnp.concatenate([kb[i], q[i]], axis=0), k[i]) for i in n]
    a_mat = [jnp.where(strict, kk[i][0:c] * dm[i], 0.0) for i in n]
    attn = [jnp.where(lower, kk[i][c:2 * c] * dm[i], 0.0) for i in n]
    edc = [jnp.exp(dcy[i]) for i in n]
    t_inv = _tri_inv(a_mat, ri, ci)
    uw = [_dotx(t_inv[i], jnp.concatenate([v[i] * beta[i], kb[i] * edc[i]], axis=1)) for i in n]
    u = [x[:, 0:LANE] for x in uw]
    w = [x[:, LANE:2 * LANE] for x in uw]
    qd = [q[i] * edc[i] for i in n]
    kdt = [(k[i] * jnp.exp(dtot[i] - dcy[i])).T for i in n]
    gl = [jnp.exp(dtot[i]) for i in n]
    o_all = []
    if nseq == 1:
        s_cur = [sout_ref[0, h] for h in heads]
        for ch in range(nch):
            ix = [ch * GDN_HEADS + h for h in heads]
            ws = [_dot(jnp.concatenate([w[i], qd[i]], axis=0), s_cur[h]) for h, i in zip(heads, ix)]
            v_new = [u[i] - ws[h][0:c] for h, i in zip(heads, ix)]
            o_all += [ws[h][c:2 * c] + _dot(attn[i], v_new[h]) for h, i in zip(heads, ix)]
            s_cur = [s_cur[h] * gl[i][0:1, :] + _dot(kdt[i], v_new[h]) for h, i in zip(heads, ix)]
        for h in heads:
            sout_ref[0, h] = s_cur[h]
    else:
        for h in heads:
            vn, oq = [], []
            for s in range(nseq):
                s_old = sout_ref[s, h]
                rows = slice(s * ls, (s + 1) * ls)
                vn.append(u[h][rows] - _dot(w[h][rows], s_old))
                oq.append(_dot(qd[h][rows], s_old))
            v_new = jnp.concatenate(vn, axis=0)
            o_all.append(jnp.concatenate(oq, axis=0) + _dot(attn[h], v_new))
            for s in range(nseq):
                kdt_s = jnp.where(_idiv(ci, ls) == s, kdt[h], 0.0)
                sout_ref[s, h] = sout_ref[s, h] * gl[h][s * ls:s * ls + 1, :] + _dot(kdt_s, v_new)
    for i, (ch, h) in enumerate(pairs):
        o = o_all[i]
        o = o * lax.rsqrt(jnp.mean(o * o, axis=-1, keepdims=True) + EPS) * ng_ref[...]
        gt = gate_ref[ch * c:(ch + 1) * c, h * LANE:(h + 1) * LANE]
        o_ref[ch * c:(ch + 1) * c, h * LANE:(h + 1) * LANE] = o * (gt * jax.nn.sigmoid(gt))


def _gdn(qkv, gate, sm, alog_b, dtb_b, ng, s0, s0_off, nb, nt, ls, nch, prep=None):
    m = qkv.shape[0]
    c = GDN_CHUNK * nch
    nseq = GDN_CHUNK // ls
    row = lambda i, j: (i * nt + j, 0)
    n_state = nb * nseq
    in_specs = [pl.BlockSpec((c, GDN_QKV), row),
                pl.BlockSpec((c, 512), row),
                pl.BlockSpec((c, LANE), row),
                pl.BlockSpec((8, LANE), lambda i, j: (0, 0)),
                pl.BlockSpec((8, LANE), lambda i, j: (0, 0)),
                pl.BlockSpec((1, LANE), lambda i, j: (0, 0)),
                pl.BlockSpec((2, GDN_CHUNK, GDN_CHUNK), lambda i, j: (0, 0, 0)),
                pl.BlockSpec((nseq, GDN_HEADS, GDN_DK, GDN_DV), lambda i, j: (s0_off + i, 0, 0, 0))]
    idx = np.arange(GDN_CHUNK)
    same_np = (idx[:, None] // ls) == (idx[None, :] // ls)
    masks = jnp.asarray(np.stack([same_np & (idx[None, :] <= idx[:, None]), same_np]), BF16)
    args = [qkv, gate, sm, alog_b, dtb_b, ng, masks, s0]
    scratch = []
    if prep is not None:
        in_specs += [pl.BlockSpec((1, 8, GDN_QKV), lambda i, j: (i, 0, 0)),
                     pl.BlockSpec((8, GDN_QKV), lambda i, j: (0, 0))]
        args += list(prep)
        scratch = [pltpu.VMEM((c + 8, GDN_QKV), F32)]
    return pl.pallas_call(
        functools.partial(_gdn_kernel, ls=ls, nch=nch, fused=prep is not None),
        grid=(nb, nt),
        in_specs=in_specs,
        out_specs=[pl.BlockSpec((c, 512), row),
                   pl.BlockSpec((nseq, GDN_HEADS, GDN_DK, GDN_DV), lambda i, j: (i, 0, 0, 0))],
        out_shape=[jax.ShapeDtypeStruct((m, 512), F32),
                   jax.ShapeDtypeStruct((n_state, GDN_HEADS, GDN_DK, GDN_DV), F32)],
        scratch_shapes=scratch,
        compiler_params=_params(("parallel", "arbitrary")),
    )(*args)


def _cmp_kernel(x_ref, pe_ref, wlo_ref, whi_ref, y_ref):
    x = x_ref[0]
    y_ref[0, :, 0:NSA_KV_W] = _dot(x + pe_ref[0, 0:1, :], wlo_ref[0])
    y_ref[0, :, NSA_KV_W:2 * NSA_KV_W] = _dot(x + pe_ref[0, 1:2, :], whi_ref[0])


def _compress(x, pe, wlo, whi, tr):
    nl, r, _ = x.shape
    tr = min(tr, r)
    return pl.pallas_call(
        _cmp_kernel,
        grid=(nl, r // tr),
        in_specs=[pl.BlockSpec((1, tr, SUB_W), lambda l, i: (l, i, 0)),
                  pl.BlockSpec((1, 2, SUB_W), lambda l, i: (l, 0, 0)),
                  pl.BlockSpec((1, SUB_W, NSA_KV_W), lambda l, i: (l, 0, 0)),
                  pl.BlockSpec((1, SUB_W, NSA_KV_W), lambda l, i: (l, 0, 0))],
        out_specs=pl.BlockSpec((1, tr, 2 * NSA_KV_W), lambda l, i: (l, i, 0)),
        out_shape=jax.ShapeDtypeStruct((nl, r, 2 * NSA_KV_W), F32),
        compiler_params=_params(("parallel", "parallel")),
    )(x, pe, wlo, whi)


def _cmp_pool_kernel(pt_ref, pef_ref, wcat_ref, *rest, npg):
    del pt_ref
    pages = rest[:npg]
    y_ref, x_scr, pb_scr = rest[npg:]

    @pl.when(pl.program_id(1) == 0)
    def _():
        for hf in range(2):
            wf = wcat_ref[0, :, hf].reshape(CMP_STRIDE * LANE, 2 * LANE)
            pb_scr[hf] = _dot(pef_ref[0, hf], wf)

    for j in range(npg):
        x = pages[j][0].T
        for hf in range(2):
            x_scr[hf, j * PAGE_SIZE:(j + 1) * PAGE_SIZE, :] = x[:, hf * LANE:(hf + 1) * LANE]
    n = npg * SUB_PP
    for hf in range(2):
        acc = jnp.zeros((n, 2 * LANE), F32)
        for l in range(0, CMP_STRIDE, 2):
            xl = jnp.concatenate([x_scr[hf, pl.ds(l + j, n, stride=CMP_STRIDE), :] for j in range(2)], axis=1)
            acc = acc + _dot(xl, wcat_ref[0, l:l + 2, hf].reshape(2 * LANE, 2 * LANE))
        y_ref[0, :, hf * LANE:(hf + 1) * LANE] = acc[:, 0:LANE] + pb_scr[hf, 0:1, 0:LANE]
        y_ref[0, :, NSA_KV_W + hf * LANE:NSA_KV_W + (hf + 1) * LANE] = acc[:, LANE:] + pb_scr[hf, 1:2, LANE:]


def _compress_pool(page_table, pages_t, pe, wlo, whi, nl, n_pool, seq_per_step):
    db, n_pages = page_table.shape
    assert db % seq_per_step == 0
    npg = seq_per_step * n_pages
    n = npg * SUB_PP
    halves = [slice(hf * LANE, (hf + 1) * LANE) for hf in range(2)]
    wcat = jnp.stack([jnp.concatenate([wlo[:, :, h, h], whi[:, :, h, h]], axis=-1) for h in halves], axis=2)
    pef = jnp.stack([pe[:, :, :, h].reshape(nl, 2, CMP_STRIDE * LANE) for h in halves], axis=1)
    pef = jnp.pad(pef, ((0, 0), (0, 0), (0, 6), (0, 0)))

    def page_map(k):
        s, p = divmod(k, n_pages)
        return lambda l, i, pt: (l * n_pool + pt[i * seq_per_step + s, p], 0, 0)

    in_specs = [pl.BlockSpec((1, 2, 8, CMP_STRIDE * LANE), lambda l, i, pt: (l, 0, 0, 0)),
                pl.BlockSpec((1, CMP_STRIDE, 2, LANE, 2 * LANE), lambda l, i, pt: (l, 0, 0, 0, 0))]
    in_specs += [pl.BlockSpec((1, NSA_KV_W, PAGE_SIZE), page_map(k)) for k in range(npg)]
    grid_spec = pltpu.PrefetchScalarGridSpec(
        num_scalar_prefetch=1,
        grid=(nl, db // seq_per_step),
        in_specs=in_specs,
        out_specs=pl.BlockSpec((1, n, 2 * NSA_KV_W), lambda l, i, pt: (l, i, 0)),
        scratch_shapes=[pltpu.VMEM((2, npg * PAGE_SIZE, LANE), F32),
                        pltpu.VMEM((2, 8, 2 * LANE), F32)])
    return pl.pallas_call(
        functools.partial(_cmp_pool_kernel, npg=npg),
        grid_spec=grid_spec,
        out_shape=jax.ShapeDtypeStruct((nl, db * n_pages * SUB_PP, 2 * NSA_KV_W), F32),
        compiler_params=_params(("parallel", "arbitrary")),
    )(page_table, pef, wcat, *([pages_t] * npg))


def _slope(h):
    return 2.0 ** (-8.0 * (h + 1) / NSA_HEADS)


def _select_blocks(imps, pq):
    ax = 0
    blk = _iota(imps[0].shape, ax)
    cur = _idiv(pq, SEL_BLOCK)
    forced = (blk == 0) | (blk == cur) | (blk == cur - 1)
    bonus = jnp.where(forced, FORCE_BONUS, 0.0)
    work = [jnp.where(blk <= cur, imp + bonus, NEG_INF) for imp in imps]
    idx = blk.astype(F32)
    sel = [jnp.zeros(w.shape, F32) for w in work]
    for _ in range(N_SEL):
        m = [jnp.max(w, axis=ax, keepdims=True) for w in work]
        first = [jnp.min(jnp.where(w == mi, idx, 2.0 * LANE), axis=ax, keepdims=True) for w, mi in zip(work, m)]
        hit = [idx == f for f in first]
        sel = [jnp.where(h, 1.0, s) for h, s in zip(hit, sel)]
        work = [jnp.where(h, -jnp.inf, w) for h, w in zip(hit, work)]
    return sel


def _select_blocks_rank(imps, pq, nb):
    qb = imps[0].shape[0]
    blk = _iota((qb, LANE), 1)
    cur = _idiv(pq, SEL_BLOCK)
    forced = (blk == 0) | (blk == cur) | (blk == cur - 1)
    bonus = jnp.where(forced, FORCE_BONUS, 0.0)
    score = [jnp.where(blk <= cur, imp + bonus, NEG_INF) for imp in imps]
    fill = jnp.full((LANE - len(imps) * qb, LANE), NEG_INF, F32)
    st = jnp.concatenate(score + [fill], axis=0).T[0:nb]
    jrow = _iota((nb, LANE), 0)
    cnt = jnp.zeros((nb, LANE), F32)
    for j in range(nb):
        row = st[j:j + 1, :]
        cnt = cnt + jnp.where(jrow > j, jnp.where(row >= st, 1.0, 0.0), jnp.where(row > st, 1.0, 0.0))
    sel_t = jnp.where(cnt < N_SEL, 1.0, 0.0)
    sel = jnp.concatenate([sel_t, jnp.zeros((LANE - nb, LANE), F32)], axis=0).T
    return [sel[g * qb:(g + 1) * qb] for g in range(len(imps))]


def _key_aug(t_col, lane, qblk):
    rel = (_idiv(t_col, SEL_BLOCK) - qblk).astype(F32)
    off = (t_col & (SEL_BLOCK - 1)).astype(F32)
    return jnp.where(lane < 4, jnp.where((lane & 1) == 0, rel, off), 0.0).astype(BF16)


def _nsa_prompt_kernel(qt_ref, smt_ref, y_ref, sk_ref, sv_ref, wk_ref, wv_ref, ngb_ref, o_ref,
                       kc_scr, vct_scr, selb_scr, m_scr, l_scr, acc_scr, *, t_len):
    i = pl.program_id(1)
    nsub = t_len // CMP_STRIDE
    gw = NSA_REP * Q_BLOCK
    half = NSA_KV_W // 2
    groups = range(NSA_KV_HEADS)

    @pl.when(i == 0)
    def _():
        y = y_ref[0]
        ckv = y[:, 0:NSA_KV_W] + pltpu.roll(y[:, NSA_KV_W:2 * NSA_KV_W], nsub - 1, 0)
        kc_scr[...] = ckv[:, 0:half].astype(BF16)
        vct_scr[...] = ckv[:, half:].T.astype(BF16)

    pos0 = i * Q_BLOCK
    qblk = pos0 // SEL_BLOCK
    pq_row = pos0 + _iota((1, Q_BLOCK), 1)
    pq_g = jnp.concatenate([pq_row] * NSA_REP, axis=1)
    qt = qt_ref[0] * (NSA_HD ** -0.5 * LOG2E)
    zeros = jnp.zeros((NSA_HD, gw), F32)
    slope_g, rq, rfull = [], [], []
    aug_row = _iota((LANE, gw), 0)
    for g in groups:
        hs = range(g * NSA_REP, (g + 1) * NSA_REP)
        qg = jnp.concatenate([qt[h * NSA_HD:(h + 1) * NSA_HD, :] for h in hs], axis=1)
        rq.append(jnp.concatenate([qg, zeros] if g == 0 else [zeros, qg], axis=0).astype(BF16))
        sl = jnp.concatenate([jnp.full((1, Q_BLOCK), _slope(h) * LOG2E, F32) for h in hs], axis=1)
        slope_g.append(sl)
        sl_hi = sl.astype(BF16).astype(F32)
        sl_p = jnp.where(aug_row < 2, sl_hi, sl - sl_hi)
        raug = jnp.where(aug_row < 4, jnp.where((aug_row & 1) == 0, SEL_BLOCK * sl_p, sl_p), 0.0).astype(BF16)
        rfull.append(jnp.concatenate([rq[g], raug], axis=0))

    n_col = _iota((nsub, gw), 0)
    ok = (n_col * CMP_STRIDE + (CMP_LEN - 1)) <= pq_g
    dist = pq_g.astype(F32) - ((n_col * CMP_STRIDE).astype(F32) + 0.5 * (CMP_LEN - 1))
    kc = kc_scr[...]
    s = [jnp.dot(kc, rq[g], preferred_element_type=F32) for g in groups]
    s = [jnp.where(ok, s[g] - slope_g[g] * dist, NEG_INF) for g in groups]
    m = [jnp.max(x, axis=0, keepdims=True) for x in s]
    e = [jnp.where(ok, jnp.exp2(s[g] - m[g]), 0.0) for g in groups]
    den = [jnp.sum(x, axis=0, keepdims=True) for x in e]
    p = [e[g] / jnp.where(den[g] > 0.0, den[g], 1.0) for g in groups]
    o_c = [jnp.dot(vct_scr[g * NSA_HD:(g + 1) * NSA_HD, :], p[g].astype(BF16), preferred_element_type=F32)
           for g in groups]
    psum = [sum(p[g][:, r * Q_BLOCK:(r + 1) * Q_BLOCK] for r in range(NSA_REP)) for g in groups]
    pool_t = (_iota((LANE, nsub), 0) == _idiv(_iota((LANE, nsub), 1), SEL_BLOCK // CMP_STRIDE)).astype(F32)
    imp_t = [_dot01_l(pool_t, psum[g]) for g in groups]

    sel = _select_blocks(imp_t, pq_row)
    for g in groups:
        selb_scr[g] = jnp.where(sel[g] > 0.5, 0.0, NEG_INF)
    used = [jnp.max(x, axis=1, keepdims=True) for x in sel]
    tile_of_blk = _idiv(_iota((LANE, 1), 0), SLC_TILE // SEL_BLOCK)

    m_scr[...] = jnp.full(m_scr.shape, NEG_INF, F32)
    l_scr[...] = jnp.zeros(l_scr.shape, F32)
    acc_scr[...] = jnp.zeros(acc_scr.shape, F32)
    n_tiles = (pos0 + Q_BLOCK + SLC_TILE - 1) // SLC_TILE
    blk_per_tile = SLC_TILE // SEL_BLOCK

    def slc_tile(kt, causal):
        k0 = kt * SLC_TILE
        t_col = k0 + _iota((SLC_TILE, LANE), 0)
        lhs = jnp.concatenate([sk_ref[0, kt], _key_aug(t_col, _iota((SLC_TILE, LANE), 1), qblk)], axis=1)
        bias = [jnp.concatenate(
            [jnp.broadcast_to(selb_scr[g, pl.ds(kt * blk_per_tile + j, 1), :], (SEL_BLOCK, Q_BLOCK))
             for j in range(blk_per_tile)], axis=0) for g in groups]
        if causal:
            bias = [jnp.where(t_col <= pq_row, b, NEG_INF) for b in bias]
        sc = [jnp.dot(lhs, rfull[g], preferred_element_type=F32) + jnp.concatenate([bias[g]] * NSA_REP, axis=1)
              for g in groups]
        m_prev = [m_scr[g, 0:1, :] for g in groups]
        m_new = [jnp.maximum(m_prev[g], jnp.max(sc[g], axis=0, keepdims=True)) for g in groups]
        alpha = [jnp.exp2(m_prev[g] - m_new[g]) for g in groups]
        pr = [jnp.exp2(sc[g] - m_new[g]) for g in groups]
        l_new = [alpha[g] * l_scr[g, 0:1, :] + jnp.sum(pr[g], axis=0, keepdims=True) for g in groups]
        pv = [jnp.dot(sv_ref[0, kt, g * NSA_HD:(g + 1) * NSA_HD, :], pr[g].astype(BF16),
                      preferred_element_type=F32) for g in groups]
        for g in groups:
            acc_scr[g] = alpha[g] * acc_scr[g] + pv[g]
            l_scr[g, 0:1, :] = l_new[g]
            m_scr[g, 0:1, :] = m_new[g]

    used_any = jnp.maximum(used[0], used[1])

    def body(kt, carry):
        @pl.when(jnp.max(jnp.where(tile_of_blk == kt, used_any, 0.0)) > 0.5)
        def _():
            slc_tile(kt, False)
        return carry

    lax.fori_loop(0, n_tiles - 1, body, 0)
    slc_tile(n_tiles - 1, True)

    wt0 = jnp.maximum(i - (WIN_TILES - 1), 0)
    wkeys = WIN_TILES * Q_BLOCK
    kw = jnp.concatenate([wk_ref[0, wt0 + j] for j in range(WIN_TILES)], axis=0)
    vw = jnp.concatenate([wv_ref[0, wt0 + j] for j in range(WIN_TILES)], axis=1)
    tw = wt0 * Q_BLOCK + _iota((wkeys, LANE), 0)
    lhs_w = jnp.concatenate([kw, _key_aug(tw, _iota((wkeys, LANE), 1), qblk)], axis=1)
    dist_w = pq_row - tw
    bias_w = jnp.where((dist_w >= 0) & (dist_w < WINDOW), 0.0, NEG_INF)
    bias_w = jnp.concatenate([bias_w] * NSA_REP, axis=1)
    sw = [jnp.dot(lhs_w, rfull[g], preferred_element_type=F32) + bias_w for g in groups]
    mw = [jnp.max(x, axis=0, keepdims=True) for x in sw]
    pw = [jnp.exp2(sw[g] - mw[g]) for g in groups]
    dw = [jnp.sum(x, axis=0, keepdims=True) for x in pw]
    o_w = [jnp.dot(vw[g * NSA_HD:(g + 1) * NSA_HD, :], pw[g].astype(BF16), preferred_element_type=F32) / dw[g]
           for g in groups]

    gates = jax.nn.sigmoid(smt_ref[0])
    outs = []
    for h in range(NSA_HEADS):
        g, r = divmod(h, NSA_REP)
        lanes = slice(r * Q_BLOCK, (r + 1) * Q_BLOCK)
        o_s = acc_scr[g][:, lanes] / l_scr[g, 0:1, lanes]
        c0 = SM_G + 3 * h
        o = gates[c0:c0 + 1] * o_c[g][:, lanes] + gates[c0 + 1:c0 + 2] * o_s + gates[c0 + 2:c0 + 3] * o_w[g][:, lanes]
        outs.append(o * lax.rsqrt(jnp.mean(o * o, axis=0, keepdims=True) + EPS) * ngb_ref[...])
    o_ref[...] = jnp.concatenate(outs, axis=0).T


def _nsa_prompt(nq_t, smt, y, sk, sv, wk, wv, ngb, b, t_len):
    nqb = t_len // Q_BLOCK
    nsub = t_len // CMP_STRIDE
    nst = t_len // SLC_TILE
    half = NSA_KV_W // 2
    gw = NSA_REP * Q_BLOCK
    whole = lambda bi, i: (bi, 0, 0, 0)
    return pl.pallas_call(
        functools.partial(_nsa_prompt_kernel, t_len=t_len),
        grid=(b, nqb),
        in_specs=[pl.BlockSpec((1, 512, Q_BLOCK), lambda bi, i: (bi, 0, i)),
                  pl.BlockSpec((1, 32, Q_BLOCK), lambda bi, i: (bi * nqb + i, 0, 0)),
                  pl.BlockSpec((1, nsub, 2 * NSA_KV_W), lambda bi, i: (bi, 0, 0)),
                  pl.BlockSpec((1, nst, SLC_TILE, half), whole),
                  pl.BlockSpec((1, nst, half, SLC_TILE), whole),
                  pl.BlockSpec((1, nqb, Q_BLOCK, half), whole),
                  pl.BlockSpec((1, nqb, half, Q_BLOCK), whole),
                  pl.BlockSpec((NSA_HD, Q_BLOCK), lambda bi, i: (0, 0))],
        out_specs=pl.BlockSpec((Q_BLOCK, 512), lambda bi, i: (bi * nqb + i, 0)),
        out_shape=jax.ShapeDtypeStruct((b * t_len, 512), F32),
        scratch_shapes=[pltpu.VMEM((nsub, half), BF16),
                        pltpu.VMEM((half, nsub), BF16),
                        pltpu.VMEM((NSA_KV_HEADS, LANE, Q_BLOCK), F32),
                        pltpu.VMEM((NSA_KV_HEADS, 8, gw), F32),
                        pltpu.VMEM((NSA_KV_HEADS, 8, gw), F32),
                        pltpu.VMEM((NSA_KV_HEADS, NSA_HD, gw), F32)],
        compiler_params=_params(("parallel", "arbitrary")),
    )(nq_t, smt, y, sk, sv, wk, wv, ngb)


def _softmax_rows(s):
    m = jnp.max(s, axis=-1, keepdims=True)
    p = jnp.exp(s - m)
    return p, jnp.sum(p, axis=-1, keepdims=True)


def _nsa_sample_kernel(pt_ref, q_ref, sm_ref, ypast_ref, ynew_ref, slcn_ref, winn_ref, winc_ref, ng_ref, *rest,
                       n_pages, past_len, dec_len, nseq):
    del pt_ref
    s_pages = rest[0:nseq * n_pages]
    o_ref = rest[nseq * n_pages]
    slc_scr, win_scr = rest[nseq * n_pages + 1:]
    ncp = n_pages * SUB_PP
    tk = slc_scr.shape[2]
    wk = win_scr.shape[2]
    wb = winc_ref.shape[2]
    qb = dec_len
    rows = NSA_REP * qb
    seqs = range(nseq)
    chains = [(s, g) for s in seqs for g in range(NSA_KV_HEADS)]
    tile4 = lambda x: jnp.concatenate([x] * NSA_REP, axis=0)

    def new_cols(ref, s):
        pad = jnp.zeros((PAGE_SIZE - qb, NSA_KV_W), F32)
        return jnp.concatenate([ref[s * qb:(s + 1) * qb, :], pad], axis=0).T.astype(BF16)

    ckv = []
    for s in seqs:
        y = jnp.concatenate([ypast_ref[s], ynew_ref[s]], axis=0)
        ckv.append((y[:, 0:NSA_KV_W] + pltpu.roll(y[:, NSA_KV_W:2 * NSA_KV_W], ncp + 7, 0))[0:ncp].astype(BF16))
        for p in range(n_pages):
            slc_scr[s, :, p * PAGE_SIZE:(p + 1) * PAGE_SIZE] = s_pages[s * n_pages + p][0].astype(BF16)
        slc_scr[s, :, past_len:tk] = new_cols(slcn_ref, s)
        win_scr[s, :, 0:wb] = winc_ref[s].astype(BF16)
        win_scr[s, :, wb:wk] = new_cols(winn_ref, s)

    def feat(ref, s, g, v):
        r0 = (v * NSA_KV_HEADS + g) * NSA_HD
        return ref[s, r0:r0 + NSA_HD, :]

    pq = past_len + _iota((qb, 1), 0)
    pq4 = tile4(pq)
    q = q_ref[...]
    qrows, slope = {}, {}
    for g in range(NSA_KV_HEADS):
        hs = range(g * NSA_REP, (g + 1) * NSA_REP)
        slope[g] = jnp.concatenate([jnp.full((qb, 1), _slope(h), F32) for h in hs], axis=0)
        for s in seqs:
            qs = q[s * qb:(s + 1) * qb]
            qrows[(s, g)] = jnp.concatenate([qs[:, h * NSA_HD:(h + 1) * NSA_HD] * (NSA_HD ** -0.5) for h in hs],
                                            axis=0).astype(BF16)

    n_ix = _iota((1, ncp), 1)
    ok = (n_ix * CMP_STRIDE + (CMP_LEN - 1)) <= pq4
    dist_c = pq4.astype(F32) - ((n_ix * CMP_STRIDE).astype(F32) + 0.5 * (CMP_LEN - 1))
    pool = (_idiv(_iota((ncp, LANE), 0), SEL_BLOCK // CMP_STRIDE) == _iota((ncp, LANE), 1)).astype(F32)
    sc = [jnp.where(ok, _dot_nt(qrows[c], ckv[c[0]][:, c[1] * NSA_HD:(c[1] + 1) * NSA_HD]) - slope[c[1]] * dist_c,
                    NEG_INF) for c in chains]
    mc = [jnp.max(x, axis=-1, keepdims=True) for x in sc]
    ec = [jnp.where(ok, jnp.exp(x - m), 0.0) for x, m in zip(sc, mc)]
    dc = [jnp.sum(x, axis=-1, keepdims=True) for x in ec]
    pc = [x / jnp.where(d > 0.0, d, 1.0) for x, d in zip(ec, dc)]
    o_c = [_dot(p, ckv[c[0]][:, (NSA_KV_HEADS + c[1]) * NSA_HD:(NSA_KV_HEADS + c[1] + 1) * NSA_HD])
           for p, c in zip(pc, chains)]
    imps = [_dot01_r(sum(p[r * qb:(r + 1) * qb] for r in range(NSA_REP)), pool) for p in pc]

    n_blk = (past_len + qb - 1) // SEL_BLOCK + 1
    sel = _select_blocks_rank(imps, pq, -(-n_blk // 8) * 8)

    dist_i = pq - _iota((1, tk), 1)
    dist_s = tile4(dist_i.astype(F32))
    e_tile = (_iota((LANE, tk), 0) == _idiv(_iota((LANE, tk), 1), SEL_BLOCK)).astype(BF16)
    on = [(jnp.dot(x.astype(BF16), e_tile, preferred_element_type=F32) > 0.5) & (dist_i >= 0) for x in sel]
    bias = [tile4(jnp.where(x, 0.0, NEG_INF)) for x in on]
    ss = [jnp.dot(qrows[c], feat(slc_scr, c[0], c[1], 0), preferred_element_type=F32)
          + (b - slope[c[1]] * dist_s) for c, b in zip(chains, bias)]
    ps = [_softmax_rows(x) for x in ss]
    o_s = [lax.dot_general(p.astype(BF16), feat(slc_scr, c[0], c[1], 1), _NT, preferred_element_type=F32) / l
           for (p, l), c in zip(ps, chains)]

    dist_wi = pq - (past_len - wb + _iota((1, wk), 1))
    bias_w = tile4(jnp.where((dist_wi >= 0) & (dist_wi < WINDOW), 0.0, NEG_INF))
    dist_w = tile4(dist_wi.astype(F32))
    sw = [jnp.dot(qrows[c], feat(win_scr, c[0], c[1], 0), preferred_element_type=F32)
          + (bias_w - slope[c[1]] * dist_w) for c in chains]
    pw = [_softmax_rows(x) for x in sw]
    o_w = [lax.dot_general(p.astype(BF16), feat(win_scr, c[0], c[1], 1), _NT, preferred_element_type=F32) / l
           for (p, l), c in zip(pw, chains)]

    gates = jax.nn.sigmoid(sm_ref[...])
    for s in seqs:
        gs = gates[s * qb:(s + 1) * qb]
        for h in range(NSA_HEADS):
            g, r = divmod(h, NSA_REP)
            i = s * NSA_KV_HEADS + g
            rr = slice(r * qb, (r + 1) * qb)
            c0 = SM_G + 3 * h
            o = gs[:, c0:c0 + 1] * o_c[i][rr] + gs[:, c0 + 1:c0 + 2] * o_s[i][rr] + gs[:, c0 + 2:c0 + 3] * o_w[i][rr]
            o = o * lax.rsqrt(jnp.mean(o * o, axis=-1, keepdims=True) + EPS) * ng_ref[...]
            o_ref[s * qb:(s + 1) * qb, h * NSA_HD:(h + 1) * NSA_HD] = o


def _nsa_sample(page_table, nq, sm, ynew, slc_new, win_new, win_cache_t, ng, y_past, slc_pool_t, layer, n_pool,
                past_len, dec_len):
    db, n_pages = page_table.shape
    wb = win_cache_t.shape[2]
    tk = past_len + PAGE_SIZE
    wk = wb + PAGE_SIZE
    ns = SAMPLE_SEQS
    assert db % ns == 0
    row = lambda b, pt: (b, 0)
    seq0 = layer * (db // ns)

    def page_map(k):
        s, p = divmod(k, n_pages)
        return lambda b, pt: (layer * n_pool + pt[b * ns + s, p], 0, 0)

    in_specs = [pl.BlockSpec((ns * dec_len, 512), row),
                pl.BlockSpec((ns * dec_len, LANE), row),
                pl.BlockSpec((ns, n_pages * SUB_PP, 2 * NSA_KV_W), lambda b, pt: (seq0 + b, 0, 0)),
                pl.BlockSpec((ns, 8, 2 * NSA_KV_W), lambda b, pt: (b, 0, 0)),
                pl.BlockSpec((ns * dec_len, NSA_KV_W), row),
                pl.BlockSpec((ns * dec_len, NSA_KV_W), row),
                pl.BlockSpec((ns, NSA_KV_W, wb), lambda b, pt: (seq0 + b, 0, 0)),
                pl.BlockSpec((1, NSA_HD), lambda b, pt: (0, 0))]
    in_specs += [pl.BlockSpec((1, NSA_KV_W, PAGE_SIZE), page_map(k)) for k in range(ns * n_pages)]
    grid_spec = pltpu.PrefetchScalarGridSpec(
        num_scalar_prefetch=1,
        grid=(db // ns,),
        in_specs=in_specs,
        out_specs=pl.BlockSpec((ns * dec_len, 512), row),
        scratch_shapes=[pltpu.VMEM((ns, NSA_KV_W, tk), BF16),
                        pltpu.VMEM((ns, NSA_KV_W, wk), BF16)])
    return pl.pallas_call(
        functools.partial(_nsa_sample_kernel, n_pages=n_pages, past_len=past_len, dec_len=dec_len, nseq=ns),
        grid_spec=grid_spec,
        out_shape=jax.ShapeDtypeStruct((db * dec_len, 512), F32),
        compiler_params=_params(("arbitrary",)),
    )(page_table, nq, sm, y_past, ynew, slc_new, win_new, win_cache_t, ng, *([slc_pool_t] * (ns * n_pages)))


def _reorder_proj(w):
    sizes = (GDN_QKV, GDN_HEADS, GDN_HEADS, GDN_HEADS * GDN_DV, NSA_HEADS * NSA_HD,
             NSA_KV_W, NSA_KV_W, NSA_KV_W, 3 * NSA_HEADS)
    off = [0] + [int(v) for v in np.cumsum(sizes)]
    seg = lambda i: w[:, :, off[i]:off[i + 1]].astype(BF16)
    pad = jnp.zeros(w.shape[:2] + (N_PROJ - off[-1],), BF16)
    return jnp.concatenate([seg(0), seg(3), seg(4), seg(5), seg(6), seg(7), seg(1), seg(2), seg(8), pad], axis=2)


def _cmp_weights(cmp_w):
    def half(w):
        rows = []
        for c in range(2):
            for g in range(NSA_KV_HEADS):
                j = c * NSA_KV_HEADS + g
                rows.append(jnp.pad(w[:, :, c], ((0, 0), (0, 0), (0, 0), (j * NSA_HD, NSA_KV_W - (j + 1) * NSA_HD))))
        return jnp.concatenate(rows, axis=2).astype(BF16)

    return half(cmp_w[:, :CMP_STRIDE]), half(cmp_w[:, CMP_STRIDE:])


def _cmp_pe(cmp_pe):
    nl = cmp_pe.shape[0]

    def half(p):
        return jnp.broadcast_to(p[:, :, :, None, :], (nl, CMP_STRIDE, 2, NSA_KV_HEADS, NSA_HD)).reshape(nl, SUB_W)

    return jnp.stack([half(cmp_pe[:, :CMP_STRIDE]), half(cmp_pe[:, CMP_STRIDE:])], axis=1)


def _small_t(sm, c):
    m = sm.shape[0]
    return jnp.transpose(sm[:, :32].reshape(m // c, c, 32), (0, 2, 1))


def _feature_major(a):
    nd = a.ndim
    perm = tuple(range(nd - 4)) + (nd - 3, nd - 2, nd - 1, nd - 4)
    t = jnp.transpose(a, perm)
    return t.reshape(t.shape[:nd - 4] + (NSA_KV_W, t.shape[-1]))


def _row_major_kv(a_t):
    lead = a_t.shape[:-2]
    n = len(lead)
    t = a_t.reshape(lead + (2, NSA_KV_HEADS, NSA_HD, a_t.shape[-1]))
    return jnp.transpose(t, tuple(range(n)) + (n + 3, n, n + 1, n + 2))


def kernel(x_prompt, x_sample, cache_cmp_kv, cache_slc_kv, page_table, cache_win_kv, state_gdn, state_conv,
           norm_mix, w_in, conv_w, gdn_a_log, gdn_dt_bias, gdn_norm, nsa_cmp_w, nsa_cmp_pe, nsa_norm,
           w_out, norm_ffn, w_ffn_in, w_ffn_out, norm_final):
    bp, tp, _ = x_prompt.shape
    db, ts, _ = x_sample.shape
    n_pool = cache_cmp_kv.shape[1]
    n_pages = page_table.shape[1]
    past_len = n_pages * PAGE_SIZE
    wb = cache_win_kv.shape[2]
    kv_row = (2, NSA_KV_HEADS, NSA_HD)
    wkeep = min(WINDOW, tp)
    assert GDN_CHUNK % ts == 0 and (db * ts) % GDN_CHUNK == 0 and tp % SLC_TILE == 0
    assert tp >= WIN_TILES * Q_BLOCK and db % POOL_SEQS == 0

    w_in_b = _reorder_proj(w_in)
    w_t = jnp.transpose(jnp.concatenate([w_in_b[:, :, C_KV:C_SMALL], w_in_b[:, :, C_NQ:C_KV]], axis=2),
                        (0, 2, 1))
    w_out_b = w_out.astype(BF16)
    w_ffn_in_b = w_ffn_in.astype(BF16)
    w_ffn_out_b = w_ffn_out.astype(BF16)
    cw = jnp.pad(conv_w, ((0, 0), (0, 8 - CONV_W), (0, 0)))
    lane_pad = ((0, 0), (SM_A, LANE - SM_A - GDN_HEADS))
    alog_b = jnp.pad(jnp.pad(gdn_a_log, lane_pad)[:, None, :], ((0, 0), (0, 7), (0, 0)))
    dtb_b = jnp.pad(jnp.pad(gdn_dt_bias, lane_pad)[:, None, :], ((0, 0), (0, 7), (0, 0)))
    wlo4, whi4 = _cmp_weights(nsa_cmp_w)
    wlo = wlo4.reshape(DEPTH, SUB_W, NSA_KV_W)
    whi = whi4.reshape(DEPTH, SUB_W, NSA_KV_W)
    pe2 = _cmp_pe(nsa_cmp_pe)

    cmp_pool_t = _feature_major(cache_cmp_kv).reshape(DEPTH * n_pool, NSA_KV_W, PAGE_SIZE)
    y_pool = _compress_pool(page_table, cmp_pool_t, pe2.reshape(DEPTH, 2, CMP_STRIDE, NSA_KV_W), wlo4, whi4,
                            DEPTH, n_pool, POOL_SEQS)
    y_pool = y_pool.reshape(DEPTH * db, n_pages * SUB_PP, 2 * NSA_KV_W)
    slc_pool_t = _feature_major(cache_slc_kv).reshape(DEPTH * n_pool, NSA_KV_W, PAGE_SIZE)
    win_cache_t = _feature_major(cache_win_kv)
    state_all = state_gdn.reshape(DEPTH * db, GDN_HEADS, GDN_DK, GDN_DV)
    seq_per_chunk = GDN_CHUNK // ts

    xp = x_prompt.reshape(bp * tp, D_MODEL)
    xs = x_sample.reshape(db * ts, D_MODEL)
    zero_hist = jnp.zeros((bp, 8, GDN_QKV), F32)
    zero_state = jnp.zeros((bp, GDN_HEADS, GDN_DK, GDN_DV), F32)
    outs = [[] for _ in range(10)]
    for l in range(DEPTH):
        g_mix = norm_mix[l][None, :]
        g_ffn = norm_ffn[l][None, :]
        gdn_g = gdn_norm[l][None, :]
        nsa_g = nsa_norm[l][None, :]

        (qkv, gate, _, ncmp, _, _, sm, cmp_t, slc_t, win_t, nq_t, sk, sv, wk, wv) = _proj(
            xp, g_mix, w_in_b[l], w_t[l], SLC_TILE, bp)
        smt = _small_t(sm, GDN_CHUNK)
        o_gdn, s_new = _gdn(qkv, gate, sm, alog_b[l], dtb_b[l], gdn_g, zero_state, 0,
                            bp, tp // (GDN_CHUNK * GDN_NCH), GDN_CHUNK, GDN_NCH, prep=(zero_hist, cw[l]))
        y = _compress(ncmp.reshape(1, bp * tp // CMP_STRIDE, SUB_W), pe2[l:l + 1], wlo[l:l + 1],
                      whi[l:l + 1], 512).reshape(bp, tp // CMP_STRIDE, 2 * NSA_KV_W)
        o_nsa = _nsa_prompt(nq_t, smt, y, sk, sv, wk, wv, jnp.broadcast_to(nsa_norm[l][:, None], (NSA_HD, Q_BLOCK)),
                            bp, tp)
        xp = _out_proj(xp, o_gdn, o_nsa, w_out_b[l], 512)
        xp = _ffn(xp, g_ffn, w_ffn_in_b[l], w_ffn_out_b[l], 1024)
        outs[0].append(cmp_t)
        outs[2].append(slc_t)
        outs[4].append(win_t[:, :, tp - wkeep:])
        outs[6].append(s_new)
        outs[8].append(qkv.reshape(bp, tp, GDN_QKV)[:, -(CONV_W - 1):])

        qkv, gate, nq, ncmp, nslc, nwin, sm = _proj(xs, g_mix, w_in_b[l], None, 512)
        hist = jnp.pad(state_conv[l], ((0, 0), (8 - (CONV_W - 1), 0), (0, 0)))
        qkvn = _gdn_prep(qkv.reshape(db, ts, GDN_QKV), hist, cw[l], ts).reshape(db * ts, GDN_QKV)
        o_gdn, s_new = _gdn(qkvn, gate, sm, alog_b[l], dtb_b[l], gdn_g, state_all,
                            l * (db // seq_per_chunk), db // seq_per_chunk, 1, ts, 1)
        new_sub = jnp.pad(ncmp.reshape(db, 1, ts * NSA_KV_W), ((0, 0), (0, 7), (0, SUB_W - ts * NSA_KV_W)))
        ynew = _compress(new_sub.reshape(1, db * 8, SUB_W), pe2[l:l + 1], wlo[l:l + 1], whi[l:l + 1],
                         512).reshape(db, 8, 2 * NSA_KV_W)
        o_nsa = _nsa_sample(page_table, nq, sm, ynew, nslc, nwin, win_cache_t.reshape(DEPTH * db, NSA_KV_W, wb),
                            nsa_g, y_pool, slc_pool_t, l, n_pool, past_len, ts)
        xs = _out_proj(xs, o_gdn, o_nsa, w_out_b[l], 512)
        xs = _ffn(xs, g_ffn, w_ffn_in_b[l], w_ffn_out_b[l], 1024)
        outs[1].append(ncmp.reshape((db, ts) + kv_row))
        outs[3].append(nslc.reshape((db, ts) + kv_row))
        outs[5].append(jnp.transpose(nwin.reshape(db, ts, NSA_KV_W), (0, 2, 1)))
        outs[7].append(s_new)
        outs[9].append(jnp.concatenate([state_conv[l], qkv.reshape(db, ts, GDN_QKV)], axis=1)[:, -(CONV_W - 1):])

    y_prompt = _final_norm(xp, norm_final[None, :], 512).reshape(bp, tp, D_MODEL)
    y_sample = _final_norm(xs, norm_final[None, :], 512).reshape(db, ts, D_MODEL)
    res = [jnp.stack(o) for o in outs]
    res[5] = _window_rows(win_cache_t.reshape(DEPTH * db, NSA_KV_W, wb),
                          res[5].reshape(DEPTH * db, NSA_KV_W, ts)).reshape(DEPTH, db, NSA_KV_W, wb)
    for i in (0, 2, 4, 5):
        res[i] = _row_major_kv(res[i])
    return (y_prompt, y_sample) + tuple(res)
```

```python
import functools

import numpy as np
import jax
import jax.numpy as jnp
from jax import lax
from jax.experimental import pallas as pl
from jax.experimental.pallas import tpu as pltpu

F32 = jnp.float32
BF16 = jnp.bfloat16

D_MODEL = 1024
DEPTH = 4
PAGE_SIZE = 128
GDN_HEADS = 4
GDN_DK = 128
GDN_DV = 128
GDN_QKV = GDN_HEADS * (2 * GDN_DK + GDN_DV)
CONV_W = 4
NSA_HEADS = 8
NSA_KV_HEADS = 2
NSA_HD = 64
NSA_REP = NSA_HEADS // NSA_KV_HEADS
CMP_LEN = 32
CMP_STRIDE = 16
SEL_BLOCK = 64
N_SEL = 16
WINDOW = 512
NSA_KV_W = 2 * NSA_KV_HEADS * NSA_HD
D_FF = (8 * D_MODEL + 3 * 256 - 1) // (3 * 256) * 256
NEG_INF = -1e30
LOG2E = 1.4426950408889634
FORCE_BONUS = 1e4
EPS = 1e-6

LANE = 128
GDN_CHUNK = 128
GDN_NCH = 2
INV_BLOCK = 16
Q_BLOCK = 128
SLC_TILE = 512
WIN_TILES = WINDOW // Q_BLOCK + 1
SUB_W = CMP_STRIDE * NSA_KV_W
SUB_PP = PAGE_SIZE // CMP_STRIDE
POOL_SEQS = 4
SAMPLE_SEQS = 4
VMEM_LIMIT = 56 * 1024 * 1024

C_QKV, C_GATE, C_NQ, C_KV, C_SMALL = 0, 1536, 2048, 2560, 3328
N_PROJ = 3456
SM_B, SM_A, SM_G = 0, 4, 8

_NT = (((1,), (1,)), ((), ()))


def _params(sem):
    return pltpu.CompilerParams(dimension_semantics=sem, vmem_limit_bytes=VMEM_LIMIT)


def _dot(a, b):
    return jnp.dot(a.astype(BF16), b.astype(BF16), preferred_element_type=F32)


def _dot_nt(a, b):
    return lax.dot_general(a.astype(BF16), b.astype(BF16), _NT, preferred_element_type=F32)


def _split2(x):
    hi = x.astype(BF16)
    lo = (x - hi.astype(F32)).astype(BF16)
    return hi, lo


def _split3(x):
    hi = x.astype(BF16)
    r = x - hi.astype(F32)
    mid = r.astype(BF16)
    lo = (r - mid.astype(F32)).astype(BF16)
    return hi, mid, lo


def _mm_split(a, b):
    ah, al = a
    bh, bl = b
    d = functools.partial(jnp.dot, preferred_element_type=F32)
    return d(jnp.concatenate([ah, al], axis=1), jnp.concatenate([bh, bh], axis=0)) + d(ah, bl)


def _dotx(a, b):
    return _mm_split(_split2(a), _split2(b))


def _dot01_l(m01, x):
    m = m01.astype(BF16)
    hi, mid, lo = _split3(x)
    d = functools.partial(jnp.dot, preferred_element_type=F32)
    return d(jnp.concatenate([m, m], axis=1), jnp.concatenate([hi, mid], axis=0)) + d(m, lo)


def _dot01_r(x, m01):
    m = m01.astype(BF16)
    hi, mid, lo = _split3(x)
    d = functools.partial(jnp.dot, preferred_element_type=F32)
    return d(hi, m) + (d(mid, m) + d(lo, m))


def _iota(shape, dim):
    return lax.broadcasted_iota(jnp.int32, shape, dim)


def _idiv(x, d):
    sh = int(d).bit_length() - 1
    assert (1 << sh) == d
    return lax.shift_right_logical(x, jnp.int32(sh))


def _rms_rows(x, g):
    return x * lax.rsqrt(jnp.mean(x * x, axis=-1, keepdims=True) + EPS) * g


def _proj_kernel(x_ref, g_ref, w_ref, *rest, with_t):
    if with_t:
        wt_ref = rest[0]
        rest = rest[1:]
    qkv_ref, gate_ref, nq_ref, cmp_ref, slc_ref, win_ref, sm_ref = rest[:7]
    h = _rms_rows(x_ref[...], g_ref[...]).astype(BF16)
    d = functools.partial(jnp.dot, preferred_element_type=F32)
    qkv_ref[...] = d(h, w_ref[:, C_QKV:C_GATE])
    gate_ref[...] = d(h, w_ref[:, C_GATE:C_NQ])
    nq_ref[...] = d(h, w_ref[:, C_NQ:C_KV])
    kv = d(h, w_ref[:, C_KV:C_SMALL])
    cmp_ref[...] = kv[:, 0:NSA_KV_W]
    slc_ref[...] = kv[:, NSA_KV_W:2 * NSA_KV_W]
    win_ref[...] = kv[:, 2 * NSA_KV_W:3 * NSA_KV_W]
    sm_ref[...] = d(h, w_ref[:, C_SMALL:N_PROJ])
    if with_t:
        cmpt_ref, slct_ref, wint_ref, nqt_ref, sk_ref, sv_ref, wk_ref, wv_ref = rest[7:]
        kvt = lax.dot_general(wt_ref[...], h, _NT, preferred_element_type=F32)
        cmpt_ref[0] = kvt[0:NSA_KV_W]
        slct = kvt[NSA_KV_W:2 * NSA_KV_W]
        wint = kvt[2 * NSA_KV_W:3 * NSA_KV_W]
        slct_ref[0] = slct
        wint_ref[0] = wint
        nqt_ref[0] = kvt[3 * NSA_KV_W:3 * NSA_KV_W + 512]
        half = NSA_KV_W // 2
        sk_ref[0, 0] = kv[:, NSA_KV_W:NSA_KV_W + half].astype(BF16)
        sv_ref[0, 0] = slct[half:].astype(BF16)
        for j in range(wk_ref.shape[1]):
            rows = slice(j * Q_BLOCK, (j + 1) * Q_BLOCK)
            wk_ref[0, j] = kv[rows, 2 * NSA_KV_W:2 * NSA_KV_W + half].astype(BF16)
            wv_ref[0, j] = wint[half:, rows].astype(BF16)


def _proj(x, g, w, wt, tm, batch=None):
    m = x.shape[0]
    tm = min(tm, m)
    with_t = wt is not None
    widths = (GDN_QKV, 512, 512, NSA_KV_W, NSA_KV_W, NSA_KV_W, LANE)
    in_specs = [pl.BlockSpec((tm, D_MODEL), lambda i: (i, 0)),
                pl.BlockSpec((1, D_MODEL), lambda i: (0, 0)),
                pl.BlockSpec((D_MODEL, N_PROJ), lambda i: (0, 0))]
    out_specs = [pl.BlockSpec((tm, wd), lambda i: (i, 0)) for wd in widths]
    out_shape = [jax.ShapeDtypeStruct((m, wd), F32) for wd in widths]
    args = [x, g, w]
    if with_t:
        t_len = m // batch
        nt = t_len // tm
        assert tm == SLC_TILE and t_len % tm == 0
        in_specs.append(pl.BlockSpec((wt.shape[0], D_MODEL), lambda i: (0, 0)))
        args.append(wt)
        tmap = lambda i: (i // nt, 0, i % nt)
        tile = lambda i: (i // nt, i % nt, 0, 0)
        half = NSA_KV_W // 2
        nwt = tm // Q_BLOCK
        out_specs += [pl.BlockSpec((1, NSA_KV_W, tm), tmap)] * 3 + [pl.BlockSpec((1, 512, tm), tmap)]
        out_shape += [jax.ShapeDtypeStruct((batch, NSA_KV_W, t_len), F32)] * 3
        out_shape += [jax.ShapeDtypeStruct((batch, 512, t_len), F32)]
        out_specs += [pl.BlockSpec((1, 1, tm, half), tile),
                      pl.BlockSpec((1, 1, half, tm), tile),
                      pl.BlockSpec((1, nwt, Q_BLOCK, half), tile),
                      pl.BlockSpec((1, nwt, half, Q_BLOCK), tile)]
        out_shape += [jax.ShapeDtypeStruct((batch, nt, tm, half), BF16),
                      jax.ShapeDtypeStruct((batch, nt, half, tm), BF16),
                      jax.ShapeDtypeStruct((batch, t_len // Q_BLOCK, Q_BLOCK, half), BF16),
                      jax.ShapeDtypeStruct((batch, t_len // Q_BLOCK, half, Q_BLOCK), BF16)]
    return pl.pallas_call(
        functools.partial(_proj_kernel, with_t=with_t),
        grid=(m // tm,),
        in_specs=in_specs,
        out_specs=out_specs,
        out_shape=out_shape,
        compiler_params=_params(("parallel",)),
    )(*args)


def _out_kernel(x_ref, a1_ref, a2_ref, w_ref, o_ref):
    half = w_ref.shape[0] // 2
    o_ref[...] = x_ref[...] + (_dot(a1_ref[...], w_ref[:half, :]) + _dot(a2_ref[...], w_ref[half:, :]))


def _out_proj(x, a1, a2, w, tm):
    m = x.shape[0]
    tm = min(tm, m)
    return pl.pallas_call(
        _out_kernel,
        grid=(m // tm,),
        in_specs=[pl.BlockSpec((tm, D_MODEL), lambda i: (i, 0)),
                  pl.BlockSpec((tm, 512), lambda i: (i, 0)),
                  pl.BlockSpec((tm, 512), lambda i: (i, 0)),
                  pl.BlockSpec((D_MODEL, D_MODEL), lambda i: (0, 0))],
        out_specs=pl.BlockSpec((tm, D_MODEL), lambda i: (i, 0)),
        out_shape=jax.ShapeDtypeStruct((m, D_MODEL), F32),
        compiler_params=_params(("parallel",)),
    )(x, a1, a2, w)


def _ffn_kernel(x_ref, g_ref, wg_ref, wu_ref, wo_ref, o_ref, h_scr, acc_scr):
    f = pl.program_id(1)

    @pl.when(f == 0)
    def _():
        h_scr[...] = _rms_rows(x_ref[...], g_ref[...]).astype(BF16)
        acc_scr[...] = jnp.zeros_like(acc_scr)

    h = h_scr[...]
    gt = jnp.dot(h, wg_ref[...], preferred_element_type=F32)
    up = jnp.dot(h, wu_ref[...], preferred_element_type=F32)
    act = (gt * jax.nn.sigmoid(gt)) * up
    acc_scr[...] += jnp.dot(act.astype(BF16), wo_ref[...], preferred_element_type=F32)

    @pl.when(f == pl.num_programs(1) - 1)
    def _():
        o_ref[...] = x_ref[...] + acc_scr[...]


def _ffn(x, g, w_in, w_out, tm):
    m = x.shape[0]
    tm = min(tm, m)
    nf = 2
    tf = D_FF // nf
    return pl.pallas_call(
        _ffn_kernel,
        grid=(m // tm, nf),
        in_specs=[pl.BlockSpec((tm, D_MODEL), lambda i, f: (i, 0)),
                  pl.BlockSpec((1, D_MODEL), lambda i, f: (0, 0)),
                  pl.BlockSpec((D_MODEL, tf), lambda i, f: (0, f)),
                  pl.BlockSpec((D_MODEL, tf), lambda i, f: (0, nf + f)),
                  pl.BlockSpec((tf, D_MODEL), lambda i, f: (f, 0))],
        out_specs=pl.BlockSpec((tm, D_MODEL), lambda i, f: (i, 0)),
        out_shape=jax.ShapeDtypeStruct((m, D_MODEL), F32),
        scratch_shapes=[pltpu.VMEM((tm, D_MODEL), BF16), pltpu.VMEM((tm, D_MODEL), F32)],
        compiler_params=_params(("parallel", "arbitrary")),
    )(x, g, w_in, w_in, w_out)


def _window_rows_kernel(c_ref, n_ref, o_ref):
    wb = c_ref.shape[-1]
    ts = n_ref.shape[-1]
    for j in range(c_ref.shape[0]):
        o_ref[j] = pltpu.roll(c_ref[j], wb - ts, 1)
        o_ref[j, :, wb - ts:wb] = n_ref[j]


def _window_rows(cache_t, new_t):
    n, f, wb = cache_t.shape
    ts = new_t.shape[-1]
    assert wb >= ts
    nb = 4 if n % 4 == 0 else 1
    return pl.pallas_call(
        _window_rows_kernel,
        grid=(n // nb,),
        in_specs=[pl.BlockSpec((nb, f, wb), lambda i: (i, 0, 0)),
                  pl.BlockSpec((nb, f, ts), lambda i: (i, 0, 0))],
        out_specs=pl.BlockSpec((nb, f, wb), lambda i: (i, 0, 0)),
        out_shape=jax.ShapeDtypeStruct((n, f, wb), F32),
        compiler_params=_params(("parallel",)),
    )(cache_t, new_t)


def _final_norm_kernel(x_ref, g_ref, o_ref):
    o_ref[...] = _rms_rows(x_ref[...], g_ref[...])


def _final_norm(x, g, tm):
    m = x.shape[0]
    tm = min(tm, m)
    return pl.pallas_call(
        _final_norm_kernel,
        grid=(m // tm,),
        in_specs=[pl.BlockSpec((tm, D_MODEL), lambda i: (i, 0)),
                  pl.BlockSpec((1, D_MODEL), lambda i: (0, 0))],
        out_specs=pl.BlockSpec((tm, D_MODEL), lambda i: (i, 0)),
        out_shape=jax.ShapeDtypeStruct((m, D_MODEL), F32),
        compiler_params=_params(("parallel",)),
    )(x, g)


def _prep_kernel(raw_ref, hist_ref, cw_ref, o_ref, ext_scr, *, tc):
    @pl.when(pl.program_id(1) == 0)
    def _():
        ext_scr[0:8, :] = hist_ref[0]

    ext_scr[8:8 + tc, :] = raw_ref[0]
    for s in range(GDN_QKV // LANE):
        o_ref[0, :, s * LANE:(s + 1) * LANE] = _conv_slab(ext_scr, cw_ref, tc, s)
    tail = ext_scr[tc:tc + 8, :]
    ext_scr[0:8, :] = tail


def _gdn_prep(raw, hist, cw, tc):
    b, t, _ = raw.shape
    return pl.pallas_call(
        functools.partial(_prep_kernel, tc=tc),
        grid=(b, t // tc),
        in_specs=[pl.BlockSpec((1, tc, GDN_QKV), lambda i, j: (i, j, 0)),
                  pl.BlockSpec((1, 8, GDN_QKV), lambda i, j: (i, 0, 0)),
                  pl.BlockSpec((8, GDN_QKV), lambda i, j: (0, 0))],
        out_specs=pl.BlockSpec((1, tc, GDN_QKV), lambda i, j: (i, j, 0)),
        out_shape=jax.ShapeDtypeStruct((b, t, GDN_QKV), F32),
        scratch_shapes=[pltpu.VMEM((tc + 8, GDN_QKV), F32)],
        compiler_params=_params(("parallel", "arbitrary")),
    )(raw, hist, cw)


def _tri_inv(mats, ri, ci):
    c = mats[0].shape[0]
    eye = (ri == ci).astype(F32)
    bd = _idiv(ri, INV_BLOCK) == _idiv(ci, INV_BLOCK)
    ad = [jnp.where(bd, a, 0.0) for a in mats]
    ao = [a - d for a, d in zip(mats, ad)]
    split = lambda xs: [_split2(x) for x in xs]
    mm = lambda xs, ys: [_mm_split(x, y) for x, y in zip(xs, ys)]
    pw_s = split(ad)
    td = [eye - d for d in ad]
    k = 2
    while k < INV_BLOCK:
        pw_s = split(mm(pw_s, pw_s))
        td = [t + x for t, x in zip(td, mm(split(td), pw_s))]
        k *= 2
    td_s = split(td)
    n = mm(td_s, split(ao))
    tn = [eye - x for x in n]
    pw_s = split(n)
    k = 2
    while k < c // INV_BLOCK:
        pw_s = split(mm(pw_s, pw_s))
        tn = [t + x for t, x in zip(tn, mm(split(tn), pw_s))]
        k *= 2
    return mm(split(tn), td_s)


def _softplus(x):
    return jnp.maximum(x, 0.0) + jnp.log1p(jnp.exp(-jnp.abs(x)))


def _conv_slab(ext_ref, cw_ref, rows, s):
    cols = slice(s * LANE, (s + 1) * LANE)
    y = jnp.zeros((rows, LANE), F32)
    for j in range(CONV_W):
        y = y + ext_ref[pl.ds(8 - (CONV_W - 1) + j, rows), cols] * cw_ref[j:j + 1, cols]
    y = y * jax.nn.sigmoid(y)
    if s < 2 * GDN_HEADS:
        y = y * lax.rsqrt(jnp.sum(y * y, axis=-1, keepdims=True) + EPS)
        if s < GDN_HEADS:
            y = y * (GDN_DK ** -0.5)
    return y


def _gdn_kernel(qkv_ref, gate_ref, sm_ref, alog_ref, dtb_ref, ng_ref, mask_ref, s0_ref, *rest, ls, nch, fused):
    c = GDN_CHUNK
    nseq = c // ls
    assert nseq == 1 or nch == 1
    heads = range(GDN_HEADS)
    pairs = [(ch, h) for ch in range(nch) for h in heads]
    if fused:
        hist_ref, cw_ref, o_ref, sout_ref, ext_scr = rest
    else:
        o_ref, sout_ref = rest

    @pl.when(pl.program_id(1) == 0)
    def _():
        sout_ref[...] = s0_ref[...]
        if fused:
            ext_scr[0:8, :] = hist_ref[0]

    if fused:
        rows = nch * c
        ext_scr[8:8 + rows, :] = qkv_ref[...]
        prepped = [_conv_slab(ext_scr, cw_ref, rows, s) for s in range(GDN_QKV // LANE)]
        tail = ext_scr[rows:rows + 8, :]
        ext_scr[0:8, :] = tail

    ri = _iota((c, c), 0)
    ci = _iota((c, c), 1)
    same = _idiv(ri, ls) == _idiv(ci, ls)
    lower = same & (ci <= ri)
    strict = same & (ci < ri)
    lo_m = mask_ref[0]
    same_m = mask_ref[1]
    sm = [sm_ref[ch * c:(ch + 1) * c, :] for ch in range(nch)]

    def slab(ch, s):
        if fused:
            return prepped[s][ch * c:(ch + 1) * c]
        return qkv_ref[ch * c:(ch + 1) * c, s * LANE:(s + 1) * LANE]

    q = [slab(ch, h) for ch, h in pairs]
    k = [slab(ch, GDN_HEADS + h) for ch, h in pairs]
    v = [slab(ch, 2 * GDN_HEADS + h) for ch, h in pairs]
    neg_a = -jnp.exp(alog_ref[0:1, :])
    beta_s = [jax.nn.sigmoid(x) for x in sm]
    g_s = [neg_a * _softplus(x + dtb_ref[0:1, :]) for x in sm]
    beta = [jnp.broadcast_to(beta_s[ch][:, SM_B + h:SM_B + h + 1], (c, LANE)) for ch, h in pairs]
    g_col = [jnp.broadcast_to(g_s[ch][:, SM_A + h:SM_A + h + 1], (c, LANE)) for ch, h in pairs]
    n = range(len(pairs))
    dcy =[_dot01_l(lo_m, g_col[i]) for i in n]
    dcy_row = [x.T for x in dcy]
    if nseq == 1:
        dtot = [jnp.broadcast_to(x[c - 1:c, :], (c, LANE)) for x in dcy]
    else:
        dtot = [_dot01_l(same_m, g_col[i]) for i in n]
    dm = [jnp.where(lower, jnp.exp(jnp.where(lower, dcy[i] - dcy_row[i], 0.0)), 0.0) for i in n]
    kb = [k[i] * beta[i] for i in n]
    kk = [_dot_nt(jnp.concatenate([kb[i], q[i]], axis=0), k[i]) for i in n]
    a_mat = [jnp.where(strict, kk[i][0:c] * dm[i], 0.0) for i in n]
    attn = [jnp.where(lower, kk[i][c:2 * c] * dm[i], 0.0) for i in n]
    edc = [jnp.exp(dcy[i]) for i in n]
    t_inv = _tri_inv(a_mat, ri, ci)
    uw = [_dotx(t_inv[i], jnp.concatenate([v[i] * beta[i], kb[i] * edc[i]], axis=1)) for i in n]
    u = [x[:, 0:LANE] for x in uw]
    w = [x[:, LANE:2 * LANE] for x in uw]
    qd = [q[i] * edc[i] for i in n]
    kdt = [(k[i] * jnp.exp(dtot[i] - dcy[i])).T for i in n]
    gl = [jnp.exp(dtot[i]) for i in n]
    o_all = []
    if nseq == 1:
        s_cur = [sout_ref[0, h] for h in heads]
        for ch in range(nch):
            ix = [ch * GDN_HEADS + h for h in heads]
            ws = [_dot(jnp.concatenate([w[i], qd[i]], axis=0), s_cur[h]) for h, i in zip(heads, ix)]
            v_new = [u[i] - ws[h][0:c] for h, i in zip(heads, ix)]
            o_all += [ws[h][c:2 * c] + _dot(attn[i], v_new[h]) for h, i in zip(heads, ix)]
            s_cur = [s_cur[h] * gl[i][0:1, :] + _dot(kdt[i], v_new[h]) for h, i in zip(heads, ix)]
        for h in heads:
            sout_ref[0, h] = s_cur[h]
    else:
        for h in heads:
            vn, oq = [], []
            for s in range(nseq):
                s_old = sout_ref[s, h]
                rows = slice(s * ls, (s + 1) * ls)
                vn.append(u[h][rows] - _dot(w[h][rows], s_old))
                oq.append(_dot(qd[h][rows], s_old))
            v_new = jnp.concatenate(vn, axis=0)
            o_all.append(jnp.concatenate(oq, axis=0) + _dot(attn[h], v_new))
            for s in range(nseq):
                kdt_s = jnp.where(_idiv(ci, ls) == s, kdt[h], 0.0)
                sout_ref[s, h] = sout_ref[s, h] * gl[h][s * ls:s * ls + 1, :] + _dot(kdt_s, v_new)
    for i, (ch, h) in enumerate(pairs):
        o = o_all[i]
        o = o * lax.rsqrt(jnp.mean(o * o, axis=-1, keepdims=True) + EPS) * ng_ref[...]
        gt = gate_ref[ch * c:(ch + 1) * c, h * LANE:(h + 1) * LANE]
        o_ref[ch * c:(ch + 1) * c, h * LANE:(h + 1) * LANE] = o * (gt * jax.nn.sigmoid(gt))


def _gdn(qkv, gate, sm, alog_b, dtb_b, ng, s0, s0_off, nb, nt, ls, nch, prep=None):
    m = qkv.shape[0]
    c = GDN_CHUNK * nch
    nseq = GDN_CHUNK // ls
    row = lambda i, j: (i * nt + j, 0)
    n_state = nb * nseq
    in_specs = [pl.BlockSpec((c, GDN_QKV), row),
                pl.BlockSpec((c, 512), row),
                pl.BlockSpec((c, LANE), row),
                pl.BlockSpec((8, LANE), lambda i, j: (0, 0)),
                pl.BlockSpec((8, LANE), lambda i, j: (0, 0)),
                pl.BlockSpec((1, LANE), lambda i, j: (0, 0)),
                pl.BlockSpec((2, GDN_CHUNK, GDN_CHUNK), lambda i, j: (0, 0, 0)),
                pl.BlockSpec((nseq, GDN_HEADS, GDN_DK, GDN_DV), lambda i, j: (s0_off + i, 0, 0, 0))]
    idx = np.arange(GDN_CHUNK)
    same_np = (idx[:, None] // ls) == (idx[None, :] // ls)
    masks = jnp.asarray(np.stack([same_np & (idx[None, :] <= idx[:, None]), same_np]), BF16)
    args = [qkv, gate, sm, alog_b, dtb_b, ng, masks, s0]
    scratch = []
    if prep is not None:
        in_specs += [pl.BlockSpec((1, 8, GDN_QKV), lambda i, j: (i, 0, 0)),
                     pl.BlockSpec((8, GDN_QKV), lambda i, j: (0, 0))]
        args += list(prep)
        scratch = [pltpu.VMEM((c + 8, GDN_QKV), F32)]
    return pl.pallas_call(
        functools.partial(_gdn_kernel, ls=ls, nch=nch, fused=prep is not None),
        grid=(nb, nt),
        in_specs=in_specs,
        out_specs=[pl.BlockSpec((c, 512), row),
                   pl.BlockSpec((nseq, GDN_HEADS, GDN_DK, GDN_DV), lambda i, j: (i, 0, 0, 0))],
        out_shape=[jax.ShapeDtypeStruct((m, 512), F32),
                   jax.ShapeDtypeStruct((n_state, GDN_HEADS, GDN_DK, GDN_DV), F32)],
        scratch_shapes=scratch,
        compiler_params=_params(("parallel", "arbitrary")),
    )(*args)


def _cmp_kernel(x_ref, pe_ref, wlo_ref, whi_ref, y_ref):
    x = x_ref[0]
    y_ref[0, :, 0:NSA_KV_W] = _dot(x + pe_ref[0, 0:1, :], wlo_ref[0])
    y_ref[0, :, NSA_KV_W:2 * NSA_KV_W] = _dot(x + pe_ref[0, 1:2, :], whi_ref[0])


def _compress(x, pe, wlo, whi, tr):
    nl, r, _ = x.shape
    tr = min(tr, r)
    return pl.pallas_call(
        _cmp_kernel,
        grid=(nl, r // tr),
        in_specs=[pl.BlockSpec((1, tr, SUB_W), lambda l, i: (l, i, 0)),
                  pl.BlockSpec((1, 2, SUB_W), lambda l, i: (l, 0, 0)),
                  pl.BlockSpec((1, SUB_W, NSA_KV_W), lambda l, i: (l, 0, 0)),
                  pl.BlockSpec((1, SUB_W, NSA_KV_W), lambda l, i: (l, 0, 0))],
        out_specs=pl.BlockSpec((1, tr, 2 * NSA_KV_W), lambda l, i: (l, i, 0)),
        out_shape=jax.ShapeDtypeStruct((nl, r, 2 * NSA_KV_W), F32),
        compiler_params=_params(("parallel", "parallel")),
    )(x, pe, wlo, whi)


def _cmp_pool_kernel(pt_ref, pef_ref, wcat_ref, *rest, npg):
    del pt_ref
    pages = rest[:npg]
    y_ref, x_scr, pb_scr = rest[npg:]

    @pl.when(pl.program_id(1) == 0)
    def _():
        for hf in range(2):
            wf = wcat_ref[0, :, hf].reshape(CMP_STRIDE * LANE, 2 * LANE)
            pb_scr[hf] = _dot(pef_ref[0, hf], wf)

    for j in range(npg):
        x = pages[j][0].T
        for hf in range(2):
            x_scr[hf, j * PAGE_SIZE:(j + 1) * PAGE_SIZE, :] = x[:, hf * LANE:(hf + 1) * LANE]
    n = npg * SUB_PP
    for hf in range(2):
        acc = jnp.zeros((n, 2 * LANE), F32)
        for l in range(0, CMP_STRIDE, 2):
            xl = jnp.concatenate([x_scr[hf, pl.ds(l + j, n, stride=CMP_STRIDE), :] for j in range(2)], axis=1)
            acc = acc + _dot(xl, wcat_ref[0, l:l + 2, hf].reshape(2 * LANE, 2 * LANE))
        y_ref[0, :, hf * LANE:(hf + 1) * LANE] = acc[:, 0:LANE] + pb_scr[hf, 0:1, 0:LANE]
        y_ref[0, :, NSA_KV_W + hf * LANE:NSA_KV_W + (hf + 1) * LANE] = acc[:, LANE:] + pb_scr[hf, 1:2, LANE:]


def _compress_pool(page_table, pages_t, pe, wlo, whi, nl, n_pool, seq_per_step):
    db, n_pages = page_table.shape
    assert db % seq_per_step == 0
    npg = seq_per_step * n_pages
    n = npg * SUB_PP
    halves = [slice(hf * LANE, (hf + 1) * LANE) for hf in range(2)]
    wcat = jnp.stack([jnp.concatenate([wlo[:, :, h, h], whi[:, :, h, h]], axis=-1) for h in halves], axis=2)
    pef = jnp.stack([pe[:, :, :, h].reshape(nl, 2, CMP_STRIDE * LANE) for h in halves], axis=1)
    pef = jnp.pad(pef, ((0, 0), (0, 0), (0, 6), (0, 0)))

    def page_map(k):
        s, p = divmod(k, n_pages)
        return lambda l, i, pt: (l * n_pool + pt[i * seq_per_step + s, p], 0, 0)

    in_specs = [pl.BlockSpec((1, 2, 8, CMP_STRIDE * LANE), lambda l, i, pt: (l, 0, 0, 0)),
                pl.BlockSpec((1, CMP_STRIDE, 2, LANE, 2 * LANE), lambda l, i, pt: (l, 0, 0, 0, 0))]
    in_specs += [pl.BlockSpec((1, NSA_KV_W, PAGE_SIZE), page_map(k)) for k in range(npg)]
    grid_spec = pltpu.PrefetchScalarGridSpec(
        num_scalar_prefetch=1,
        grid=(nl, db // seq_per_step),
        in_specs=in_specs,
        out_specs=pl.BlockSpec((1, n, 2 * NSA_KV_W), lambda l, i, pt: (l, i, 0)),
        scratch_shapes=[pltpu.VMEM((2, npg * PAGE_SIZE, LANE), F32),
                        pltpu.VMEM((2, 8, 2 * LANE), F32)])
    return pl.pallas_call(
        functools.partial(_cmp_pool_kernel, npg=npg),
        grid_spec=grid_spec,
        out_shape=jax.ShapeDtypeStruct((nl, db * n_pages * SUB_PP, 2 * NSA_KV_W), F32),
        compiler_params=_params(("parallel", "arbitrary")),
    )(page_table, pef, wcat, *([pages_t] * npg))


def _slope(h):
    return 2.0 ** (-8.0 * (h + 1) / NSA_HEADS)


def _select_blocks(imps, pq):
    ax = 0
    blk = _iota(imps[0].shape, ax)
    cur = _idiv(pq, SEL_BLOCK)
    forced = (blk == 0) | (blk == cur) | (blk == cur - 1)
    bonus = jnp.where(forced, FORCE_BONUS, 0.0)
    work = [jnp.where(blk <= cur, imp + bonus, NEG_INF) for imp in imps]
    idx = blk.astype(F32)
    sel = [jnp.zeros(w.shape, F32) for w in work]
    for _ in range(N_SEL):
        m = [jnp.max(w, axis=ax, keepdims=True) for w in work]
        first = [jnp.min(jnp.where(w == mi, idx, 2.0 * LANE), axis=ax, keepdims=True) for w, mi in zip(work, m)]
        hit = [idx == f for f in first]
        sel = [jnp.where(h, 1.0, s) for h, s in zip(hit, sel)]
        work = [jnp.where(h, -jnp.inf, w) for h, w in zip(hit, work)]
    return sel


def _select_blocks_rank(imps, pq, nb):
    qb = imps[0].shape[0]
    blk = _iota((qb, LANE), 1)
    cur = _idiv(pq, SEL_BLOCK)
    forced = (blk == 0) | (blk == cur) | (blk == cur - 1)
    bonus = jnp.where(forced, FORCE_BONUS, 0.0)
    score = [jnp.where(blk <= cur, imp + bonus, NEG_INF) for imp in imps]
    fill = jnp.full((LANE - len(imps) * qb, LANE), NEG_INF, F32)
    st = jnp.concatenate(score + [fill], axis=0).T[0:nb]
    jrow = _iota((nb, LANE), 0)
    cnt = jnp.zeros((nb, LANE), F32)
    for j in range(nb):
        row = st[j:j + 1, :]
        cnt = cnt + jnp.where(jrow > j, jnp.where(row >= st, 1.0, 0.0), jnp.where(row > st, 1.0, 0.0))
    sel_t = jnp.where(cnt < N_SEL, 1.0, 0.0)
    sel = jnp.concatenate([sel_t, jnp.zeros((LANE - nb, LANE), F32)], axis=0).T
    return [sel[g * qb:(g + 1) * qb] for g in range(len(imps))]


def _key_aug(t_col, lane, qblk):
    rel = (_idiv(t_col, SEL_BLOCK) - qblk).astype(F32)
    off = (t_col & (SEL_BLOCK - 1)).astype(F32)
    return jnp.where(lane < 4, jnp.where((lane & 1) == 0, rel, off), 0.0).astype(BF16)


def _nsa_prompt_kernel(qt_ref, smt_ref, y_ref, sk_ref, sv_ref, wk_ref, wv_ref, ngb_ref, o_ref,
                       kc_scr, vct_scr, selb_scr, m_scr, l_scr, acc_scr, *, t_len):
    i = pl.program_id(1)
    nsub = t_len // CMP_STRIDE
    gw = NSA_REP * Q_BLOCK
    half = NSA_KV_W // 2
    groups = range(NSA_KV_HEADS)

    @pl.when(i == 0)
    def _():
        y = y_ref[0]
        ckv = y[:, 0:NSA_KV_W] + pltpu.roll(y[:, NSA_KV_W:2 * NSA_KV_W], nsub - 1, 0)
        kc_scr[...] = ckv[:, 0:half].astype(BF16)
        vct_scr[...] = ckv[:, half:].T.astype(BF16)

    pos0 = i * Q_BLOCK
    qblk = pos0 // SEL_BLOCK
    pq_row = pos0 + _iota((1, Q_BLOCK), 1)
    pq_g = jnp.concatenate([pq_row] * NSA_REP, axis=1)
    qt = qt_ref[0] * (NSA_HD ** -0.5 * LOG2E)
    zeros = jnp.zeros((NSA_HD, gw), F32)
    slope_g, rq, rfull = [], [], []
    aug_row = _iota((LANE, gw), 0)
    for g in groups:
        hs = range(g * NSA_REP, (g + 1) * NSA_REP)
        qg = jnp.concatenate([qt[h * NSA_HD:(h + 1) * NSA_HD, :] for h in hs], axis=1)
        rq.append(jnp.concatenate([qg, zeros] if g == 0 else [zeros, qg], axis=0).astype(BF16))
        sl = jnp.concatenate([jnp.full((1, Q_BLOCK), _slope(h) * LOG2E, F32) for h in hs], axis=1)
        slope_g.append(sl)
        sl_hi = sl.astype(BF16).astype(F32)
        sl_p = jnp.where(aug_row < 2, sl_hi, sl - sl_hi)
        raug = jnp.where(aug_row < 4, jnp.where((aug_row & 1) == 0, SEL_BLOCK * sl_p, sl_p), 0.0).astype(BF16)
        rfull.append(jnp.concatenate([rq[g], raug], axis=0))

    n_col = _iota((nsub, gw), 0)
    ok = (n_col * CMP_STRIDE + (CMP_LEN - 1)) <= pq_g
    dist = pq_g.astype(F32) - ((n_col * CMP_STRIDE).astype(F32) + 0.5 * (CMP_LEN - 1))
    kc = kc_scr[...]
    s = [jnp.dot(kc, rq[g], preferred_element_type=F32) for g in groups]
    s = [jnp.where(ok, s[g] - slope_g[g] * dist, NEG_INF) for g in groups]
    m = [jnp.max(x, axis=0, keepdims=True) for x in s]
    e = [jnp.where(ok, jnp.exp2(s[g] - m[g]), 0.0) for g in groups]
    den = [jnp.sum(x, axis=0, keepdims=True) for x in e]
    p = [e[g] / jnp.where(den[g] > 0.0, den[g], 1.0) for g in groups]
    o_c = [jnp.dot(vct_scr[g * NSA_HD:(g + 1) * NSA_HD, :], p[g].astype(BF16), preferred_element_type=F32)
           for g in groups]
    psum = [sum(p[g][:, r * Q_BLOCK:(r + 1) * Q_BLOCK] for r in range(NSA_REP)) for g in groups]
    pool_t = (_iota((LANE, nsub), 0) == _idiv(_iota((LANE, nsub), 1), SEL_BLOCK // CMP_STRIDE)).astype(F32)
    imp_t = [_dot01_l(pool_t, psum[g]) for g in groups]

    sel = _select_blocks(imp_t, pq_row)
    for g in groups:
        selb_scr[g] = jnp.where(sel[g] > 0.5, 0.0, NEG_INF)
    used = [jnp.max(x, axis=1, keepdims=True) for x in sel]
    tile_of_blk = _idiv(_iota((LANE, 1), 0), SLC_TILE // SEL_BLOCK)

    m_scr[...] = jnp.full(m_scr.shape, NEG_INF, F32)
    l_scr[...] = jnp.zeros(l_scr.shape, F32)
    acc_scr[...] = jnp.zeros(acc_scr.shape, F32)
    n_tiles = (pos0 + Q_BLOCK + SLC_TILE - 1) // SLC_TILE
    blk_per_tile = SLC_TILE // SEL_BLOCK

    def slc_tile(kt, causal):
        k0 = kt * SLC_TILE
        t_col = k0 + _iota((SLC_TILE, LANE), 0)
        lhs = jnp.concatenate([sk_ref[0, kt], _key_aug(t_col, _iota((SLC_TILE, LANE), 1), qblk)], axis=1)
        bias = [jnp.concatenate(
            [jnp.broadcast_to(selb_scr[g, pl.ds(kt * blk_per_tile + j, 1), :], (SEL_BLOCK, Q_BLOCK))
             for j in range(blk_per_tile)], axis=0) for g in groups]
        if causal:
            bias = [jnp.where(t_col <= pq_row, b, NEG_INF) for b in bias]
        sc = [jnp.dot(lhs, rfull[g], preferred_element_type=F32) + jnp.concatenate([bias[g]] * NSA_REP, axis=1)
              for g in groups]
        m_prev = [m_scr[g, 0:1, :] for g in groups]
        m_new = [jnp.maximum(m_prev[g], jnp.max(sc[g], axis=0, keepdims=True)) for g in groups]
        alpha = [jnp.exp2(m_prev[g] - m_new[g]) for g in groups]
        pr = [jnp.exp2(sc[g] - m_new[g]) for g in groups]
        l_new = [alpha[g] * l_scr[g, 0:1, :] + jnp.sum(pr[g], axis=0, keepdims=True) for g in groups]
        pv = [jnp.dot(sv_ref[0, kt, g * NSA_HD:(g + 1) * NSA_HD, :], pr[g].astype(BF16),
                      preferred_element_type=F32) for g in groups]
        for g in groups:
            acc_scr[g] = alpha[g] * acc_scr[g] + pv[g]
            l_scr[g, 0:1, :] = l_new[g]
            m_scr[g, 0:1, :] = m_new[g]

    used_any = jnp.maximum(used[0], used[1])

    def body(kt, carry):
        @pl.when(jnp.max(jnp.where(tile_of_blk == kt, used_any, 0.0)) > 0.5)
        def _():
            slc_tile(kt, False)
        return carry

    lax.fori_loop(0, n_tiles - 1, body, 0)
    slc_tile(n_tiles - 1, True)

    wt0 = jnp.maximum(i - (WIN_TILES - 1), 0)
    wkeys = WIN_TILES * Q_BLOCK
    kw = jnp.concatenate([wk_ref[0, wt0 + j] for j in range(WIN_TILES)], axis=0)
    vw = jnp.concatenate([wv_ref[0, wt0 + j] for j in range(WIN_TILES)], axis=1)
    tw = wt0 * Q_BLOCK + _iota((wkeys, LANE), 0)
    lhs_w = jnp.concatenate([kw, _key_aug(tw, _iota((wkeys, LANE), 1), qblk)], axis=1)
    dist_w = pq_row - tw
    bias_w = jnp.where((dist_w >= 0) & (dist_w < WINDOW), 0.0, NEG_INF)
    bias_w = jnp.concatenate([bias_w] * NSA_REP, axis=1)
    sw = [jnp.dot(lhs_w, rfull[g], preferred_element_type=F32) + bias_w for g in groups]
    mw = [jnp.max(x, axis=0, keepdims=True) for x in sw]
    pw = [jnp.exp2(sw[g] - mw[g]) for g in groups]
    dw = [jnp.sum(x, axis=0, keepdims=True) for x in pw]
    o_w = [jnp.dot(vw[g * NSA_HD:(g + 1) * NSA_HD, :], pw[g].astype(BF16), preferred_element_type=F32) / dw[g]
           for g in groups]

    gates = jax.nn.sigmoid(smt_ref[0])
    outs = []
    for h in range(NSA_HEADS):
        g, r = divmod(h, NSA_REP)
        lanes = slice(r * Q_BLOCK, (r + 1) * Q_BLOCK)
        o_s = acc_scr[g][:, lanes] / l_scr[g, 0:1, lanes]
        c0 = SM_G + 3 * h
        o = gates[c0:c0 + 1] * o_c[g][:, lanes] + gates[c0 + 1:c0 + 2] * o_s + gates[c0 + 2:c0 + 3] * o_w[g][:, lanes]
        outs.append(o * lax.rsqrt(jnp.mean(o * o, axis=0, keepdims=True) + EPS) * ngb_ref[...])
    o_ref[...] = jnp.concatenate(outs, axis=0).T


def _nsa_prompt(nq_t, smt, y, sk, sv, wk, wv, ngb, b, t_len):
    nqb = t_len // Q_BLOCK
    nsub = t_len // CMP_STRIDE
    nst = t_len // SLC_TILE
    half = NSA_KV_W // 2
    gw = NSA_REP * Q_BLOCK
    whole = lambda bi, i: (bi, 0, 0, 0)
    return pl.pallas_call(
        functools.partial(_nsa_prompt_kernel, t_len=t_len),
        grid=(b, nqb),
        in_specs=[pl.BlockSpec((1, 512, Q_BLOCK), lambda bi, i: (bi, 0, i)),
                  pl.BlockSpec((1, 32, Q_BLOCK), lambda bi, i: (bi * nqb + i, 0, 0)),
                  pl.BlockSpec((1, nsub, 2 * NSA_KV_W), lambda bi, i: (bi, 0, 0)),
                  pl.BlockSpec((1, nst, SLC_TILE, half), whole),
                  pl.BlockSpec((1, nst, half, SLC_TILE), whole),
                  pl.BlockSpec((1, nqb, Q_BLOCK, half), whole),
                  pl.BlockSpec((1, nqb, half, Q_BLOCK), whole),
                  pl.BlockSpec((NSA_HD, Q_BLOCK), lambda bi, i: (0, 0))],
        out_specs=pl.BlockSpec((Q_BLOCK, 512), lambda bi, i: (bi * nqb + i, 0)),
        out_shape=jax.ShapeDtypeStruct((b * t_len, 512), F32),
        scratch_shapes=[pltpu.VMEM((nsub, half), BF16),
                        pltpu.VMEM((half, nsub), BF16),
                        pltpu.VMEM((NSA_KV_HEADS, LANE, Q_BLOCK), F32),
                        pltpu.VMEM((NSA_KV_HEADS, 8, gw), F32),
                        pltpu.VMEM((NSA_KV_HEADS, 8, gw), F32),
                        pltpu.VMEM((NSA_KV_HEADS, NSA_HD, gw), F32)],
        compiler_params=_params(("parallel", "arbitrary")),
    )(nq_t, smt, y, sk, sv, wk, wv, ngb)


def _softmax_rows(s):
    m = jnp.max(s, axis=-1, keepdims=True)
    p = jnp.exp(s - m)
    return p, jnp.sum(p, axis=-1, keepdims=True)


def _nsa_sample_kernel(pt_ref, q_ref, sm_ref, ypast_ref, ynew_ref, slcn_ref, winn_ref, winc_ref, ng_ref, *rest,
                       n_pages, past_len, dec_len, nseq):
    del pt_ref
    s_pages = rest[0:nseq * n_pages]
    o_ref = rest[nseq * n_pages]
    slc_scr, win_scr = rest[nseq * n_pages + 1:]
    ncp = n_pages * SUB_PP
    tk = slc_scr.shape[2]
    wk = win_scr.shape[2]
    wb = winc_ref.shape[2]
    qb = dec_len
    rows = NSA_REP * qb
    seqs = range(nseq)
    chains = [(s, g) for s in seqs for g in range(NSA_KV_HEADS)]
    tile4 = lambda x: jnp.concatenate([x] * NSA_REP, axis=0)

    def new_cols(ref, s):
        pad = jnp.zeros((PAGE_SIZE - qb, NSA_KV_W), F32)
        return jnp.concatenate([ref[s * qb:(s + 1) * qb, :], pad], axis=0).T.astype(BF16)

    ckv = []
    for s in seqs:
        y = jnp.concatenate([ypast_ref[s], ynew_ref[s]], axis=0)
        ckv.append((y[:, 0:NSA_KV_W] + pltpu.roll(y[:, NSA_KV_W:2 * NSA_KV_W], ncp + 7, 0))[0:ncp].astype(BF16))
        for p in range(n_pages):
            slc_scr[s, :, p * PAGE_SIZE:(p + 1) * PAGE_SIZE] = s_pages[s * n_pages + p][0].astype(BF16)
        slc_scr[s, :, past_len:tk] = new_cols(slcn_ref, s)
        win_scr[s, :, 0:wb] = winc_ref[s].astype(BF16)
        win_scr[s, :, wb:wk] = new_cols(winn_ref, s)

    def feat(ref, s, g, v):
        r0 = (v * NSA_KV_HEADS + g) * NSA_HD
        return ref[s, r0:r0 + NSA_HD, :]

    pq = past_len + _iota((qb, 1), 0)
    pq4 = tile4(pq)
    q = q_ref[...]
    qrows, slope = {}, {}
    for g in range(NSA_KV_HEADS):
        hs = range(g * NSA_REP, (g + 1) * NSA_REP)
        slope[g] = jnp.concatenate([jnp.full((qb, 1), _slope(h), F32) for h in hs], axis=0)
        for s in seqs:
            qs = q[s * qb:(s + 1) * qb]
            qrows[(s, g)] = jnp.concatenate([qs[:, h * NSA_HD:(h + 1) * NSA_HD] * (NSA_HD ** -0.5) for h in hs],
                                            axis=0).astype(BF16)

    n_ix = _iota((1, ncp), 1)
    ok = (n_ix * CMP_STRIDE + (CMP_LEN - 1)) <= pq4
    dist_c = pq4.astype(F32) - ((n_ix * CMP_STRIDE).astype(F32) + 0.5 * (CMP_LEN - 1))
    pool = (_idiv(_iota((ncp, LANE), 0), SEL_BLOCK // CMP_STRIDE) == _iota((ncp, LANE), 1)).astype(F32)
    sc = [jnp.where(ok, _dot_nt(qrows[c], ckv[c[0]][:, c[1] * NSA_HD:(c[1] + 1) * NSA_HD]) - slope[c[1]] * dist_c,
                    NEG_INF) for c in chains]
    mc = [jnp.max(x, axis=-1, keepdims=True) for x in sc]
    ec = [jnp.where(ok, jnp.exp(x - m), 0.0) for x, m in zip(sc, mc)]
    dc = [jnp.sum(x, axis=-1, keepdims=True) for x in ec]
    pc = [x / jnp.where(d > 0.0, d, 1.0) for x, d in zip(ec, dc)]
    o_c = [_dot(p, ckv[c[0]][:, (NSA_KV_HEADS + c[1]) * NSA_HD:(NSA_KV_HEADS + c[1] + 1) * NSA_HD])
           for p, c in zip(pc, chains)]
    imps = [_dot01_r(sum(p[r * qb:(r + 1) * qb] for r in range(NSA_REP)), pool) for p in pc]

    n_blk = (past_len + qb - 1) // SEL_BLOCK + 1
    sel = _select_blocks_rank(imps, pq, -(-n_blk // 8) * 8)

    dist_i = pq - _iota((1, tk), 1)
    dist_s = tile4(dist_i.astype(F32))
    e_tile = (_iota((LANE, tk), 0) == _idiv(_iota((LANE, tk), 1), SEL_BLOCK)).astype(BF16)
    on = [(jnp.dot(x.astype(BF16), e_tile, preferred_element_type=F32) > 0.5) & (dist_i >= 0) for x in sel]
    bias = [tile4(jnp.where(x, 0.0, NEG_INF)) for x in on]
    ss = [jnp.dot(qrows[c], feat(slc_scr, c[0], c[1], 0), preferred_element_type=F32)
          + (b - slope[c[1]] * dist_s) for c, b in zip(chains, bias)]
    ps = [_softmax_rows(x) for x in ss]
    o_s = [lax.dot_general(p.astype(BF16), feat(slc_scr, c[0], c[1], 1), _NT, preferred_element_type=F32) / l
           for (p, l), c in zip(ps, chains)]

    dist_wi = pq - (past_len - wb + _iota((1, wk), 1))
    bias_w = tile4(jnp.where((dist_wi >= 0) & (dist_wi < WINDOW), 0.0, NEG_INF))
    dist_w = tile4(dist_wi.astype(F32))
    sw = [jnp.dot(qrows[c], feat(win_scr, c[0], c[1], 0), preferred_element_type=F32)
          + (bias_w - slope[c[1]] * dist_w) for c in chains]
    pw = [_softmax_rows(x) for x in sw]
    o_w = [lax.dot_general(p.astype(BF16), feat(win_scr, c[0], c[1], 1), _NT, preferred_element_type=F32) / l
           for (p, l), c in zip(pw, chains)]

    gates = jax.nn.sigmoid(sm_ref[...])
    for s in seqs:
        gs = gates[s * qb:(s + 1) * qb]
        for h in range(NSA_HEADS):
            g, r = divmod(h, NSA_REP)
            i = s * NSA_KV_HEADS + g
            rr = slice(r * qb, (r + 1) * qb)
            c0 = SM_G + 3 * h
            o = gs[:, c0:c0 + 1] * o_c[i][rr] + gs[:, c0 + 1:c0 + 2] * o_s[i][rr] + gs[:, c0 + 2:c0 + 3] * o_w[i][rr]
            o = o * lax.rsqrt(jnp.mean(o * o, axis=-1, keepdims=True) + EPS) * ng_ref[...]
            o_ref[s * qb:(s + 1) * qb, h * NSA_HD:(h + 1) * NSA_HD] = o


def _nsa_sample(page_table, nq, sm, ynew, slc_new, win_new, win_cache_t, ng, y_past, slc_pool_t, layer, n_pool,
                past_len, dec_len):
    db, n_pages = page_table.shape
    wb = win_cache_t.shape[2]
    tk = past_len + PAGE_SIZE
    wk = wb + PAGE_SIZE
    ns = SAMPLE_SEQS
    assert db % ns == 0
    row = lambda b, pt: (b, 0)
    seq0 = layer * (db // ns)

    def page_map(k):
        s, p = divmod(k, n_pages)
        return lambda b, pt: (layer * n_pool + pt[b * ns + s, p], 0, 0)

    in_specs = [pl.BlockSpec((ns * dec_len, 512), row),
                pl.BlockSpec((ns * dec_len, LANE), row),
                pl.BlockSpec((ns, n_pages * SUB_PP, 2 * NSA_KV_W), lambda b, pt: (seq0 + b, 0, 0)),
                pl.BlockSpec((ns, 8, 2 * NSA_KV_W), lambda b, pt: (b, 0, 0)),
                pl.BlockSpec((ns * dec_len, NSA_KV_W), row),
                pl.BlockSpec((ns * dec_len, NSA_KV_W), row),
                pl.BlockSpec((ns, NSA_KV_W, wb), lambda b, pt: (seq0 + b, 0, 0)),
                pl.BlockSpec((1, NSA_HD), lambda b, pt: (0, 0))]
    in_specs += [pl.BlockSpec((1, NSA_KV_W, PAGE_SIZE), page_map(k)) for k in range(ns * n_pages)]
    grid_spec = pltpu.PrefetchScalarGridSpec(
        num_scalar_prefetch=1,
        grid=(db // ns,),
        in_specs=in_specs,
        out_specs=pl.BlockSpec((ns * dec_len, 512), row),
        scratch_shapes=[pltpu.VMEM((ns, NSA_KV_W, tk), BF16),
                        pltpu.VMEM((ns, NSA_KV_W, wk), BF16)])
    return pl.pallas_call(
        functools.partial(_nsa_sample_kernel, n_pages=n_pages, past_len=past_len, dec_len=dec_len, nseq=ns),
        grid_spec=grid_spec,
        out_shape=jax.ShapeDtypeStruct((db * dec_len, 512), F32),
        compiler_params=_params(("arbitrary",)),
    )(page_table, nq, sm, y_past, ynew, slc_new, win_new, win_cache_t, ng, *([slc_pool_t] * (ns * n_pages)))


def _reorder_proj(w):
    sizes = (GDN_QKV, GDN_HEADS, GDN_HEADS, GDN_HEADS * GDN_DV, NSA_HEADS * NSA_HD,
             NSA_KV_W, NSA_KV_W, NSA_KV_W, 3 * NSA_HEADS)
    off = [0] + [int(v) for v in np.cumsum(sizes)]
    seg = lambda i: w[:, :, off[i]:off[i + 1]].astype(BF16)
    pad = jnp.zeros(w.shape[:2] + (N_PROJ - off[-1],), BF16)
    return jnp.concatenate([seg(0), seg(3), seg(4), seg(5), seg(6), seg(7), seg(1), seg(2), seg(8), pad], axis=2)


def _cmp_weights(cmp_w):
    def half(w):
        rows = []
        for c in range(2):
            for g in range(NSA_KV_HEADS):
                j = c * NSA_KV_HEADS + g
                rows.append(jnp.pad(w[:, :, c], ((0, 0), (0, 0), (0, 0), (j * NSA_HD, NSA_KV_W - (j + 1) * NSA_HD))))
        return jnp.concatenate(rows, axis=2).astype(BF16)

    return half(cmp_w[:, :CMP_STRIDE]), half(cmp_w[:, CMP_STRIDE:])


def _cmp_pe(cmp_pe):
    nl = cmp_pe.shape[0]

    def half(p):
        return jnp.broadcast_to(p[:, :, :, None, :], (nl, CMP_STRIDE, 2, NSA_KV_HEADS, NSA_HD)).reshape(nl, SUB_W)

    return jnp.stack([half(cmp_pe[:, :CMP_STRIDE]), half(cmp_pe[:, CMP_STRIDE:])], axis=1)


def _small_t(sm, c):
    m = sm.shape[0]
    return jnp.transpose(sm[:, :32].reshape(m // c, c, 32), (0, 2, 1))


def _feature_major(a):
    nd = a.ndim
    perm = tuple(range(nd - 4)) + (nd - 3, nd - 2, nd - 1, nd - 4)
    t = jnp.transpose(a, perm)
    return t.reshape(t.shape[:nd - 4] + (NSA_KV_W, t.shape[-1]))


def _row_major_kv(a_t):
    lead = a_t.shape[:-2]
    n = len(lead)
    t = a_t.reshape(lead + (2, NSA_KV_HEADS, NSA_HD, a_t.shape[-1]))
    return jnp.transpose(t, tuple(range(n)) + (n + 3, n, n + 1, n + 2))


def kernel(x_prompt, x_sample, cache_cmp_kv, cache_slc_kv, page_table, cache_win_kv, state_gdn, state_conv,
           norm_mix, w_in, conv_w, gdn_a_log, gdn_dt_bias, gdn_norm, nsa_cmp_w, nsa_cmp_pe, nsa_norm,
           w_out, norm_ffn, w_ffn_in, w_ffn_out, norm_final):
    bp, tp, _ = x_prompt.shape
    db, ts, _ = x_sample.shape
    n_pool = cache_cmp_kv.shape[1]
    n_pages = page_table.shape[1]
    past_len = n_pages * PAGE_SIZE
    wb = cache_win_kv.shape[2]
    kv_row = (2, NSA_KV_HEADS, NSA_HD)
    wkeep = min(WINDOW, tp)
    assert GDN_CHUNK % ts == 0 and (db * ts) % GDN_CHUNK == 0 and tp % SLC_TILE == 0
    assert tp >= WIN_TILES * Q_BLOCK and db % POOL_SEQS == 0

    w_in_b = _reorder_proj(w_in)
    w_t = jnp.transpose(jnp.concatenate([w_in_b[:, :, C_KV:C_SMALL], w_in_b[:, :, C_NQ:C_KV]], axis=2),
                        (0, 2, 1))
    w_out_b = w_out.astype(BF16)
    w_ffn_in_b = w_ffn_in.astype(BF16)
    w_ffn_out_b = w_ffn_out.astype(BF16)
    cw = jnp.pad(conv_w, ((0, 0), (0, 8 - CONV_W), (0, 0)))
    lane_pad = ((0, 0), (SM_A, LANE - SM_A - GDN_HEADS))
    alog_b = jnp.pad(jnp.pad(gdn_a_log, lane_pad)[:, None, :], ((0, 0), (0, 7), (0, 0)))
    dtb_b = jnp.pad(jnp.pad(gdn_dt_bias, lane_pad)[:, None, :], ((0, 0), (0, 7), (0, 0)))
    wlo4, whi4 = _cmp_weights(nsa_cmp_w)
    wlo = wlo4.reshape(DEPTH, SUB_W, NSA_KV_W)
    whi = whi4.reshape(DEPTH, SUB_W, NSA_KV_W)
    pe2 = _cmp_pe(nsa_cmp_pe)

    cmp_pool_t = _feature_major(cache_cmp_kv).reshape(DEPTH * n_pool, NSA_KV_W, PAGE_SIZE)
    y_pool = _compress_pool(page_table, cmp_pool_t, pe2.reshape(DEPTH, 2, CMP_STRIDE, NSA_KV_W), wlo4, whi4,
                            DEPTH, n_pool, POOL_SEQS)
    y_pool = y_pool.reshape(DEPTH * db, n_pages * SUB_PP, 2 * NSA_KV_W)
    slc_pool_t = _feature_major(cache_slc_kv).reshape(DEPTH * n_pool, NSA_KV_W, PAGE_SIZE)
    win_cache_t = _feature_major(cache_win_kv)
    state_all = state_gdn.reshape(DEPTH * db, GDN_HEADS, GDN_DK, GDN_DV)
    seq_per_chunk = GDN_CHUNK // ts

    xp = x_prompt.reshape(bp * tp, D_MODEL)
    xs = x_sample.reshape(db * ts, D_MODEL)
    zero_hist = jnp.zeros((bp, 8, GDN_QKV), F32)
    zero_state = jnp.zeros((bp, GDN_HEADS, GDN_DK, GDN_DV), F32)
    outs = [[] for _ in range(10)]
    for l in range(DEPTH):
        g_mix = norm_mix[l][None, :]
        g_ffn = norm_ffn[l][None, :]
        gdn_g = gdn_norm[l][None, :]
        nsa_g = nsa_norm[l][None, :]

        (qkv, gate, _, ncmp, _, _, sm, cmp_t, slc_t, win_t, nq_t, sk, sv, wk, wv) = _proj(
            xp, g_mix, w_in_b[l], w_t[l], SLC_TILE, bp)
        smt = _small_t(sm, GDN_CHUNK)
        o_gdn, s_new = _gdn(qkv, gate, sm, alog_b[l], dtb_b[l], gdn_g, zero_state, 0,
                            bp, tp // (GDN_CHUNK * GDN_NCH), GDN_CHUNK, GDN_NCH, prep=(zero_hist, cw[l]))
        y = _compress(ncmp.reshape(1, bp * tp // CMP_STRIDE, SUB_W), pe2[l:l + 1], wlo[l:l + 1],
                      whi[l:l + 1], 512).reshape(bp, tp // CMP_STRIDE, 2 * NSA_KV_W)
        o_nsa = _nsa_prompt(nq_t, smt, y, sk, sv, wk, wv, jnp.broadcast_to(nsa_norm[l][:, None], (NSA_HD, Q_BLOCK)),
                            bp, tp)
        xp = _out_proj(xp, o_gdn, o_nsa, w_out_b[l], 512)
        xp = _ffn(xp, g_ffn, w_ffn_in_b[l], w_ffn_out_b[l], 1024)
        outs[0].append(cmp_t)
        outs[2].append(slc_t)
        outs[4].append(win_t[:, :, tp - wkeep:])
        outs[6].append(s_new)
        outs[8].append(qkv.reshape(bp, tp, GDN_QKV)[:, -(CONV_W - 1):])

        qkv, gate, nq, ncmp, nslc, nwin, sm = _proj(xs, g_mix, w_in_b[l], None, 512)
        hist = jnp.pad(state_conv[l], ((0, 0), (8 - (CONV_W - 1), 0), (0, 0)))
        qkvn = _gdn_prep(qkv.reshape(db, ts, GDN_QKV), hist, cw[l], ts).reshape(db * ts, GDN_QKV)
        o_gdn, s_new = _gdn(qkvn, gate, sm, alog_b[l], dtb_b[l], gdn_g, state_all,
                            l * (db // seq_per_chunk), db // seq_per_chunk, 1, ts, 1)
        new_sub = jnp.pad(ncmp.reshape(db, 1, ts * NSA_KV_W), ((0, 0), (0, 7), (0, SUB_W - ts * NSA_KV_W)))
        ynew = _compress(new_sub.reshape(1, db * 8, SUB_W), pe2[l:l + 1], wlo[l:l + 1], whi[l:l + 1],
                         512).reshape(db, 8, 2 * NSA_KV_W)
        o_nsa = _nsa_sample(page_table, nq, sm, ynew, nslc, nwin, win_cache_t.reshape(DEPTH * db, NSA_KV_W, wb),
                            nsa_g, y_pool, slc_pool_t, l, n_pool, past_len, ts)
        xs = _out_proj(xs, o_gdn, o_nsa, w_out_b[l], 512)
        xs = _ffn(xs, g_ffn, w_ffn_in_b[l], w_ffn_out_b[l], 1024)
        outs[1].append(ncmp.reshape((db, ts) + kv_row))
        outs[3].append(nslc.reshape((db, ts) + kv_row))
        outs[5].append(jnp.transpose(nwin.reshape(db, ts, NSA_KV_W), (0, 2, 1)))
        outs[7].append(s_new)
        outs[9].append(jnp.concatenate([state_conv[l], qkv.reshape(db, ts, GDN_QKV)], axis=1)[:, -(CONV_W - 1):])

    y_prompt = _final_norm(xp, norm_final[None, :], 512).reshape(bp, tp, D_MODEL)
    y_sample = _final_norm(xs, norm_final[None, :], 512).reshape(db, ts, D_MODEL)
    res = [jnp.stack(o) for o in outs]
    res[5] = _window_rows(win_cache_t.reshape(DEPTH * db, NSA_KV_W, wb),
                          res[5].reshape(DEPTH * db, NSA_KV_W, ts)).reshape(DEPTH, db, NSA_KV_W, wb)
    for i in (0, 2, 4, 5):
        res[i] = _row_major_kv(res[i])
    return (y_prompt, y_sample) + tuple(res)
```
